```python
import jax, jax.numpy as jnp
from jax import lax
import numpy as np

D_MODEL = 1024
BATCH = 2
SEQ = 8192
DEPTH = 4

N_MIXERS = 2
N_MLSTM_LAYERS = (DEPTH + N_MIXERS - 1) // N_MIXERS
N_ATTN_LAYERS = DEPTH // N_MIXERS

MLSTM_HEADS = 4
MLSTM_QK_DIM = D_MODEL // (2 * MLSTM_HEADS)
MLSTM_V_DIM = D_MODEL // MLSTM_HEADS
MLSTM_CHUNK = 128
MLSTM_N_GATES = 4 * MLSTM_HEADS
MLSTM_IN_WIDTH = 2 * MLSTM_HEADS * MLSTM_QK_DIM + 2 * D_MODEL + MLSTM_N_GATES
FGATE_BIAS_LO = 3.0
FGATE_BIAS_HI = 6.0

ATTN_HEAD_DIM = 64
ATTN_Q_HEADS = D_MODEL // ATTN_HEAD_DIM
ATTN_KV_HEADS = 4
ATTN_GROUP = ATTN_Q_HEADS // ATTN_KV_HEADS
WINDOW = 128
ATTN_BLOCK = 128
ATTN_IN_WIDTH = (ATTN_Q_HEADS + 2 * ATTN_KV_HEADS) * ATTN_HEAD_DIM
ROPE_THETA = 10000.0

D_FF = -(-(8 * D_MODEL) // (3 * 256)) * 256
EPS = 1e-6

kernel_name = "bidir_mlstm_swa_hybrid_trunk"


def rmsnorm(x, w):
    xf = x.astype(jnp.float32)
    y = xf * lax.rsqrt(jnp.mean(xf * xf, axis=-1, keepdims=True) + EPS)
    return (y * w.astype(jnp.float32)).astype(x.dtype)


def rope(x, positions):
    half = x.shape[-1] // 2
    inv_freq = ROPE_THETA ** (-jnp.arange(half, dtype=jnp.float32) / half)
    ang = positions.astype(jnp.float32)[..., None] * inv_freq
    cos = jnp.cos(ang)[:, :, None, :]
    sin = jnp.sin(ang)[:, :, None, :]
    xf = x.astype(jnp.float32)
    x1, x2 = xf[..., :half], xf[..., half:]
    return jnp.concatenate([x1 * cos - x2 * sin, x2 * cos + x1 * sin], axis=-1)


def mlstm_chunkwise(q, k, v, log_i, log_f):
    B, H, S, dk = q.shape
    dv = v.shape[-1]
    L = MLSTM_CHUNK
    NC = S // L
    q = q.reshape(B, H, NC, L, dk)
    k = k.reshape(B, H, NC, L, dk)
    v = v.reshape(B, H, NC, L, dv)
    log_i = log_i.reshape(B, H, NC, L)
    b = jnp.cumsum(log_f.reshape(B, H, NC, L), axis=-1)
    b_last = b[..., -1]

    a = b_last[..., None] - b + log_i
    a_max = jnp.max(a, axis=-1)
    w = jnp.exp(a - a_max[..., None])
    kv_chunk = jnp.einsum('bhcl,bhcld,bhcle->bhcde', w, k, v)
    k_chunk = jnp.einsum('bhcl,bhcld->bhcd', w, k)

    def step(carry, xs):
        C, n, m = carry
        kv_c, k_c, bl_c, am_c = xs
        m_new = jnp.maximum(bl_c + m, am_c)
        s_prev = jnp.exp(bl_c + m - m_new)
        s_cur = jnp.exp(am_c - m_new)
        C_new = s_prev[..., None, None] * C + s_cur[..., None, None] * kv_c
        n_new = s_prev[..., None] * n + s_cur[..., None] * k_c
        return (C_new, n_new, m_new), (C, n, m)

    init = (jnp.zeros((B, H, dk, dv), jnp.float32),
            jnp.zeros((B, H, dk), jnp.float32),
            jnp.zeros((B, H), jnp.float32))
    xs = (jnp.moveaxis(kv_chunk, 2, 0), jnp.moveaxis(k_chunk, 2, 0),
          jnp.moveaxis(b_last, 2, 0), jnp.moveaxis(a_max, 2, 0))
    _, (C_prev, n_prev, m_prev) = lax.scan(step, init, xs)
    C_prev = jnp.moveaxis(C_prev, 0, 2)
    n_prev = jnp.moveaxis(n_prev, 0, 2)
    m_prev = jnp.moveaxis(m_prev, 0, 2)

    tril = jnp.tril(jnp.ones((L, L), dtype=bool))
    D = b[..., :, None] - b[..., None, :] + log_i[..., None, :]
    D = jnp.where(tril, D, -jnp.inf)
    g = b + m_prev[..., None]
    m_t = jnp.maximum(g, jnp.max(D, axis=-1))
    s = jnp.einsum('bhcld,bhcsd->bhcls', q, k) * jnp.exp(D - m_t[..., None])
    inter = jnp.exp(g - m_t)
    num = (jnp.einsum('bhcls,bhcse->bhcle', s, v)
           + inter[..., None] * jnp.einsum('bhcld,bhcde->bhcle', q, C_prev))
    den = jnp.sum(s, axis=-1) + inter * jnp.einsum('bhcld,bhcd->bhcl', q, n_prev)
    den = jnp.maximum(jnp.abs(den), jnp.exp(-m_t))
    return (num / den[..., None]).reshape(B, H, S, dv)


def mlstm_mixer(h, w_in, b_gate, norm_w, w_out):
    B, S, _ = h.shape
    H, dk, dv = MLSTM_HEADS, MLSTM_QK_DIM, MLSTM_V_DIM
    proj = h @ w_in
    o1 = H * dk
    o2 = 2 * H * dk
    o3 = o2 + D_MODEL
    o4 = o3 + D_MODEL
    q = proj[..., :o1].reshape(B, S, H, dk).transpose(0, 2, 1, 3).astype(jnp.float32) * (dk ** -0.5)
    k = proj[..., o1:o2].reshape(B, S, H, dk).transpose(0, 2, 1, 3).astype(jnp.float32)
    v = proj[..., o2:o3].reshape(B, S, H, dv).transpose(0, 2, 1, 3).astype(jnp.float32)
    o_gate = jax.nn.sigmoid(proj[..., o3:o4].astype(jnp.float32))
    gates = (proj[..., o4:] + b_gate).astype(jnp.float32)
    gates = gates.reshape(B, S, 4, H).transpose(2, 0, 3, 1)
    i_fwd, f_fwd, i_bwd, f_bwd = gates[0], gates[1], gates[2], gates[3]

    h_fwd = mlstm_chunkwise(q, k, v, i_fwd, jax.nn.log_sigmoid(f_fwd))
    flip = lambda t: t[:, :, ::-1]
    h_bwd = flip(mlstm_chunkwise(flip(q), flip(k), flip(v), flip(i_bwd),
                                 flip(jax.nn.log_sigmoid(f_bwd))))
    hs = h_fwd + h_bwd
    hs = hs * lax.rsqrt(jnp.mean(hs * hs, axis=-1, keepdims=True) + EPS)
    hs = hs.transpose(0, 2, 1, 3).reshape(B, S, H * dv) * norm_w.astype(jnp.float32)
    return (hs * o_gate).astype(h.dtype) @ w_out


def window_attn_mixer(h, positions, w_in, sink, w_out):
    B, S, _ = h.shape
    Hq, Hkv, G, hd, L = ATTN_Q_HEADS, ATTN_KV_HEADS, ATTN_GROUP, ATTN_HEAD_DIM, ATTN_BLOCK
    NB = S // L
    proj = h @ w_in
    q = proj[..., :Hq * hd].reshape(B, S, Hq, hd)
    k = proj[..., Hq * hd:(Hq + Hkv) * hd].reshape(B, S, Hkv, hd)
    v = proj[..., (Hq + Hkv) * hd:].reshape(B, S, Hkv, hd)
    q = rope(q, positions).reshape(B, NB, L, Hkv, G, hd)
    k = rope(k, positions)

    def band(t):
        tp = jnp.pad(t, ((0, 0), (L, L), (0, 0), (0, 0))).reshape(B, NB + 2, L, Hkv, hd)
        return jnp.concatenate([tp[:, :-2], tp[:, 1:-1], tp[:, 2:]], axis=2)

    k_band = band(k)
    v_band = band(v).astype(jnp.float32)
    scores = jnp.einsum('bcqhgd,bckhd->bchgqk', q, k_band) * (hd ** -0.5)

    qi = jnp.arange(L)
    kj = jnp.arange(3 * L)
    in_window = jnp.abs(kj[None, :] - L - qi[:, None]) <= WINDOW
    k_pos = jnp.arange(NB)[:, None] * L - L + kj[None, :]
    k_valid = (k_pos >= 0) & (k_pos < S)
    mask = in_window[None, :, :] & k_valid[:, None, :]
    scores = jnp.where(mask[None, :, None, None], scores, -jnp.inf)

    sink_l = sink.astype(jnp.float32).reshape(Hkv, G)[None, None, :, :, None]
    m = jnp.maximum(jnp.max(scores, axis=-1), sink_l)
    p = jnp.exp(scores - m[..., None])
    denom = jnp.sum(p, axis=-1) + jnp.exp(sink_l - m)
    out = jnp.einsum('bchgqk,bckhd->bcqhgd', p, v_band)
    out = out / denom.transpose(0, 1, 4, 2, 3)[..., None]
    return out.reshape(B, S, Hq * hd).astype(h.dtype) @ w_out


def swiglu(h, w_in, w_out):
    gu = h @ w_in
    gate, up = gu[..., :D_FF], gu[..., D_FF:]
    return (jax.nn.silu(gate) * up) @ w_out


def setup_inputs(seed: int = 0) -> dict:
    key = jax.random.key(seed)
    ks = jax.random.split(key, 16)
    f32 = jnp.float32

    def w(k, shape, fan_in):
        return jax.random.normal(k, shape, f32) * (fan_in ** -0.5)

    x = jax.random.normal(ks[0], (BATCH, SEQ, D_MODEL), f32)
    positions = jnp.broadcast_to(jnp.arange(SEQ, dtype=jnp.int32), (BATCH, SEQ))
    norm_mix_w = 1.0 + 0.02 * jax.random.normal(ks[1], (DEPTH, D_MODEL), f32)
    norm_ffn_w = 1.0 + 0.02 * jax.random.normal(ks[2], (DEPTH, D_MODEL), f32)
    norm_final_w = 1.0 + 0.02 * jax.random.normal(ks[3], (D_MODEL,), f32)

    mlstm_w_in = w(ks[4], (N_MLSTM_LAYERS, D_MODEL, MLSTM_IN_WIDTH), D_MODEL)
    fbias = jnp.linspace(FGATE_BIAS_LO, FGATE_BIAS_HI, MLSTM_HEADS, dtype=f32)
    base = jnp.stack([jnp.zeros_like(fbias), fbias, jnp.zeros_like(fbias), fbias], axis=0)
    mlstm_b_gate = (base[None] + 0.1 * jax.random.normal(ks[5], (N_MLSTM_LAYERS, 4, MLSTM_HEADS), f32)
                    ).reshape(N_MLSTM_LAYERS, MLSTM_N_GATES)
    mlstm_norm_w = 1.0 + 0.02 * jax.random.normal(ks[6], (N_MLSTM_LAYERS, D_MODEL), f32)
    mlstm_w_out = w(ks[7], (N_MLSTM_LAYERS, D_MODEL, D_MODEL), D_MODEL)

    attn_w_in = w(ks[8], (N_ATTN_LAYERS, D_MODEL, ATTN_IN_WIDTH), D_MODEL)
    attn_sink = 0.5 * jax.random.normal(ks[9], (N_ATTN_LAYERS, ATTN_Q_HEADS), f32)
    attn_w_out = w(ks[10], (N_ATTN_LAYERS, ATTN_Q_HEADS * ATTN_HEAD_DIM, D_MODEL), ATTN_Q_HEADS * ATTN_HEAD_DIM)

    ffn_w_in = w(ks[11], (DEPTH, D_MODEL, 2 * D_FF), D_MODEL)
    ffn_w_out = w(ks[12], (DEPTH, D_FF, D_MODEL), D_FF)
    return {"x": x, "positions": positions, "norm_mix_w": norm_mix_w, "norm_ffn_w": norm_ffn_w,
            "norm_final_w": norm_final_w, "mlstm_w_in": mlstm_w_in, "mlstm_b_gate": mlstm_b_gate,
            "mlstm_norm_w": mlstm_norm_w, "mlstm_w_out": mlstm_w_out, "attn_w_in": attn_w_in,
            "attn_sink": attn_sink, "attn_w_out": attn_w_out, "ffn_w_in": ffn_w_in,
            "ffn_w_out": ffn_w_out}


def reference(x, positions, norm_mix_w, norm_ffn_w, norm_final_w, mlstm_w_in, mlstm_b_gate,
              mlstm_norm_w, mlstm_w_out, attn_w_in, attn_sink, attn_w_out, ffn_w_in, ffn_w_out):
    for i in range(DEPTH):
        j = i // N_MIXERS
        hn = rmsnorm(x, norm_mix_w[i])
        if i % N_MIXERS == 0:
            x = x + mlstm_mixer(hn, mlstm_w_in[j], mlstm_b_gate[j], mlstm_norm_w[j], mlstm_w_out[j])
        else:
            x = x + window_attn_mixer(hn, positions, attn_w_in[j], attn_sink[j], attn_w_out[j])
        x = x + swiglu(rmsnorm(x, norm_ffn_w[i]), ffn_w_in[i], ffn_w_out[i])
    return rmsnorm(x, norm_final_w)
```

```python
import functools

import jax
import jax.numpy as jnp
from jax import lax
from jax.experimental import pallas as pl
from jax.experimental.pallas import tpu as pltpu

F32 = jnp.float32
BF16 = jnp.bfloat16

EPS = 1e-6
LANES = 128

MLSTM_HEADS = 4
MLSTM_CHUNK = 128
ATTN_HEAD_DIM = 64
ATTN_Q_HEADS = 16
ATTN_KV_HEADS = 4
ATTN_GROUP = ATTN_Q_HEADS // ATTN_KV_HEADS
ATTN_BLOCK = 128
ROPE_THETA = 10000.0

NEG_INF = float("-inf")
VMEM_LIMIT = 56 * 1024 * 1024


def _params():
    return pltpu.CompilerParams(dimension_semantics=("arbitrary", "arbitrary"),
                                vmem_limit_bytes=VMEM_LIMIT)


def _resident(shape):
    return pl.BlockSpec(shape, lambda *_: (0,) * len(shape), pipeline_mode=pl.Buffered(1))


def _rmsnorm(x, w):
    ms = jnp.mean(x * x, axis=-1, keepdims=True)
    return x * lax.rsqrt(ms + EPS) * w


def _sigmoid(x):
    return 1.0 / (1.0 + jnp.exp(-x))


def _dot(a, b):
    return jnp.dot(a, b, preferred_element_type=F32)


def _dot_nt(a, b):
    return lax.dot_general(a, b, (((1,), (1,)), ((), ())), preferred_element_type=F32)


def _ffn_kernel(x_ref, nw_ref, win_ref, wout_ref, *rest, d_ff, tf, final):
    if final:
        fw_ref, o_ref, hn_ref, a_ref = rest
    else:
        o_ref, hn_ref, a_ref = rest
    hn_ref[...] = _rmsnorm(x_ref[0], nw_ref[...]).astype(BF16)
    for j in range(d_ff // tf):
        hn = hn_ref[...]
        g = _dot(hn, win_ref[:, j * tf:(j + 1) * tf])
        u = _dot(hn, win_ref[:, d_ff + j * tf:d_ff + (j + 1) * tf])
        a_ref[:, j * tf:(j + 1) * tf] = (g * _sigmoid(g) * u).astype(BF16)
    y = x_ref[0] + _dot(a_ref[...], wout_ref[...])
    if final:
        y = _rmsnorm(y, fw_ref[...])
    o_ref[0] = y


def _ffn(x, norm_w, w_in, w_out, final_w=None, *, tm=512, tf=256):
    B, S, D = x.shape
    d_ff = w_out.shape[0]
    final = final_w is not None
    in_specs = [pl.BlockSpec((1, tm, D), lambda b, i: (b, i, 0)),
                _resident((1, D)), _resident((D, 2 * d_ff)), _resident((d_ff, D))]
    args = [x, norm_w.reshape(1, D), w_in, w_out]
    if final:
        in_specs.append(_resident((1, D)))
        args.append(final_w.reshape(1, D))
    return pl.pallas_call(
        functools.partial(_ffn_kernel, d_ff=d_ff, tf=tf, final=final),
        grid=(B, S // tm),
        in_specs=in_specs,
        out_specs=pl.BlockSpec((1, tm, D), lambda b, i: (b, i, 0)),
        out_shape=jax.ShapeDtypeStruct((B, S, D), F32),
        scratch_shapes=[pltpu.VMEM((tm, D), BF16), pltpu.VMEM((tm, d_ff), BF16)],
        compiler_params=_params(),
        name="ffn",
    )(*args)


def _resid_proj_kernel(a_ref, x_ref, w_ref, o_ref):
    o_ref[0] = x_ref[0] + _dot(a_ref[0], w_ref[...])


def _resid_proj(a, x, w, *, tm=512):
    B, S, D = x.shape
    K = a.shape[-1]
    return pl.pallas_call(
        _resid_proj_kernel,
        grid=(B, S // tm),
        in_specs=[pl.BlockSpec((1, tm, K), lambda b, i: (b, i, 0)),
                  pl.BlockSpec((1, tm, D), lambda b, i: (b, i, 0)),
                  _resident((K, D))],
        out_specs=pl.BlockSpec((1, tm, D), lambda b, i: (b, i, 0)),
        out_shape=jax.ShapeDtypeStruct((B, S, D), F32),
        compiler_params=_params(),
        name="attn_out",
    )(a, x, w)


def _mlstm_proj_kernel(x_ref, nw_ref, wq_ref, wkt_ref, wvo_ref, wgt_ref, bias_ref,
                       q_ref, kt_ref, v_ref, og_ref, rf_ref, *, dk, tm):
    hn = _rmsnorm(x_ref[0], nw_ref[...]).astype(BF16)
    q_ref[0] = (_dot(hn, wq_ref[...]) * (dk ** -0.5)).astype(BF16)
    kt = _dot_nt(wkt_ref[...], hn).astype(BF16)
    gt = _dot_nt(wgt_ref[...], hn)
    ng = gt.shape[0]
    for j in range(tm // LANES):
        kt_ref[0, j] = kt[:, j * LANES:(j + 1) * LANES]
        rf_ref[0, j * ng:(j + 1) * ng, :] = gt[:, j * LANES:(j + 1) * LANES] + bias_ref[...]
    d = v_ref.shape[-1]
    v_ref[0] = _dot(hn, wvo_ref[:, :d]).astype(BF16)
    og_ref[0] = _sigmoid(_dot(hn, wvo_ref[:, d:])).astype(BF16)


def _mlstm_proj(x, norm_w, wq, wkt, wvo, wgt, bias_rows, *, tm=512):
    B, S, D = x.shape
    H = MLSTM_HEADS
    hdk = wq.shape[1]
    dk = hdk // H
    ng = wgt.shape[0]
    nj = tm // LANES
    NC = S // LANES
    row = lambda b, i: (b, i, 0)
    return pl.pallas_call(
        functools.partial(_mlstm_proj_kernel, dk=dk, tm=tm),
        grid=(B, S // tm),
        in_specs=[pl.BlockSpec((1, tm, D), row), _resident((1, D)), _resident((D, hdk)),
                  _resident((hdk, D)), _resident((D, 2 * D)), _resident((ng, D)),
                  _resident((ng, LANES))],
        out_specs=[pl.BlockSpec((1, tm, hdk), row),
                   pl.BlockSpec((1, nj, hdk, LANES), lambda b, i: (b, i, 0, 0)),
                   pl.BlockSpec((1, tm, D), row),
                   pl.BlockSpec((1, tm, D), row),
                   pl.BlockSpec((1, nj * ng, LANES), row)],
        out_shape=[jax.ShapeDtypeStruct((B, S, hdk), BF16),
                   jax.ShapeDtypeStruct((B, NC, hdk, LANES), BF16),
                   jax.ShapeDtypeStruct((B, S, D), BF16),
                   jax.ShapeDtypeStruct((B, S, D), BF16),
                   jax.ShapeDtypeStruct((B, NC * ng, LANES), F32)],
        compiler_params=_params(),
        name="mlstm_proj",
    )(x, norm_w.reshape(1, D), wq, wkt, wvo, wgt, bias_rows)


def _log_sigmoid(x):
    return jnp.minimum(x, 0.0) - jnp.log1p(jnp.exp(-jnp.abs(x)))


def _mlstm_core_kernel(q_ref, kt_ref, v_ref, rf_ref, o_ref, st_ref, c_ref, *, nc, dv):
    L = LANES
    h = pl.program_id(1)
    ng = 4 * MLSTM_HEADS

    def gate_rows(g):
        return rf_ref[0, pl.ds(4 * h + g, nc, stride=ng), :]

    lane = lax.broadcasted_iota(jnp.int32, (nc, L), 1)

    def prefix_sum(x):
        for s in (1, 2, 4, 8, 16, 32, 64):
            x = x + jnp.where(lane >= s, pltpu.roll(x, s, axis=1), 0.0)
        return x

    def suffix_sum(x):
        for s in (1, 2, 4, 8, 16, 32, 64):
            x = x + jnp.where(lane < L - s, pltpu.roll(x, L - s, axis=1), 0.0)
        return x

    for d in range(2):
        log_i = gate_rows(2 * d)
        log_f = _log_sigmoid(gate_rows(2 * d + 1))
        if d == 0:
            b = prefix_sum(log_f)
            b_last = jnp.broadcast_to(b[:, L - 1:L], (nc, L))
        else:
            b = suffix_sum(log_f)
            b_last = jnp.broadcast_to(b[:, 0:1], (nc, L))
        r = log_i - b
        a = b_last + r
        a_max = jnp.broadcast_to(jnp.max(a, axis=1, keepdims=True), (nc, L))
        st_ref[d, 0] = log_f
        st_ref[d, 1] = r
        st_ref[d, 2] = jnp.exp(a - a_max)
        st_ref[d, 3] = b_last
        st_ref[d, 4] = a_max

    c_ref[...] = jnp.zeros_like(c_ref)

    t_idx = lax.broadcasted_iota(jnp.int32, (L, L), 0)
    s_idx = lax.broadcasted_iota(jnp.int32, (L, L), 1)
    masks = (s_idx <= t_idx, s_idx >= t_idx)
    ones = jnp.ones((L, L), BF16)

    def tile3(x):
        return jnp.concatenate([x] * (dv // L + 1), axis=1)

    def chunk(d, c, m_prev, assign):
        rows = pl.ds(pl.multiple_of(c * L, L), L)
        qc = q_ref[0, rows, :]
        kt = kt_ref[0, c]
        vaug = jnp.concatenate([v_ref[0, rows, :], ones], axis=1)
        log_f = st_ref[d, 0, pl.ds(c, 1), :]
        r = st_ref[d, 1, pl.ds(c, 1), :]
        w = st_ref[d, 2, pl.ds(c, 1), :]
        b_last = st_ref[d, 3, pl.ds(c, 1), :]
        a_max = st_ref[d, 4, pl.ds(c, 1), :]
        mask = masks[d]

        cm = jnp.max(jnp.where(mask, r, NEG_INF), axis=1, keepdims=True)
        b_col = jnp.sum(jnp.where(mask, log_f, 0.0), axis=1, keepdims=True)
        mu = jnp.broadcast_to(jnp.maximum(m_prev[:, 0:1], cm), (L, L))
        decay = jnp.exp(jnp.where(mask, r - mu, NEG_INF))
        inter = jnp.exp(m_prev - mu)

        p = (_dot(qc, kt) * decay).astype(BF16)
        caug = c_ref[d]
        pv = _dot(p, vaug)
        qc_state = _dot(qc, caug.astype(BF16))
        den = pv[:, dv:] + inter * qc_state[:, dv:]
        den = jnp.maximum(jnp.abs(den), jnp.exp(-(b_col + mu)))
        num = pv[:, :dv] + tile3(inter)[:, :dv] * qc_state[:, :dv]
        out = num * tile3(1.0 / den)[:, :dv]
        if assign:
            o_ref[0, rows, :] = out
        else:
            o_ref[0, rows, :] += out

        m_new = jnp.maximum(b_last + m_prev, a_max)
        s_prev = jnp.exp(b_last + m_prev - m_new)
        s_cur = jnp.exp(a_max - m_new)
        kw = (kt.astype(F32) * w).astype(BF16)
        c_ref[d] = tile3(s_prev) * caug + tile3(s_cur) * _dot(kw, vaug)
        return m_new

    def body(i, carry, assign):
        m_f, m_b = carry
        return chunk(0, i, m_f, assign), chunk(1, nc - 1 - i, m_b, assign)

    m0 = jnp.zeros((1, L), F32)
    carry = lax.fori_loop(0, nc // 2, functools.partial(body, assign=True), (m0, m0))
    lax.fori_loop(nc // 2, nc, functools.partial(body, assign=False), carry)


def _mlstm_core(q, kt, v, rf):
    B, S, hdk = q.shape
    H = MLSTM_HEADS
    dk = hdk // H
    dv = v.shape[-1] // H
    nc = S // LANES
    return pl.pallas_call(
        functools.partial(_mlstm_core_kernel, nc=nc, dv=dv),
        grid=(B, H),
        in_specs=[pl.BlockSpec((1, S, dk), lambda b, h: (b, 0, h)),
                  pl.BlockSpec((1, nc, dk, LANES), lambda b, h: (b, 0, h, 0)),
                  pl.BlockSpec((1, S, dv), lambda b, h: (b, 0, h)),
                  pl.BlockSpec((1, rf.shape[1], LANES), lambda b, h: (b, 0, 0))],
        out_specs=pl.BlockSpec((1, S, dv), lambda b, h: (b, 0, h)),
        out_shape=jax.ShapeDtypeStruct((B, S, H * dv), F32),
        scratch_shapes=[pltpu.VMEM((2, 5, nc, LANES), F32),
                        pltpu.VMEM((2, dk, dv + LANES), F32)],
        compiler_params=_params(),
        name="mlstm_core",
    )(q, kt, v, rf)


def _mlstm_out_kernel(hs_ref, og_ref, x_ref, nw_ref, w_ref, o_ref, *, dv):
    parts = []
    for h in range(MLSTM_HEADS):
        hs = hs_ref[0, :, h * dv:(h + 1) * dv]
        ms = jnp.mean(hs * hs, axis=-1, keepdims=True)
        parts.append(hs * lax.rsqrt(ms + EPS))
    y = jnp.concatenate(parts, axis=-1) * nw_ref[...] * og_ref[0].astype(F32)
    o_ref[0] = x_ref[0] + _dot(y.astype(BF16), w_ref[...])


def _mlstm_out(hs, og, x, norm_w, w, *, tm=512):
    B, S, D = x.shape
    row = lambda b, i: (b, i, 0)
    return pl.pallas_call(
        functools.partial(_mlstm_out_kernel, dv=D // MLSTM_HEADS),
        grid=(B, S // tm),
        in_specs=[pl.BlockSpec((1, tm, D), row), pl.BlockSpec((1, tm, D), row),
                  pl.BlockSpec((1, tm, D), row), _resident((1, D)), _resident((D, D))],
        out_specs=pl.BlockSpec((1, tm, D), row),
        out_shape=jax.ShapeDtypeStruct((B, S, D), F32),
        compiler_params=_params(),
        name="mlstm_out",
    )(hs, og, x, norm_w.reshape(1, D), w)


def _attn_proj_kernel(x_ref, nw_ref, wt_ref, pos_ref, invf_ref, qt_ref, k_ref, vt_ref, *, tm):
    hd = ATTN_HEAD_DIM
    half = hd // 2
    nq = ATTN_Q_HEADS * hd
    nk = ATTN_KV_HEADS * hd
    hn = _rmsnorm(x_ref[0], nw_ref[...]).astype(BF16)
    pt = _dot_nt(wt_ref[...], hn)
    ang = invf_ref[...] * pos_ref[0].astype(F32)
    cos = jnp.cos(ang)
    sin = jnp.sin(ang)

    def rope(xh):
        x1, x2 = xh[:half], xh[half:]
        return jnp.concatenate([x1 * cos - x2 * sin, x2 * cos + x1 * sin], axis=0)

    scale = hd ** -0.5
    qt = jnp.concatenate([rope(pt[h * hd:(h + 1) * hd]) * scale for h in range(ATTN_Q_HEADS)],
                         axis=0).astype(BF16)
    kt = jnp.concatenate([rope(pt[nq + g * hd:nq + (g + 1) * hd]) for g in range(ATTN_KV_HEADS)],
                         axis=0)
    vt = pt[nq + nk:].astype(BF16)
    for j in range(tm // LANES):
        cols = slice(j * LANES, (j + 1) * LANES)
        qt_ref[0, j] = qt[:, cols]
        vt_ref[0, j] = vt[:, cols]
        for f in range(nk // LANES):
            k_ref[0, cols, f * LANES:(f + 1) * LANES] = kt[f * LANES:(f + 1) * LANES, cols].T.astype(BF16)


def _attn_proj(x, norm_w, wt, pos_rows, inv_freq, *, tm=512):
    B, S, D = x.shape
    nq = ATTN_Q_HEADS * ATTN_HEAD_DIM
    nk = ATTN_KV_HEADS * ATTN_HEAD_DIM
    nj = tm // LANES
    nb = S // LANES
    return pl.pallas_call(
        functools.partial(_attn_proj_kernel, tm=tm),
        grid=(B, S // tm),
        in_specs=[pl.BlockSpec((1, tm, D), lambda b, i: (b, i, 0)), _resident((1, D)),
                  _resident((nq + 2 * nk, D)),
                  pl.BlockSpec((1, 1, tm), lambda b, i: (b, 0, i)),
                  _resident((ATTN_HEAD_DIM // 2, 1))],
        out_specs=[pl.BlockSpec((1, nj, nq, LANES), lambda b, i: (b, i, 0, 0)),
                   pl.BlockSpec((1, tm, nk), lambda b, i: (b, i, 0)),
                   pl.BlockSpec((1, nj, nk, LANES), lambda b, i: (b, i, 0, 0))],
        out_shape=[jax.ShapeDtypeStruct((B, nb, nq, LANES), BF16),
                   jax.ShapeDtypeStruct((B, S, nk), BF16),
                   jax.ShapeDtypeStruct((B, nb, nk, LANES), BF16)],
        compiler_params=_params(),
        name="attn_proj",
    )(x, norm_w.reshape(1, D), wt, pos_rows, inv_freq)


def _attn_core_kernel(qt_ref, kp_ref, kc_ref, kn_ref, vp_ref, vc_ref, vn_ref, sink_ref, o_ref, *, nb):
    L = LANES
    hd = ATTN_HEAD_DIM
    G = ATTN_GROUP
    c = pl.program_id(1)
    k_refs = (kp_ref, kc_ref, kn_ref)
    v_refs = (vp_ref, vc_ref, vn_ref)
    key = lax.broadcasted_iota(jnp.int32, (L, G * L), 0)
    qry = lax.broadcasted_iota(jnp.int32, (L, G * L), 1) % L
    prev_bias = jnp.where(c > 0, 0.0, NEG_INF)
    next_bias = jnp.where(c < nb - 1, 0.0, NEG_INF)
    zeros = jnp.zeros((hd, G * L), BF16)

    for g in range(ATTN_KV_HEADS):
        qg = jnp.concatenate([qt_ref[0, 0, (G * g + i) * hd:(G * g + i + 1) * hd, :] for i in range(G)],
                             axis=1)
        qz = jnp.concatenate([qg, zeros] if g % 2 == 0 else [zeros, qg], axis=0)
        slab = slice((g // 2) * L, (g // 2 + 1) * L)
        s = [_dot(kr[0, :, slab], qz) for kr in k_refs]
        s[0] = jnp.where(key >= qry, s[0], NEG_INF) + prev_bias
        s[2] = jnp.where(key <= qry, s[2], NEG_INF) + next_bias
        sink = sink_ref[g:g + 1, :]
        m = sink
        for sj in s:
            m = jnp.maximum(m, jnp.max(sj, axis=0, keepdims=True))
        denom = jnp.exp(sink - m)
        acc = jnp.zeros((hd, G * L), F32)
        for sj, vr in zip(s, v_refs):
            pj = jnp.exp(sj - m)
            denom = denom + jnp.sum(pj, axis=0, keepdims=True)
            acc = acc + _dot(vr[0, 0, g * hd:(g + 1) * hd, :], pj.astype(BF16))
        acc = acc * (1.0 / denom)
        for pair in range(G // 2):
            tile = jnp.concatenate([acc[:, (2 * pair) * L:(2 * pair + 1) * L],
                                    acc[:, (2 * pair + 1) * L:(2 * pair + 2) * L]], axis=0)
            col = (G * g + 2 * pair) * hd
            o_ref[0, :, col:col + L] = tile.T.astype(BF16)


def _attn_core(qt, k, vt, sink_rows):
    B, nb, nq, L = qt.shape
    nk = k.shape[-1]
    S = nb * L
    prev = lambda b, c: (b, jnp.maximum(c - 1, 0), 0)
    cur = lambda b, c: (b, c, 0)
    nxt = lambda b, c: (b, jnp.minimum(c + 1, nb - 1), 0)
    k_spec = lambda f: pl.BlockSpec((1, L, nk), f)
    v_spec = lambda f: pl.BlockSpec((1, 1, nk, L), lambda b, c: f(b, c) + (0,))
    return pl.pallas_call(
        functools.partial(_attn_core_kernel, nb=nb),
        grid=(B, nb),
        in_specs=[pl.BlockSpec((1, 1, nq, L), lambda b, c: (b, c, 0, 0)),
                  k_spec(prev), k_spec(cur), k_spec(nxt),
                  v_spec(prev), v_spec(cur), v_spec(nxt),
                  _resident(sink_rows.shape)],
        out_specs=pl.BlockSpec((1, L, nq), cur),
        out_shape=jax.ShapeDtypeStruct((B, S, nq), BF16),
        compiler_params=_params(),
        name="attn_core",
    )(qt, k, k, k, vt, vt, vt, sink_rows)


def _mlstm_layer(x, norm_w, w_in, b_gate, head_norm_w, w_out):
    D = x.shape[-1]
    H = MLSTM_HEADS
    hdk = D // 2
    w = w_in.astype(BF16)
    wq = w[:, :hdk]
    wkt = w[:, hdk:2 * hdk].T
    wvo = w[:, 2 * hdk:2 * hdk + 2 * D]
    wgt = w[:, 2 * hdk + 2 * D:].T.reshape(4, H, D).transpose(1, 0, 2).reshape(4 * H, D)
    bias = b_gate.astype(F32).reshape(4, H).T.reshape(4 * H, 1)
    bias_rows = jnp.broadcast_to(bias, (4 * H, LANES))
    q, kt, v, og, rf = _mlstm_proj(x, norm_w, wq, wkt, wvo, wgt, bias_rows)
    hs = _mlstm_core(q, kt, v, rf)
    return _mlstm_out(hs, og, x, head_norm_w, w_out.astype(BF16))


def _attn_layer(x, pos_rows, inv_freq, norm_w, w_in, sink, w_out):
    G, L = ATTN_GROUP, LANES
    qt, k, vt = _attn_proj(x, norm_w, w_in.astype(BF16).T, pos_rows, inv_freq)
    sink_rows = jnp.repeat(sink.astype(F32).reshape(ATTN_KV_HEADS, G), L, axis=1)
    a = _attn_core(qt, k, vt, sink_rows)
    return _resid_proj(a, x, w_out.astype(BF16))


def kernel(x, positions, norm_mix_w, norm_ffn_w, norm_final_w, mlstm_w_in, mlstm_b_gate, mlstm_norm_w, mlstm_w_out, attn_w_in, attn_sink, attn_w_out, ffn_w_in, ffn_w_out):
    depth = norm_mix_w.shape[0]
    B, S = positions.shape
    half = ATTN_HEAD_DIM // 2
    inv_freq = (ROPE_THETA ** (-jnp.arange(half, dtype=F32) / half)).reshape(half, 1)
    pos_rows = positions.reshape(B, 1, S)
    for i in range(depth):
        j = i // 2
        if i % 2 == 0:
            x = _mlstm_layer(x, norm_mix_w[i], mlstm_w_in[j], mlstm_b_gate[j], mlstm_norm_w[j],
                             mlstm_w_out[j])
        else:
            x = _attn_layer(x, pos_rows, inv_freq, norm_mix_w[i], attn_w_in[j], attn_sink[j],
                            attn_w_out[j])
        final_w = norm_final_w if i == depth - 1 else None
        x = _ffn(x, norm_ffn_w[i], ffn_w_in[i].astype(BF16), ffn_w_out[i].astype(BF16), final_w)
    return x
```

```python
import functools

import jax
import jax.numpy as jnp
from jax import lax
from jax.experimental import pallas as pl
from jax.experimental.pallas import tpu as pltpu

F32 = jnp.float32
BF16 = jnp.bfloat16

EPS = 1e-6
LANES = 128

MLSTM_HEADS = 4
MLSTM_CHUNK = 128
ATTN_HEAD_DIM = 64
ATTN_Q_HEADS = 16
ATTN_KV_HEADS = 4
ATTN_GROUP = ATTN_Q_HEADS // ATTN_KV_HEADS
ATTN_BLOCK = 128
ROPE_THETA = 10000.0

NEG_INF = float("-inf")
LOG2E = 1.4426950408889634
VMEM_LIMIT = 56 * 1024 * 1024


def _params():
    return pltpu.CompilerParams(dimension_semantics=("arbitrary", "arbitrary"),
                                vmem_limit_bytes=VMEM_LIMIT)


def _resident(shape):
    return pl.BlockSpec(shape, lambda *_: (0,) * len(shape), pipeline_mode=pl.Buffered(1))


def _rmsnorm(x, w):
    ms = jnp.mean(x * x, axis=-1, keepdims=True)
    return x * lax.rsqrt(ms + EPS) * w


def _sigmoid(x):
    return 1.0 / (1.0 + jnp.exp(-x))


def _dot(a, b):
    return jnp.dot(a, b, preferred_element_type=F32)


def _dot_nt(a, b):
    return lax.dot_general(a, b, (((1,), (1,)), ((), ())), preferred_element_type=F32)


def _ffn_kernel(x_ref, nw_ref, win_ref, wout_ref, *rest, d_ff, tf, final):
    if final:
        fw_ref, o_ref, hn_ref, a_ref = rest
    else:
        o_ref, hn_ref, a_ref = rest
    hn_ref[...] = _rmsnorm(x_ref[0], nw_ref[...]).astype(BF16)
    for j in range(d_ff // tf):
        hn = hn_ref[...]
        g = _dot(hn, win_ref[:, j * tf:(j + 1) * tf])
        u = _dot(hn, win_ref[:, d_ff + j * tf:d_ff + (j + 1) * tf])
        a_ref[:, j * tf:(j + 1) * tf] = (g * _sigmoid(g) * u).astype(BF16)
    y = x_ref[0] + _dot(a_ref[...], wout_ref[...])
    if final:
        y = _rmsnorm(y, fw_ref[...])
    o_ref[0] = y


def _ffn(x, norm_w, w_in, w_out, final_w=None, *, tm=512, tf=256):
    B, S, D = x.shape
    d_ff = w_out.shape[0]
    final = final_w is not None
    in_specs = [pl.BlockSpec((1, tm, D), lambda b, i: (b, i, 0)),
                _resident((1, D)), _resident((D, 2 * d_ff)), _resident((d_ff, D))]
    args = [x, norm_w.reshape(1, D), w_in, w_out]
    if final:
        in_specs.append(_resident((1, D)))
        args.append(final_w.reshape(1, D))
    return pl.pallas_call(
        functools.partial(_ffn_kernel, d_ff=d_ff, tf=tf, final=final),
        grid=(B, S // tm),
        in_specs=in_specs,
        out_specs=pl.BlockSpec((1, tm, D), lambda b, i: (b, i, 0)),
        out_shape=jax.ShapeDtypeStruct((B, S, D), F32),
        scratch_shapes=[pltpu.VMEM((tm, D), BF16), pltpu.VMEM((tm, d_ff), BF16)],
        compiler_params=_params(),
        name="ffn",
    )(*args)


def _resid_proj_kernel(a_ref, x_ref, w_ref, o_ref):
    o_ref[0] = x_ref[0] + _dot(a_ref[0], w_ref[...])


def _resid_proj(a, x, w, *, tm=512):
    B, S, D = x.shape
    K = a.shape[-1]
    return pl.pallas_call(
        _resid_proj_kernel,
        grid=(B, S // tm),
        in_specs=[pl.BlockSpec((1, tm, K), lambda b, i: (b, i, 0)),
                  pl.BlockSpec((1, tm, D), lambda b, i: (b, i, 0)),
                  _resident((K, D))],
        out_specs=pl.BlockSpec((1, tm, D), lambda b, i: (b, i, 0)),
        out_shape=jax.ShapeDtypeStruct((B, S, D), F32),
        compiler_params=_params(),
        name="attn_out",
    )(a, x, w)


def _mlstm_proj_kernel(x_ref, nw_ref, wq_ref, wkt_ref, wvo_ref, wgt_ref, bias_ref,
                       q_ref, kt_ref, v_ref, og_ref, rf_ref, *, dk, tm):
    hn = _rmsnorm(x_ref[0], nw_ref[...]).astype(BF16)
    q_ref[0] = (_dot(hn, wq_ref[...]) * (dk ** -0.5)).astype(BF16)
    kt = _dot_nt(wkt_ref[...], hn).astype(BF16)
    gt = _dot_nt(wgt_ref[...], hn)
    ng = gt.shape[0]
    for j in range(tm // LANES):
        kt_ref[0, j] = kt[:, j * LANES:(j + 1) * LANES]
        rf_ref[0, j * ng:(j + 1) * ng, :] = gt[:, j * LANES:(j + 1) * LANES] + bias_ref[...]
    d = v_ref.shape[-1]
    v_ref[0] = _dot(hn, wvo_ref[:, :d]).astype(BF16)
    og_ref[0] = _sigmoid(_dot(hn, wvo_ref[:, d:])).astype(BF16)


def _mlstm_proj(x, norm_w, wq, wkt, wvo, wgt, bias_rows, *, tm=512):
    B, S, D = x.shape
    H = MLSTM_HEADS
    hdk = wq.shape[1]
    dk = hdk // H
    ng = wgt.shape[0]
    nj = tm // LANES
    NC = S // LANES
    row = lambda b, i: (b, i, 0)
    return pl.pallas_call(
        functools.partial(_mlstm_proj_kernel, dk=dk, tm=tm),
        grid=(B, S // tm),
        in_specs=[pl.BlockSpec((1, tm, D), row), _resident((1, D)), _resident((D, hdk)),
                  _resident((hdk, D)), _resident((D, 2 * D)), _resident((ng, D)),
                  _resident((ng, LANES))],
        out_specs=[pl.BlockSpec((1, tm, hdk), row),
                   pl.BlockSpec((1, nj, hdk, LANES), lambda b, i: (b, i, 0, 0)),
                   pl.BlockSpec((1, tm, D), row),
                   pl.BlockSpec((1, tm, D), row),
                   pl.BlockSpec((1, nj * ng, LANES), row)],
        out_shape=[jax.ShapeDtypeStruct((B, S, hdk), BF16),
                   jax.ShapeDtypeStruct((B, NC, hdk, LANES), BF16),
                   jax.ShapeDtypeStruct((B, S, D), BF16),
                   jax.ShapeDtypeStruct((B, S, D), BF16),
                   jax.ShapeDtypeStruct((B, NC * ng, LANES), F32)],
        compiler_params=_params(),
        name="mlstm_proj",
    )(x, norm_w.reshape(1, D), wq, wkt, wvo, wgt, bias_rows)


def _log_sigmoid(x):
    return jnp.minimum(x, 0.0) - jnp.log1p(jnp.exp(-jnp.abs(x)))


def _mlstm_core_kernel(q_ref, kt_ref, v_ref, rf_ref, o_ref, st_ref, c_ref, *, nc, dv, unroll):
    L = LANES
    h = pl.program_id(1)
    ng = 4 * MLSTM_HEADS

    def gate_rows(g):
        return rf_ref[0, pl.ds(4 * h + g, nc, stride=ng), :]

    lane = lax.broadcasted_iota(jnp.int32, (nc, L), 1)

    def prefix_sum(x):
        for s in (1, 2, 4, 8, 16, 32, 64):
            x = x + jnp.where(lane >= s, pltpu.roll(x, s, axis=1), 0.0)
        return x

    def suffix_sum(x):
        for s in (1, 2, 4, 8, 16, 32, 64):
            x = x + jnp.where(lane < L - s, pltpu.roll(x, L - s, axis=1), 0.0)
        return x

    for d in range(2):
        log_i = gate_rows(2 * d)
        log_f = _log_sigmoid(gate_rows(2 * d + 1))
        if d == 0:
            b = prefix_sum(log_f)
            b_last = jnp.broadcast_to(b[:, L - 1:L], (nc, L))
        else:
            b = suffix_sum(log_f)
            b_last = jnp.broadcast_to(b[:, 0:1], (nc, L))
        r = log_i - b
        a = b_last + r
        a_max = jnp.broadcast_to(jnp.max(a, axis=1, keepdims=True), (nc, L))
        st_ref[d, 0] = log_f
        st_ref[d, 1] = r
        st_ref[d, 2] = jnp.exp(a - a_max)
        st_ref[d, 3] = b_last
        st_ref[d, 4] = a_max

    def m_scan(i, carry):
        new = []
        for d, c in ((0, i), (1, nc - 1 - i)):
            row = pl.ds(c, 1)
            st_ref[d, 5, row, :] = carry[d]
            new.append(jnp.maximum(st_ref[d, 3, row, :] + carry[d], st_ref[d, 4, row, :]))
        return tuple(new)

    m0 = jnp.zeros((1, L), F32)
    lax.fori_loop(0, nc, m_scan, (m0, m0))
    for d in range(2):
        b_last, a_max, m_prev = st_ref[d, 3], st_ref[d, 4], st_ref[d, 5]
        m_new = jnp.maximum(b_last + m_prev, a_max)
        st_ref[d, 3] = jnp.exp(b_last + m_prev - m_new)
        st_ref[d, 2] = st_ref[d, 2] * jnp.exp(a_max - m_new)
        st_ref[d, 0] = st_ref[d, 0] * LOG2E
        st_ref[d, 1] = st_ref[d, 1] * LOG2E
        st_ref[d, 5] = m_prev * LOG2E

    c_ref[...] = jnp.zeros_like(c_ref)

    t_idx = lax.broadcasted_iota(jnp.int32, (L, L), 0)
    s_idx = lax.broadcasted_iota(jnp.int32, (L, L), 1)
    masks = (s_idx <= t_idx, s_idx >= t_idx)
    ones = jnp.ones((L, L), BF16)

    def tile3(x):
        return jnp.concatenate([x] * (dv // L + 1), axis=1)

    def body(i, carry, assign):
        items = []
        for u in range(unroll):
            c = i * unroll + u
            items += [(0, c), (1, nc - 1 - c)]

        work = []
        for d, c in items:
            rows = pl.ds(pl.multiple_of(c * L, L), L)
            qc = q_ref[0, rows, :]
            kt = kt_ref[0, c]
            vaug = jnp.concatenate([v_ref[0, rows, :], ones], axis=1)
            w = st_ref[d, 2, pl.ds(c, 1), :]
            scores = _dot(qc, kt)
            kv = _dot(kt * w.astype(BF16), vaug)
            work.append((rows, qc, vaug, scores, kv))

        states = [c_ref[0], c_ref[1]]
        prev_states = []
        for (d, c), (rows, qc, vaug, scores, kv) in zip(items, work):
            prev_states.append(states[d].astype(BF16))
            states[d] = tile3(st_ref[d, 3, pl.ds(c, 1), :]) * states[d] + kv
        c_ref[0] = states[0]
        c_ref[1] = states[1]

        for (d, c), (rows, qc, vaug, scores, kv), prev_state in zip(items, work, prev_states):
            log_f = st_ref[d, 0, pl.ds(c, 1), :]
            r = st_ref[d, 1, pl.ds(c, 1), :]
            m_prev = st_ref[d, 5, pl.ds(c, 1), :]
            mask = masks[d]
            r_masked = jnp.where(mask, r, NEG_INF)
            cm = jnp.max(r_masked, axis=1, keepdims=True)
            b_col = jnp.sum(jnp.where(mask, log_f, 0.0), axis=1, keepdims=True)
            mu = jnp.broadcast_to(jnp.maximum(m_prev[:, 0:1], cm), (L, L))
            decay = jnp.exp2(r_masked - mu)
            inter = jnp.exp2(m_prev - mu)
            lhs = jnp.concatenate([(scores * decay).astype(BF16), inter.astype(BF16) * qc], axis=1)
            both = _dot(lhs, jnp.concatenate([vaug, prev_state], axis=0))
            den = jnp.maximum(jnp.abs(both[:, dv:]), jnp.exp2(-(b_col + mu)))
            out = both[:, :dv] * tile3(1.0 / den)[:, :dv]
            if assign:
                o_ref[0, rows, :] = out
            else:
                o_ref[0, rows, :] += out
        return carry

    steps = nc // 2 // unroll
    lax.fori_loop(0, steps, functools.partial(body, assign=True), 0)
    lax.fori_loop(steps, 2 * steps, functools.partial(body, assign=False), 0)


def _mlstm_core(q, kt, v, rf, *, unroll=4):
    B, S, hdk = q.shape
    H = MLSTM_HEADS
    dk = hdk // H
    dv = v.shape[-1] // H
    nc = S // LANES
    return pl.pallas_call(
        functools.partial(_mlstm_core_kernel, nc=nc, dv=dv, unroll=min(unroll, nc // 2)),
        grid=(B, H),
        in_specs=[pl.BlockSpec((1, S, dk), lambda b, h: (b, 0, h)),
                  pl.BlockSpec((1, nc, dk, LANES), lambda b, h: (b, 0, h, 0)),
                  pl.BlockSpec((1, S, dv), lambda b, h: (b, 0, h)),
                  pl.BlockSpec((1, rf.shape[1], LANES), lambda b, h: (b, 0, 0))],
        out_specs=pl.BlockSpec((1, S, dv), lambda b, h: (b, 0, h)),
        out_shape=jax.ShapeDtypeStruct((B, S, H * dv), F32),
        scratch_shapes=[pltpu.VMEM((2, 6, nc, LANES), F32),
                        pltpu.VMEM((2, dk, dv + LANES), F32)],
        compiler_params=_params(),
        name="mlstm_core",
    )(q, kt, v, rf)


def _mlstm_out_kernel(hs_ref, og_ref, x_ref, nw_ref, w_ref, o_ref, *, dv):
    parts = []
    for h in range(MLSTM_HEADS):
        hs = hs_ref[0, :, h * dv:(h + 1) * dv]
        ms = jnp.mean(hs * hs, axis=-1, keepdims=True)
        parts.append(hs * lax.rsqrt(ms + EPS))
    y = jnp.concatenate(parts, axis=-1) * nw_ref[...] * og_ref[0].astype(F32)
    o_ref[0] = x_ref[0] + _dot(y.astype(BF16), w_ref[...])


def _mlstm_out(hs, og, x, norm_w, w, *, tm=512):
    B, S, D = x.shape
    row = lambda b, i: (b, i, 0)
    return pl.pallas_call(
        functools.partial(_mlstm_out_kernel, dv=D // MLSTM_HEADS),
        grid=(B, S // tm),
        in_specs=[pl.BlockSpec((1, tm, D), row), pl.BlockSpec((1, tm, D), row),
                  pl.BlockSpec((1, tm, D), row), _resident((1, D)), _resident((D, D))],
        out_specs=pl.BlockSpec((1, tm, D), row),
        out_shape=jax.ShapeDtypeStruct((B, S, D), F32),
        compiler_params=_params(),
        name="mlstm_out",
    )(hs, og, x, norm_w.reshape(1, D), w)


def _attn_proj_kernel(x_ref, nw_ref, wt_ref, pos_ref, invf_ref, qt_ref, k_ref, vt_ref, *, tm):
    hd = ATTN_HEAD_DIM
    half = hd // 2
    nq = ATTN_Q_HEADS * hd
    nk = ATTN_KV_HEADS * hd
    hn = _rmsnorm(x_ref[0], nw_ref[...]).astype(BF16)
    pt = _dot_nt(wt_ref[...], hn)
    ang = invf_ref[...] * pos_ref[0].astype(F32)
    cos = jnp.cos(ang)
    sin = jnp.sin(ang)

    def rope(xh):
        x1, x2 = xh[:half], xh[half:]
        return jnp.concatenate([x1 * cos - x2 * sin, x2 * cos + x1 * sin], axis=0)

    scale = LOG2E * hd ** -0.5
    qt = jnp.concatenate([rope(pt[h * hd:(h + 1) * hd]) * scale for h in range(ATTN_Q_HEADS)],
                         axis=0).astype(BF16)
    kt = jnp.concatenate([rope(pt[nq + g * hd:nq + (g + 1) * hd]) for g in range(ATTN_KV_HEADS)],
                         axis=0)
    vt = pt[nq + nk:].astype(BF16)
    for j in range(tm // LANES):
        cols = slice(j * LANES, (j + 1) * LANES)
        qt_ref[0, j] = qt[:, cols]
        vt_ref[0, j] = vt[:, cols]
        for f in range(nk // LANES):
            k_ref[0, cols, f * LANES:(f + 1) * LANES] = kt[f * LANES:(f + 1) * LANES, cols].T.astype(BF16)


def _attn_proj(x, norm_w, wt, pos_rows, inv_freq, *, tm=512):
    B, S, D = x.shape
    nq = ATTN_Q_HEADS * ATTN_HEAD_DIM
    nk = ATTN_KV_HEADS * ATTN_HEAD_DIM
    nj = tm // LANES
    nb = S // LANES
    return pl.pallas_call(
        functools.partial(_attn_proj_kernel, tm=tm),
        grid=(B, S // tm),
        in_specs=[pl.BlockSpec((1, tm, D), lambda b, i: (b, i, 0)), _resident((1, D)),
                  _resident((nq + 2 * nk, D)),
                  pl.BlockSpec((1, 1, tm), lambda b, i: (b, 0, i)),
                  _resident((ATTN_HEAD_DIM // 2, 1))],
        out_specs=[pl.BlockSpec((1, nj, nq, LANES), lambda b, i: (b, i, 0, 0)),
                   pl.BlockSpec((1, tm, nk), lambda b, i: (b, i, 0)),
                   pl.BlockSpec((1, nj, nk, LANES), lambda b, i: (b, i, 0, 0))],
        out_shape=[jax.ShapeDtypeStruct((B, nb, nq, LANES), BF16),
                   jax.ShapeDtypeStruct((B, S, nk), BF16),
                   jax.ShapeDtypeStruct((B, nb, nk, LANES), BF16)],
        compiler_params=_params(),
        name="attn_proj",
    )(x, norm_w.reshape(1, D), wt, pos_rows, inv_freq)


def _attn_core_kernel(qt_ref, kp_ref, kc_ref, kn_ref, vp_ref, vc_ref, vn_ref, sink_ref, o_ref,
                      k_all, v_all, *, nb, nblk):
    L = LANES
    hd = ATTN_HEAD_DIM
    G = ATTN_GROUP
    c = pl.program_id(1)
    k_all[0:L] = kp_ref[0]
    k_all[L:(nblk + 1) * L] = kc_ref[0]
    k_all[(nblk + 1) * L:] = kn_ref[0]
    v_all[0] = vp_ref[0, 0]
    v_all[1:nblk + 1] = vc_ref[0]
    v_all[nblk + 1] = vn_ref[0, 0]

    key = lax.broadcasted_iota(jnp.int32, (L, G * L), 0)
    qry = lax.broadcasted_iota(jnp.int32, (L, G * L), 1) % L
    prev_mask = jnp.where(key >= qry, 0.0, NEG_INF)
    next_mask = jnp.where(key <= qry, 0.0, NEG_INF)
    zeros = jnp.zeros((hd, G * L), BF16)
    ones = jnp.ones((16, 3 * L), BF16)

    def scores(j, g):
        qg = jnp.concatenate([qt_ref[0, j, (G * g + i) * hd:(G * g + i + 1) * hd, :] for i in range(G)],
                             axis=1)
        qz = jnp.concatenate([qg, zeros] if g % 2 == 0 else [zeros, qg], axis=0)
        return _dot(k_all[j * L:(j + 3) * L, (g // 2) * L:(g // 2 + 1) * L], qz)

    items = [(j, g) for j in range(nblk) for g in range(ATTN_KV_HEADS)]
    s_next = scores(*items[0])
    for n, (j, g) in enumerate(items):
        s = s_next
        if n + 1 < len(items):
            s_next = scores(*items[n + 1])
        if g == 0:
            blk = c * nblk + j
            prev_bias = prev_mask + jnp.where(blk > 0, 0.0, NEG_INF)
            next_bias = next_mask + jnp.where(blk < nb - 1, 0.0, NEG_INF)
        s = [s[0:L] + prev_bias, s[L:2 * L], s[2 * L:] + next_bias]
        sink = sink_ref[g:g + 1, :]
        m = sink
        for sj in s:
            m = jnp.maximum(m, jnp.max(sj, axis=0, keepdims=True))
        p = jnp.concatenate([jnp.exp2(sj - m).astype(BF16) for sj in s], axis=0)
        vg = jnp.concatenate([v_all[j + i, g * hd:(g + 1) * hd, :] for i in range(3)], axis=1)
        acc = _dot(jnp.concatenate([vg, ones], axis=0), p)
        denom = acc[hd:hd + 1] + jnp.exp2(sink - m)
        out = acc[:hd] * (1.0 / denom)
        for pair in range(G // 2):
            tile = jnp.concatenate([out[:, (2 * pair) * L:(2 * pair + 1) * L],
                                    out[:, (2 * pair + 1) * L:(2 * pair + 2) * L]], axis=0)
            col = (G * g + 2 * pair) * hd
            o_ref[0, j * L:(j + 1) * L, col:col + L] = tile.T.astype(BF16)


def _attn_core(qt, k, vt, sink_rows, *, nblk=4):
    B, nb, nq, L = qt.shape
    nk = k.shape[-1]
    S = nb * L
    nblk = min(nblk, nb)
    prev = lambda b, c: (b, jnp.maximum(c * nblk - 1, 0), 0)
    cur = lambda b, c: (b, c, 0)
    nxt = lambda b, c: (b, jnp.minimum((c + 1) * nblk, nb - 1), 0)
    four = lambda f: (lambda b, c: f(b, c) + (0,))
    return pl.pallas_call(
        functools.partial(_attn_core_kernel, nb=nb, nblk=nblk),
        grid=(B, nb // nblk),
        in_specs=[pl.BlockSpec((1, nblk, nq, L), four(cur)),
                  pl.BlockSpec((1, L, nk), prev), pl.BlockSpec((1, nblk * L, nk), cur),
                  pl.BlockSpec((1, L, nk), nxt),
                  pl.BlockSpec((1, 1, nk, L), four(prev)), pl.BlockSpec((1, nblk, nk, L), four(cur)),
                  pl.BlockSpec((1, 1, nk, L), four(nxt)),
                  _resident(sink_rows.shape)],
        out_specs=pl.BlockSpec((1, nblk * L, nq), cur),
        out_shape=jax.ShapeDtypeStruct((B, S, nq), BF16),
        scratch_shapes=[pltpu.VMEM(((nblk + 2) * L, nk), BF16),
                        pltpu.VMEM((nblk + 2, nk, L), BF16)],
        compiler_params=_params(),
        name="attn_core",
    )(qt, k, k, k, vt, vt, vt, sink_rows)


def _mlstm_layer(x, norm_w, w_in, b_gate, head_norm_w, w_out):
    D = x.shape[-1]
    H = MLSTM_HEADS
    hdk = D // 2
    w = w_in.astype(BF16)
    wq = w[:, :hdk]
    wkt = w[:, hdk:2 * hdk].T
    wvo = w[:, 2 * hdk:2 * hdk + 2 * D]
    wgt = w[:, 2 * hdk + 2 * D:].T.reshape(4, H, D).transpose(1, 0, 2).reshape(4 * H, D)
    bias = b_gate.astype(F32).reshape(4, H).T.reshape(4 * H, 1)
    bias_rows = jnp.broadcast_to(bias, (4 * H, LANES))
    q, kt, v, og, rf = _mlstm_proj(x, norm_w, wq, wkt, wvo, wgt, bias_rows)
    hs = _mlstm_core(q, kt, v, rf)
    return _mlstm_out(hs, og, x, head_norm_w, w_out.astype(BF16))


def _attn_layer(x, pos_rows, inv_freq, norm_w, w_in, sink, w_out):
    G, L = ATTN_GROUP, LANES
    qt, k, vt = _attn_proj(x, norm_w, w_in.astype(BF16).T, pos_rows, inv_freq)
    sink_rows = jnp.repeat(LOG2E * sink.astype(F32).reshape(ATTN_KV_HEADS, G), L, axis=1)
    a = _attn_core(qt, k, vt, sink_rows)
    return _resid_proj(a, x, w_out.astype(BF16))


def kernel(x, positions, norm_mix_w, norm_ffn_w, norm_final_w, mlstm_w_in, mlstm_b_gate, mlstm_norm_w, mlstm_w_out, attn_w_in, attn_sink, attn_w_out, ffn_w_in, ffn_w_out):
    depth = norm_mix_w.shape[0]
    B, S = positions.shape
    half = ATTN_HEAD_DIM // 2
    inv_freq = (ROPE_THETA ** (-jnp.arange(half, dtype=F32) / half)).reshape(half, 1)
    pos_rows = positions.reshape(B, 1, S)
    for i in range(depth):
        j = i // 2
        if i % 2 == 0:
            x = _mlstm_layer(x, norm_mix_w[i], mlstm_w_in[j], mlstm_b_gate[j], mlstm_norm_w[j],
                             mlstm_w_out[j])
        else:
            x = _attn_layer(x, pos_rows, inv_freq, norm_mix_w[i], attn_w_in[j], attn_sink[j],
                            attn_w_out[j])
        final_w = norm_final_w if i == depth - 1 else None
        x = _ffn(x, norm_ffn_w[i], ffn_w_in[i].astype(BF16), ffn_w_out[i].astype(BF16), final_w)
    return x
```

```python
import functools

import jax
import jax.numpy as jnp
from jax import lax
from jax.experimental import pallas as pl
from jax.experimental.pallas import tpu as pltpu

F32 = jnp.float32
BF16 = jnp.bfloat16

EPS = 1e-6
LANES = 128

MLSTM_HEADS = 4
MLSTM_CHUNK = 128
ATTN_HEAD_DIM = 64
ATTN_Q_HEADS = 16
ATTN_KV_HEADS = 4
ATTN_GROUP = ATTN_Q_HEADS // ATTN_KV_HEADS
ATTN_BLOCK = 128
ROPE_THETA = 10000.0

NEG_INF = float("-inf")
LOG2E = 1.4426950408889634
VMEM_LIMIT = 56 * 1024 * 1024


def _params():
    return pltpu.CompilerParams(dimension_semantics=("arbitrary", "arbitrary"),
                                vmem_limit_bytes=VMEM_LIMIT)


def _resident(shape):
    return pl.BlockSpec(shape, lambda *_: (0,) * len(shape), pipeline_mode=pl.Buffered(1))


def _rmsnorm(x, w):
    ms = jnp.mean(x * x, axis=-1, keepdims=True)
    return x * lax.rsqrt(ms + EPS) * w


def _sigmoid(x):
    return 1.0 / (1.0 + jnp.exp(-x))


def _dot(a, b):
    return jnp.dot(a, b, preferred_element_type=F32)


def _dot_nt(a, b):
    return lax.dot_general(a, b, (((1,), (1,)), ((), ())), preferred_element_type=F32)


def _layer_tail_kernel(*refs, d_ff, tf, mixer, final):
    it = iter(refs)
    if mixer == "mlstm":
        hs_ref, og_ref, hw_ref = next(it), next(it), next(it)
    else:
        a_ref = next(it)
    x_ref, wo_ref, nw_ref, win_ref, wout_ref = (next(it) for _ in range(5))
    fw_ref = next(it) if final else None
    o_ref, x1_ref, hn_ref, act_ref = (next(it) for _ in range(4))

    if mixer == "mlstm":
        dv = hs_ref.shape[-1] // MLSTM_HEADS
        parts = []
        for h in range(MLSTM_HEADS):
            hs = hs_ref[0, :, h * dv:(h + 1) * dv]
            ms = jnp.mean(hs * hs, axis=-1, keepdims=True)
            parts.append(hs * lax.rsqrt(ms + EPS))
        y = (jnp.concatenate(parts, axis=-1) * hw_ref[...] * og_ref[0].astype(F32)).astype(BF16)
    else:
        y = a_ref[0]
    x1_ref[...] = x_ref[0] + _dot(y, wo_ref[...])

    hn_ref[...] = _rmsnorm(x1_ref[...], nw_ref[...]).astype(BF16)
    for j in range(d_ff // tf):
        hn = hn_ref[...]
        g = _dot(hn, win_ref[:, j * tf:(j + 1) * tf])
        u = _dot(hn, win_ref[:, d_ff + j * tf:d_ff + (j + 1) * tf])
        act_ref[:, j * tf:(j + 1) * tf] = (g * _sigmoid(g) * u).astype(BF16)
    out = x1_ref[...] + _dot(act_ref[...], wout_ref[...])
    if final:
        out = _rmsnorm(out, fw_ref[...])
    o_ref[0] = out


def _layer_tail(mixer, mixer_args, x, w_o, layer_o, norm_w, w_in, w_out, layer, final_w=None,
                *, tm=512, tf=256):
    B, S, D = x.shape
    d_ff = w_out.shape[1]
    final = final_w is not None
    row = pl.BlockSpec((1, tm, D), lambda b, i: (b, i, 0))

    def stacked(shape, l):
        return pl.BlockSpec((None,) + shape, lambda b, i: (l,) + (0,) * len(shape),
                            pipeline_mode=pl.Buffered(1))

    if mixer == "mlstm":
        hs, og, head_w = mixer_args
        args = [hs, og, head_w.reshape(1, D)]
        in_specs = [row, row, _resident((1, D))]
    else:
        args = list(mixer_args)
        in_specs = [row]
    args += [x, w_o, norm_w.reshape(1, D), w_in, w_out]
    in_specs += [row, stacked((w_o.shape[1], D), layer_o), _resident((1, D)),
                 stacked((D, 2 * d_ff), layer), stacked((d_ff, D), layer)]
    if final:
        args.append(final_w.reshape(1, D))
        in_specs.append(_resident((1, D)))
    return pl.pallas_call(
        functools.partial(_layer_tail_kernel, d_ff=d_ff, tf=tf, mixer=mixer, final=final),
        grid=(B, S // tm),
        in_specs=in_specs,
        out_specs=row,
        out_shape=jax.ShapeDtypeStruct((B, S, D), F32),
        scratch_shapes=[pltpu.VMEM((tm, D), F32), pltpu.VMEM((tm, D), BF16),
                        pltpu.VMEM((tm, d_ff), BF16)],
        compiler_params=_params(),
        name=mixer + "_tail",
    )(*args)


def _mlstm_proj_kernel(x_ref, nw_ref, wq_ref, wkt_ref, wvo_ref, wgt_ref, bias_ref,
                       q_ref, kt_ref, v_ref, og_ref, rf_ref, *, dk, tm):
    hn = _rmsnorm(x_ref[0], nw_ref[...]).astype(BF16)
    d = v_ref.shape[-1]
    og_ref[0] = _sigmoid(_dot(hn, wvo_ref[:, d:])).astype(BF16)
    v_ref[0] = _dot(hn, wvo_ref[:, :d]).astype(BF16)
    q_ref[0] = (_dot(hn, wq_ref[...]) * (dk ** -0.5)).astype(BF16)
    kt = _dot_nt(wkt_ref[...], hn).astype(BF16)
    gt = _dot_nt(wgt_ref[...], hn)
    ng = gt.shape[0]
    for j in range(tm // LANES):
        kt_ref[0, j] = kt[:, j * LANES:(j + 1) * LANES]
        rf_ref[0, j * ng:(j + 1) * ng, :] = gt[:, j * LANES:(j + 1) * LANES] + bias_ref[...]


def _mlstm_proj(x, norm_w, wq, wkt, wvo, wgt, bias_rows, *, tm=512):
    B, S, D = x.shape
    H = MLSTM_HEADS
    hdk = wq.shape[1]
    dk = hdk // H
    ng = wgt.shape[0]
    nj = tm // LANES
    NC = S // LANES
    row = lambda b, i: (b, i, 0)
    return pl.pallas_call(
        functools.partial(_mlstm_proj_kernel, dk=dk, tm=tm),
        grid=(B, S // tm),
        in_specs=[pl.BlockSpec((1, tm, D), row), _resident((1, D)), _resident((D, hdk)),
                  _resident((hdk, D)), _resident((D, 2 * D)), _resident((ng, D)),
                  _resident((ng, LANES))],
        out_specs=[pl.BlockSpec((1, tm, hdk), row),
                   pl.BlockSpec((1, nj, hdk, LANES), lambda b, i: (b, i, 0, 0)),
                   pl.BlockSpec((1, tm, D), row),
                   pl.BlockSpec((1, tm, D), row),
                   pl.BlockSpec((1, nj * ng, LANES), row)],
        out_shape=[jax.ShapeDtypeStruct((B, S, hdk), BF16),
                   jax.ShapeDtypeStruct((B, NC, hdk, LANES), BF16),
                   jax.ShapeDtypeStruct((B, S, D), BF16),
                   jax.ShapeDtypeStruct((B, S, D), BF16),
                   jax.ShapeDtypeStruct((B, NC * ng, LANES), F32)],
        compiler_params=_params(),
        name="mlstm_proj",
    )(x, norm_w.reshape(1, D), wq, wkt, wvo, wgt, bias_rows)


def _log_sigmoid(x):
    return jnp.minimum(x, 0.0) - jnp.log1p(jnp.exp(-jnp.abs(x)))


def _mlstm_core_kernel(q_ref, kt_ref, v_ref, rf_ref, o_ref, st_ref, c_ref, *, nc, dv, unroll):
    L = LANES
    h = pl.program_id(1)
    ng = 4 * MLSTM_HEADS

    def gate_rows(g):
        return rf_ref[0, pl.ds(4 * h + g, nc, stride=ng), :]

    lane = lax.broadcasted_iota(jnp.int32, (nc, L), 1)

    def prefix_sum(x):
        for s in (1, 2, 4, 8, 16, 32, 64):
            x = x + jnp.where(lane >= s, pltpu.roll(x, s, axis=1), 0.0)
        return x

    def suffix_sum(x):
        for s in (1, 2, 4, 8, 16, 32, 64):
            x = x + jnp.where(lane < L - s, pltpu.roll(x, L - s, axis=1), 0.0)
        return x

    for d in range(2):
        log_i = gate_rows(2 * d)
        log_f = _log_sigmoid(gate_rows(2 * d + 1))
        if d == 0:
            b = prefix_sum(log_f)
            b_last = jnp.broadcast_to(b[:, L - 1:L], (nc, L))
        else:
            b = suffix_sum(log_f)
            b_last = jnp.broadcast_to(b[:, 0:1], (nc, L))
        r = log_i - b
        a = b_last + r
        a_max = jnp.broadcast_to(jnp.max(a, axis=1, keepdims=True), (nc, L))
        st_ref[d, 0] = log_f
        st_ref[d, 1] = r
        st_ref[d, 2] = jnp.exp(a - a_max)
        st_ref[d, 3] = b_last
        st_ref[d, 4] = a_max

    def m_scan(i, carry):
        new = []
        for d, c in ((0, i), (1, nc - 1 - i)):
            row = pl.ds(c, 1)
            st_ref[d, 5, row, :] = carry[d]
            new.append(jnp.maximum(st_ref[d, 3, row, :] + carry[d], st_ref[d, 4, row, :]))
        return tuple(new)

    m0 = jnp.zeros((1, L), F32)
    lax.fori_loop(0, nc, m_scan, (m0, m0))
    for d in range(2):
        b_last, a_max, m_prev = st_ref[d, 3], st_ref[d, 4], st_ref[d, 5]
        m_new = jnp.maximum(b_last + m_prev, a_max)
        st_ref[d, 3] = jnp.exp(b_last + m_prev - m_new)
        st_ref[d, 2] = st_ref[d, 2] * jnp.exp(a_max - m_new)
        st_ref[d, 0] = st_ref[d, 0] * LOG2E
        st_ref[d, 1] = st_ref[d, 1] * LOG2E
        st_ref[d, 5] = m_prev * LOG2E

    c_ref[...] = jnp.zeros_like(c_ref)

    t_idx = lax.broadcasted_iota(jnp.int32, (L, L), 0)
    s_idx = lax.broadcasted_iota(jnp.int32, (L, L), 1)
    masks = (s_idx <= t_idx, s_idx >= t_idx)
    ones = jnp.ones((L, L), BF16)

    def tile3(x):
        return jnp.concatenate([x] * (dv // L + 1), axis=1)

    def body(i, carry, assign):
        items = []
        for u in range(unroll):
            c = i * unroll + u
            items += [(0, c), (1, nc - 1 - c)]

        work = []
        for d, c in items:
            rows = pl.ds(pl.multiple_of(c * L, L), L)
            qc = q_ref[0, rows, :]
            kt = kt_ref[0, c]
            vaug = jnp.concatenate([v_ref[0, rows, :], ones], axis=1)
            w = st_ref[d, 2, pl.ds(c, 1), :]
            scores = _dot(qc, kt)
            kv = _dot(kt * w.astype(BF16), vaug)
            work.append((rows, qc, vaug, scores, kv))

        states = [c_ref[0], c_ref[1]]
        prev_states = []
        for (d, c), (rows, qc, vaug, scores, kv) in zip(items, work):
            prev_states.append(states[d].astype(BF16))
            states[d] = tile3(st_ref[d, 3, pl.ds(c, 1), :]) * states[d] + kv
        c_ref[0] = states[0]
        c_ref[1] = states[1]

        for (d, c), (rows, qc, vaug, scores, kv), prev_state in zip(items, work, prev_states):
            log_f = st_ref[d, 0, pl.ds(c, 1), :]
            r = st_ref[d, 1, pl.ds(c, 1), :]
            m_prev = st_ref[d, 5, pl.ds(c, 1), :]
            mask = masks[d]
            r_masked = jnp.where(mask, r, NEG_INF)
            cm = jnp.max(r_masked, axis=1, keepdims=True)
            b_col = jnp.sum(jnp.where(mask, log_f, 0.0), axis=1, keepdims=True)
            mu = jnp.broadcast_to(jnp.maximum(m_prev[:, 0:1], cm), (L, L))
            decay = jnp.exp2(r_masked - mu)
            inter = jnp.exp2(m_prev - mu)
            lhs = jnp.concatenate([(scores * decay).astype(BF16), inter.astype(BF16) * qc], axis=1)
            both = _dot(lhs, jnp.concatenate([vaug, prev_state], axis=0))
            den = jnp.maximum(jnp.abs(both[:, dv:]), jnp.exp2(-(b_col + mu)))
            out = both[:, :dv] * tile3(1.0 / den)[:, :dv]
            if assign:
                o_ref[0, rows, :] = out
            else:
                o_ref[0, rows, :] += out
        return carry

    steps = nc // 2 // unroll
    lax.fori_loop(0, steps, functools.partial(body, assign=True), 0)
    lax.fori_loop(steps, 2 * steps, functools.partial(body, assign=False), 0)


def _mlstm_core(q, kt, v, rf, *, unroll=4):
    B, S, hdk = q.shape
    H = MLSTM_HEADS
    dk = hdk // H
    dv = v.shape[-1] // H
    nc = S // LANES
    return pl.pallas_call(
        functools.partial(_mlstm_core_kernel, nc=nc, dv=dv, unroll=min(unroll, nc // 2)),
        grid=(B, H),
        in_specs=[pl.BlockSpec((1, S, dk), lambda b, h: (b, 0, h)),
                  pl.BlockSpec((1, nc, dk, LANES), lambda b, h: (b, 0, h, 0)),
                  pl.BlockSpec((1, S, dv), lambda b, h: (b, 0, h)),
                  pl.BlockSpec((1, rf.shape[1], LANES), lambda b, h: (b, 0, 0))],
        out_specs=pl.BlockSpec((1, S, dv), lambda b, h: (b, 0, h)),
        out_shape=jax.ShapeDtypeStruct((B, S, H * dv), F32),
        scratch_shapes=[pltpu.VMEM((2, 6, nc, LANES), F32),
                        pltpu.VMEM((2, dk, dv + LANES), F32)],
        compiler_params=_params(),
        name="mlstm_core",
    )(q, kt, v, rf)


def _attn_proj_kernel(x_ref, nw_ref, wt_ref, pos_ref, invf_ref, qt_ref, k_ref, vt_ref, *, tm):
    hd = ATTN_HEAD_DIM
    half = hd // 2
    nq = ATTN_Q_HEADS * hd
    nk = ATTN_KV_HEADS * hd
    hn = _rmsnorm(x_ref[0], nw_ref[...]).astype(BF16)
    ang = invf_ref[...] * pos_ref[0].astype(F32)
    cos = jnp.cos(ang)
    sin = jnp.sin(ang)
    chunks = [slice(j * LANES, (j + 1) * LANES) for j in range(tm // LANES)]

    def proj(lo, hi):
        return _dot_nt(wt_ref[lo:hi, :], hn)

    def rope(xh):
        x1, x2 = xh[:half], xh[half:]
        return jnp.concatenate([x1 * cos - x2 * sin, x2 * cos + x1 * sin], axis=0)

    vt = proj(nq + nk, nq + 2 * nk).astype(BF16)
    for j, cols in enumerate(chunks):
        vt_ref[0, j] = vt[:, cols]
    kp = proj(nq, nq + nk)
    kt = jnp.concatenate([rope(kp[g * hd:(g + 1) * hd]) for g in range(ATTN_KV_HEADS)], axis=0)
    for j, cols in enumerate(chunks):
        for f in range(nk // LANES):
            k_ref[0, cols, f * LANES:(f + 1) * LANES] = kt[f * LANES:(f + 1) * LANES, cols].T.astype(BF16)
    scale = LOG2E * hd ** -0.5
    piece = 4 * hd
    for lo in range(0, nq, piece):
        qp = proj(lo, lo + piece)
        qt = jnp.concatenate([rope(qp[h * hd:(h + 1) * hd]) * scale for h in range(piece // hd)],
                             axis=0).astype(BF16)
        for j, cols in enumerate(chunks):
            qt_ref[0, j, lo:lo + piece, :] = qt[:, cols]


def _attn_proj(x, norm_w, wt, pos_rows, inv_freq, *, tm=512):
    B, S, D = x.shape
    nq = ATTN_Q_HEADS * ATTN_HEAD_DIM
    nk = ATTN_KV_HEADS * ATTN_HEAD_DIM
    nj = tm // LANES
    nb = S // LANES
    return pl.pallas_call(
        functools.partial(_attn_proj_kernel, tm=tm),
        grid=(B, S // tm),
        in_specs=[pl.BlockSpec((1, tm, D), lambda b, i: (b, i, 0)), _resident((1, D)),
                  _resident((nq + 2 * nk, D)),
                  pl.BlockSpec((1, 1, tm), lambda b, i: (b, 0, i)),
                  _resident((ATTN_HEAD_DIM // 2, 1))],
        out_specs=[pl.BlockSpec((1, nj, nq, LANES), lambda b, i: (b, i, 0, 0)),
                   pl.BlockSpec((1, tm, nk), lambda b, i: (b, i, 0)),
                   pl.BlockSpec((1, nj, nk, LANES), lambda b, i: (b, i, 0, 0))],
        out_shape=[jax.ShapeDtypeStruct((B, nb, nq, LANES), BF16),
                   jax.ShapeDtypeStruct((B, S, nk), BF16),
                   jax.ShapeDtypeStruct((B, nb, nk, LANES), BF16)],
        compiler_params=_params(),
        name="attn_proj",
    )(x, norm_w.reshape(1, D), wt, pos_rows, inv_freq)


def _attn_core_kernel(qt_ref, kp_ref, kc_ref, kn_ref, vp_ref, vc_ref, vn_ref, sink_ref, o_ref,
                      k_all, v_all, *, nb, nblk):
    L = LANES
    hd = ATTN_HEAD_DIM
    G = ATTN_GROUP
    c = pl.program_id(1)
    k_all[0:L] = kp_ref[0]
    k_all[L:(nblk + 1) * L] = kc_ref[0]
    k_all[(nblk + 1) * L:] = kn_ref[0]
    v_all[0] = vp_ref[0, 0]
    v_all[1:nblk + 1] = vc_ref[0]
    v_all[nblk + 1] = vn_ref[0, 0]

    key = lax.broadcasted_iota(jnp.int32, (L, G * L), 0)
    qry = lax.broadcasted_iota(jnp.int32, (L, G * L), 1) % L
    prev_mask = jnp.where(key >= qry, 0.0, NEG_INF)
    next_mask = jnp.where(key <= qry, 0.0, NEG_INF)
    zeros = jnp.zeros((hd, G * L), BF16)
    ones = jnp.ones((16, 3 * L), BF16)

    def scores(j, g):
        qg = jnp.concatenate([qt_ref[0, j, (G * g + i) * hd:(G * g + i + 1) * hd, :] for i in range(G)],
                             axis=1)
        qz = jnp.concatenate([qg, zeros] if g % 2 == 0 else [zeros, qg], axis=0)
        return _dot(k_all[j * L:(j + 3) * L, (g // 2) * L:(g // 2 + 1) * L], qz)

    items = [(j, g) for j in range(nblk) for g in range(ATTN_KV_HEADS)]
    s_next = scores(*items[0])
    for n, (j, g) in enumerate(items):
        s = s_next
        if n + 1 < len(items):
            s_next = scores(*items[n + 1])
        if g == 0:
            blk = c * nblk + j
            prev_bias = prev_mask + jnp.where(blk > 0, 0.0, NEG_INF)
            next_bias = next_mask + jnp.where(blk < nb - 1, 0.0, NEG_INF)
        s = [s[0:L] + prev_bias, s[L:2 * L], s[2 * L:] + next_bias]
        sink = sink_ref[g:g + 1, :]
        m = sink
        for sj in s:
            m = jnp.maximum(m, jnp.max(sj, axis=0, keepdims=True))
        p = jnp.concatenate([jnp.exp2(sj - m).astype(BF16) for sj in s], axis=0)
        vg = jnp.concatenate([v_all[j + i, g * hd:(g + 1) * hd, :] for i in range(3)], axis=1)
        acc = _dot(jnp.concatenate([vg, ones], axis=0), p)
        denom = acc[hd:hd + 1] + jnp.exp2(sink - m)
        out = acc[:hd] * (1.0 / denom)
        for pair in range(G // 2):
            tile = jnp.concatenate([out[:, (2 * pair) * L:(2 * pair + 1) * L],
                                    out[:, (2 * pair + 1) * L:(2 * pair + 2) * L]], axis=0)
            col = (G * g + 2 * pair) * hd
            o_ref[0, j * L:(j + 1) * L, col:col + L] = tile.T.astype(BF16)


def _attn_core(qt, k, vt, sink_rows, *, nblk=4):
    B, nb, nq, L = qt.shape
    nk = k.shape[-1]
    S = nb * L
    nblk = min(nblk, nb)
    prev = lambda b, c: (b, jnp.maximum(c * nblk - 1, 0), 0)
    cur = lambda b, c: (b, c, 0)
    nxt = lambda b, c: (b, jnp.minimum((c + 1) * nblk, nb - 1), 0)
    four = lambda f: (lambda b, c: f(b, c) + (0,))
    return pl.pallas_call(
        functools.partial(_attn_core_kernel, nb=nb, nblk=nblk),
        grid=(B, nb // nblk),
        in_specs=[pl.BlockSpec((1, nblk, nq, L), four(cur)),
                  pl.BlockSpec((1, L, nk), prev), pl.BlockSpec((1, nblk * L, nk), cur),
                  pl.BlockSpec((1, L, nk), nxt),
                  pl.BlockSpec((1, 1, nk, L), four(prev)), pl.BlockSpec((1, nblk, nk, L), four(cur)),
                  pl.BlockSpec((1, 1, nk, L), four(nxt)),
                  _resident(sink_rows.shape)],
        out_specs=pl.BlockSpec((1, nblk * L, nq), cur),
        out_shape=jax.ShapeDtypeStruct((B, S, nq), BF16),
        scratch_shapes=[pltpu.VMEM(((nblk + 2) * L, nk), BF16),
                        pltpu.VMEM((nblk + 2, nk, L), BF16)],
        compiler_params=_params(),
        name="attn_core",
    )(qt, k, k, k, vt, vt, vt, sink_rows)


def _mlstm_mixer(x, norm_w, w_in, b_gate):
    D = x.shape[-1]
    H = MLSTM_HEADS
    hdk = D // 2
    w = w_in.astype(BF16)
    wq = w[:, :hdk]
    wkt = w[:, hdk:2 * hdk].T
    wvo = w[:, 2 * hdk:2 * hdk + 2 * D]
    wgt = w[:, 2 * hdk + 2 * D:].T.reshape(4, H, D).transpose(1, 0, 2).reshape(4 * H, D)
    bias = b_gate.astype(F32).reshape(4, H).T.reshape(4 * H, 1)
    bias_rows = jnp.broadcast_to(bias, (4 * H, LANES))
    q, kt, v, og, rf = _mlstm_proj(x, norm_w, wq, wkt, wvo, wgt, bias_rows)
    return _mlstm_core(q, kt, v, rf), og


def _attn_mixer(x, pos_rows, inv_freq, norm_w, w_in, sink):
    G, L = ATTN_GROUP, LANES
    qt, k, vt = _attn_proj(x, norm_w, w_in.astype(BF16).T, pos_rows, inv_freq)
    sink_rows = jnp.repeat(LOG2E * sink.astype(F32).reshape(ATTN_KV_HEADS, G), L, axis=1)
    return _attn_core(qt, k, vt, sink_rows)


def kernel(x, positions, norm_mix_w, norm_ffn_w, norm_final_w, mlstm_w_in, mlstm_b_gate, mlstm_norm_w, mlstm_w_out, attn_w_in, attn_sink, attn_w_out, ffn_w_in, ffn_w_out):
    depth = norm_mix_w.shape[0]
    B, S = positions.shape
    half = ATTN_HEAD_DIM // 2
    inv_freq = (ROPE_THETA ** (-jnp.arange(half, dtype=F32) / half)).reshape(half, 1)
    pos_rows = positions.reshape(B, 1, S)
    mlstm_wo, attn_wo = mlstm_w_out.astype(BF16), attn_w_out.astype(BF16)
    ffn_wi, ffn_wo = ffn_w_in.astype(BF16), ffn_w_out.astype(BF16)
    for i in range(depth):
        j = i // 2
        final_w = norm_final_w if i == depth - 1 else None
        if i % 2 == 0:
            hs, og = _mlstm_mixer(x, norm_mix_w[i], mlstm_w_in[j], mlstm_b_gate[j])
            x = _layer_tail("mlstm", (hs, og, mlstm_norm_w[j]), x, mlstm_wo, j, norm_ffn_w[i],
                            ffn_wi, ffn_wo, i, final_w)
        else:
            a = _attn_mixer(x, pos_rows, inv_freq, norm_mix_w[i], attn_w_in[j], attn_sink[j])
            x = _layer_tail("attn", (a,), x, attn_wo, j, norm_ffn_w[i], ffn_wi, ffn_wo, i, final_w)
    return x
```

```python
import functools

import jax
import jax.numpy as jnp
from jax import lax
from jax.experimental import pallas as pl
from jax.experimental.pallas import tpu as pltpu

F32 = jnp.float32
BF16 = jnp.bfloat16

EPS = 1e-6
LANES = 128

MLSTM_HEADS = 4
MLSTM_CHUNK = 128
ATTN_HEAD_DIM = 64
ATTN_Q_HEADS = 16
ATTN_KV_HEADS = 4
ATTN_GROUP = ATTN_Q_HEADS // ATTN_KV_HEADS
ATTN_BLOCK = 128
ROPE_THETA = 10000.0

NEG_INF = float("-inf")
LOG2E = 1.4426950408889634
VMEM_LIMIT = 56 * 1024 * 1024


def _params():
    return pltpu.CompilerParams(dimension_semantics=("arbitrary", "arbitrary"),
                                vmem_limit_bytes=VMEM_LIMIT)


def _resident(shape):
    return pl.BlockSpec(shape, lambda *_: (0,) * len(shape), pipeline_mode=pl.Buffered(1))


def _rmsnorm(x, w):
    ms = jnp.mean(x * x, axis=-1, keepdims=True)
    return x * lax.rsqrt(ms + EPS) * w


def _sigmoid(x):
    return 1.0 / (1.0 + jnp.exp(-x))


def _dot(a, b):
    return jnp.dot(a, b, preferred_element_type=F32)


def _dot_nt(a, b):
    return lax.dot_general(a, b, (((1,), (1,)), ((), ())), preferred_element_type=F32)


def _layer_tail_kernel(*refs, d_ff, tf, nsub, mixer, final):
    it = iter(refs)
    if mixer == "mlstm":
        hs_ref, og_ref, hw_ref = next(it), next(it), next(it)
    else:
        a_ref = next(it)
    x_ref, wo_ref, nw_ref, win_ref, wout_ref = (next(it) for _ in range(5))
    fw_ref = next(it) if final else None
    o_ref, x1_ref, hn_ref, act_ref = (next(it) for _ in range(4))

    tm = x_ref.shape[1]
    subs = [slice(r * (tm // nsub), (r + 1) * (tm // nsub)) for r in range(nsub)]
    for rs in subs:
        if mixer == "mlstm":
            dv = hs_ref.shape[-1] // MLSTM_HEADS
            parts = []
            for h in range(MLSTM_HEADS):
                hs = hs_ref[0, rs, h * dv:(h + 1) * dv]
                ms = jnp.mean(hs * hs, axis=-1, keepdims=True)
                parts.append(hs * lax.rsqrt(ms + EPS))
            y = (jnp.concatenate(parts, axis=-1) * hw_ref[...] * og_ref[0, rs, :].astype(F32)).astype(BF16)
        else:
            y = a_ref[0, rs, :]
        x1_ref[rs, :] = x_ref[0, rs, :] + _dot(y, wo_ref[...])
    for rs in subs:
        hn_ref[rs, :] = _rmsnorm(x1_ref[rs, :], nw_ref[...]).astype(BF16)
    for rs in subs:
        for j in range(d_ff // tf):
            hn = hn_ref[rs, :]
            g = _dot(hn, win_ref[:, j * tf:(j + 1) * tf])
            u = _dot(hn, win_ref[:, d_ff + j * tf:d_ff + (j + 1) * tf])
            act_ref[rs, j * tf:(j + 1) * tf] = (g * _sigmoid(g) * u).astype(BF16)
    for rs in subs:
        out = x1_ref[rs, :] + _dot(act_ref[rs, :], wout_ref[...])
        if final:
            out = _rmsnorm(out, fw_ref[...])
        o_ref[0, rs, :] = out


def _layer_tail(mixer, mixer_args, x, w_o, layer_o, norm_w, w_in, w_out, layer, final_w=None,
                *, tm=512, tf=256, nsub=2):
    B, S, D = x.shape
    d_ff = w_out.shape[1]
    final = final_w is not None
    row = pl.BlockSpec((1, tm, D), lambda b, i: (b, i, 0))

    def stacked(shape, l):
        return pl.BlockSpec((None,) + shape, lambda b, i: (l,) + (0,) * len(shape),
                            pipeline_mode=pl.Buffered(1))

    if mixer == "mlstm":
        hs, og, head_w = mixer_args
        args = [hs, og, head_w.reshape(1, D)]
        in_specs = [row, row, _resident((1, D))]
    else:
        args = list(mixer_args)
        in_specs = [row]
    args += [x, w_o, norm_w.reshape(1, D), w_in, w_out]
    in_specs += [row, stacked((w_o.shape[1], D), layer_o), _resident((1, D)),
                 stacked((D, 2 * d_ff), layer), stacked((d_ff, D), layer)]
    if final:
        args.append(final_w.reshape(1, D))
        in_specs.append(_resident((1, D)))
    return pl.pallas_call(
        functools.partial(_layer_tail_kernel, d_ff=d_ff, tf=tf, nsub=nsub, mixer=mixer, final=final),
        grid=(B, S // tm),
        in_specs=in_specs,
        out_specs=row,
        out_shape=jax.ShapeDtypeStruct((B, S, D), F32),
        scratch_shapes=[pltpu.VMEM((tm, D), F32), pltpu.VMEM((tm, D), BF16),
                        pltpu.VMEM((tm, d_ff), BF16)],
        compiler_params=_params(),
        name=mixer + "_tail",
    )(*args)


def _mlstm_proj_kernel(x_ref, nw_ref, wq_ref, wkt_ref, wvo_ref, wgt_ref, bias_ref,
                       q_ref, kt_ref, v_ref, og_ref, rf_ref, *, dk, tm, nsub):
    d = v_ref.shape[-1]
    ng = wgt_ref.shape[0]
    sub = tm // nsub
    for r in range(nsub):
        rs = slice(r * sub, (r + 1) * sub)
        hn = _rmsnorm(x_ref[0, rs, :], nw_ref[...]).astype(BF16)
        og_ref[0, rs, :] = _sigmoid(_dot(hn, wvo_ref[:, d:])).astype(BF16)
        v_ref[0, rs, :] = _dot(hn, wvo_ref[:, :d]).astype(BF16)
        q_ref[0, rs, :] = (_dot(hn, wq_ref[...]) * (dk ** -0.5)).astype(BF16)
        kt = _dot_nt(wkt_ref[...], hn).astype(BF16)
        gt = _dot_nt(wgt_ref[...], hn)
        for jj in range(sub // LANES):
            j = r * (sub // LANES) + jj
            kt_ref[0, j] = kt[:, jj * LANES:(jj + 1) * LANES]
            rf_ref[0, j * ng:(j + 1) * ng, :] = gt[:, jj * LANES:(jj + 1) * LANES] + bias_ref[...]


def _mlstm_proj(x, norm_w, wq, wkt, wvo, wgt, bias_rows, *, tm=1024, nsub=4):
    B, S, D = x.shape
    H = MLSTM_HEADS
    hdk = wq.shape[1]
    dk = hdk // H
    ng = wgt.shape[0]
    nj = tm // LANES
    NC = S // LANES
    row = lambda b, i: (b, i, 0)
    return pl.pallas_call(
        functools.partial(_mlstm_proj_kernel, dk=dk, tm=tm, nsub=nsub),
        grid=(B, S // tm),
        in_specs=[pl.BlockSpec((1, tm, D), row), _resident((1, D)), _resident((D, hdk)),
                  _resident((hdk, D)), _resident((D, 2 * D)), _resident((ng, D)),
                  _resident((ng, LANES))],
        out_specs=[pl.BlockSpec((1, tm, hdk), row),
                   pl.BlockSpec((1, nj, hdk, LANES), lambda b, i: (b, i, 0, 0)),
                   pl.BlockSpec((1, tm, D), row),
                   pl.BlockSpec((1, tm, D), row),
                   pl.BlockSpec((1, nj * ng, LANES), row)],
        out_shape=[jax.ShapeDtypeStruct((B, S, hdk), BF16),
                   jax.ShapeDtypeStruct((B, NC, hdk, LANES), BF16),
                   jax.ShapeDtypeStruct((B, S, D), BF16),
                   jax.ShapeDtypeStruct((B, S, D), BF16),
                   jax.ShapeDtypeStruct((B, NC * ng, LANES), F32)],
        compiler_params=_params(),
        name="mlstm_proj",
    )(x, norm_w.reshape(1, D), wq, wkt, wvo, wgt, bias_rows)


def _log_sigmoid(x):
    return jnp.minimum(x, 0.0) - jnp.log1p(jnp.exp(-jnp.abs(x)))


def _mlstm_core_kernel(q_ref, kt_ref, v_ref, rf_ref, o_ref, st_ref, c_ref, *, nc, dv, unroll):
    L = LANES
    h = pl.program_id(1)
    ng = 4 * MLSTM_HEADS

    def gate_rows(g):
        return rf_ref[0, pl.ds(4 * h + g, nc, stride=ng), :]

    lane = lax.broadcasted_iota(jnp.int32, (nc, L), 1)

    def prefix_sum(x):
        for s in (1, 2, 4, 8, 16, 32, 64):
            x = x + jnp.where(lane >= s, pltpu.roll(x, s, axis=1), 0.0)
        return x

    def suffix_sum(x):
        for s in (1, 2, 4, 8, 16, 32, 64):
            x = x + jnp.where(lane < L - s, pltpu.roll(x, L - s, axis=1), 0.0)
        return x

    for d in range(2):
        log_i = gate_rows(2 * d)
        log_f = _log_sigmoid(gate_rows(2 * d + 1))
        if d == 0:
            b = prefix_sum(log_f)
            b_last = jnp.broadcast_to(b[:, L - 1:L], (nc, L))
        else:
            b = suffix_sum(log_f)
            b_last = jnp.broadcast_to(b[:, 0:1], (nc, L))
        r = log_i - b
        a = b_last + r
        a_max = jnp.broadcast_to(jnp.max(a, axis=1, keepdims=True), (nc, L))
        st_ref[d, 0] = log_f
        st_ref[d, 1] = r
        st_ref[d, 2] = jnp.exp(a - a_max)
        st_ref[d, 3] = b_last
        st_ref[d, 4] = a_max

    def m_scan(i, carry):
        new = []
        for d, c in ((0, i), (1, nc - 1 - i)):
            row = pl.ds(c, 1)
            st_ref[d, 5, row, :] = carry[d]
            new.append(jnp.maximum(st_ref[d, 3, row, :] + carry[d], st_ref[d, 4, row, :]))
        return tuple(new)

    m0 = jnp.zeros((1, L), F32)
    lax.fori_loop(0, nc, m_scan, (m0, m0))
    for d in range(2):
        b_last, a_max, m_prev = st_ref[d, 3], st_ref[d, 4], st_ref[d, 5]
        m_new = jnp.maximum(b_last + m_prev, a_max)
        st_ref[d, 3] = jnp.exp(b_last + m_prev - m_new)
        st_ref[d, 2] = st_ref[d, 2] * jnp.exp(a_max - m_new)
        st_ref[d, 0] = st_ref[d, 0] * LOG2E
        st_ref[d, 1] = st_ref[d, 1] * LOG2E
        st_ref[d, 5] = m_prev * LOG2E

    c_ref[...] = jnp.zeros_like(c_ref)

    t_idx = lax.broadcasted_iota(jnp.int32, (L, L), 0)
    s_idx = lax.broadcasted_iota(jnp.int32, (L, L), 1)
    masks = (s_idx <= t_idx, s_idx >= t_idx)
    ones = jnp.ones((L, L), BF16)

    def tile3(x):
        return jnp.concatenate([x] * (dv // L + 1), axis=1)

    def body(i, carry, assign):
        items = []
        for u in range(unroll):
            c = i * unroll + u
            items += [(0, c), (1, nc - 1 - c)]

        work = []
        for d, c in items:
            rows = pl.ds(pl.multiple_of(c * L, L), L)
            qc = q_ref[0, rows, :]
            kt = kt_ref[0, c]
            vaug = jnp.concatenate([v_ref[0, rows, :], ones], axis=1)
            w = st_ref[d, 2, pl.ds(c, 1), :]
            scores = _dot(qc, kt)
            kv = _dot(kt * w.astype(BF16), vaug)
            work.append((rows, qc, vaug, scores, kv))

        states = [c_ref[0], c_ref[1]]
        prev_states = []
        for (d, c), (rows, qc, vaug, scores, kv) in zip(items, work):
            prev_states.append(states[d].astype(BF16))
            states[d] = tile3(st_ref[d, 3, pl.ds(c, 1), :]) * states[d] + kv
        c_ref[0] = states[0]
        c_ref[1] = states[1]

        for (d, c), (rows, qc, vaug, scores, kv), prev_state in zip(items, work, prev_states):
            log_f = st_ref[d, 0, pl.ds(c, 1), :]
            r = st_ref[d, 1, pl.ds(c, 1), :]
            m_prev = st_ref[d, 5, pl.ds(c, 1), :]
            mask = masks[d]
            r_masked = jnp.where(mask, r, NEG_INF)
            cm = jnp.max(r_masked, axis=1, keepdims=True)
            b_col = jnp.sum(jnp.where(mask, log_f, 0.0), axis=1, keepdims=True)
            mu = jnp.broadcast_to(jnp.maximum(m_prev[:, 0:1], cm), (L, L))
            decay = jnp.exp2(r_masked - mu)
            inter = jnp.exp2(m_prev - mu)
            lhs = jnp.concatenate([(scores * decay).astype(BF16), inter.astype(BF16) * qc], axis=1)
            both = _dot(lhs, jnp.concatenate([vaug, prev_state], axis=0))
            den = jnp.maximum(jnp.abs(both[:, dv:]), jnp.exp2(-(b_col + mu)))
            out = both[:, :dv] * tile3(1.0 / den)[:, :dv]
            if assign:
                o_ref[0, rows, :] = out
            else:
                o_ref[0, rows, :] += out
        return carry

    steps = nc // 2 // unroll
    lax.fori_loop(0, steps, functools.partial(body, assign=True), 0)
    lax.fori_loop(steps, 2 * steps, functools.partial(body, assign=False), 0)


def _mlstm_core(q, kt, v, rf, *, unroll=4):
    B, S, hdk = q.shape
    H = MLSTM_HEADS
    dk = hdk // H
    dv = v.shape[-1] // H
    nc = S // LANES
    return pl.pallas_call(
        functools.partial(_mlstm_core_kernel, nc=nc, dv=dv, unroll=min(unroll, nc // 2)),
        grid=(B, H),
        in_specs=[pl.BlockSpec((1, S, dk), lambda b, h: (b, 0, h)),
                  pl.BlockSpec((1, nc, dk, LANES), lambda b, h: (b, 0, h, 0)),
                  pl.BlockSpec((1, S, dv), lambda b, h: (b, 0, h)),
                  pl.BlockSpec((1, rf.shape[1], LANES), lambda b, h: (b, 0, 0))],
        out_specs=pl.BlockSpec((1, S, dv), lambda b, h: (b, 0, h)),
        out_shape=jax.ShapeDtypeStruct((B, S, H * dv), F32),
        scratch_shapes=[pltpu.VMEM((2, 6, nc, LANES), F32),
                        pltpu.VMEM((2, dk, dv + LANES), F32)],
        compiler_params=_params(),
        name="mlstm_core",
    )(q, kt, v, rf)


def _attn_proj_kernel(x_ref, nw_ref, wt_ref, pos_ref, invf_ref, qt_ref, k_ref, vt_ref, *, tm, nsub):
    hd = ATTN_HEAD_DIM
    half = hd // 2
    nq = ATTN_Q_HEADS * hd
    nk = ATTN_KV_HEADS * hd
    scale = LOG2E * hd ** -0.5
    piece = 4 * hd
    sub = tm // nsub
    for r in range(nsub):
        rs = slice(r * sub, (r + 1) * sub)
        hn = _rmsnorm(x_ref[0, rs, :], nw_ref[...]).astype(BF16)
        ang = invf_ref[...] * pos_ref[0, :, rs].astype(F32)
        cos = jnp.cos(ang)
        sin = jnp.sin(ang)
        chunks = [(r * (sub // LANES) + jj, slice(jj * LANES, (jj + 1) * LANES))
                  for jj in range(sub // LANES)]

        def proj(lo, hi):
            return _dot_nt(wt_ref[lo:hi, :], hn)

        def rope(xh):
            x1, x2 = xh[:half], xh[half:]
            return jnp.concatenate([x1 * cos - x2 * sin, x2 * cos + x1 * sin], axis=0)

        vt = proj(nq + nk, nq + 2 * nk).astype(BF16)
        for j, cols in chunks:
            vt_ref[0, j] = vt[:, cols]
        kp = proj(nq, nq + nk)
        kt = jnp.concatenate([rope(kp[g * hd:(g + 1) * hd]) for g in range(ATTN_KV_HEADS)], axis=0)
        for j, cols in chunks:
            for f in range(nk // LANES):
                k_ref[0, j * LANES:(j + 1) * LANES, f * LANES:(f + 1) * LANES] = (
                    kt[f * LANES:(f + 1) * LANES, cols].T.astype(BF16))
        for lo in range(0, nq, piece):
            qp = proj(lo, lo + piece)
            qt = jnp.concatenate([rope(qp[h * hd:(h + 1) * hd]) * scale for h in range(piece // hd)],
                                 axis=0).astype(BF16)
            for j, cols in chunks:
                qt_ref[0, j, lo:lo + piece, :] = qt[:, cols]


def _attn_proj(x, norm_w, wt, pos_rows, inv_freq, *, tm=1024, nsub=4):
    B, S, D = x.shape
    nq = ATTN_Q_HEADS * ATTN_HEAD_DIM
    nk = ATTN_KV_HEADS * ATTN_HEAD_DIM
    nj = tm // LANES
    nb = S // LANES
    return pl.pallas_call(
        functools.partial(_attn_proj_kernel, tm=tm, nsub=nsub),
        grid=(B, S // tm),
        in_specs=[pl.BlockSpec((1, tm, D), lambda b, i: (b, i, 0)), _resident((1, D)),
                  _resident((nq + 2 * nk, D)),
                  pl.BlockSpec((1, 1, tm), lambda b, i: (b, 0, i)),
                  _resident((ATTN_HEAD_DIM // 2, 1))],
        out_specs=[pl.BlockSpec((1, nj, nq, LANES), lambda b, i: (b, i, 0, 0)),
                   pl.BlockSpec((1, tm, nk), lambda b, i: (b, i, 0)),
                   pl.BlockSpec((1, nj, nk, LANES), lambda b, i: (b, i, 0, 0))],
        out_shape=[jax.ShapeDtypeStruct((B, nb, nq, LANES), BF16),
                   jax.ShapeDtypeStruct((B, S, nk), BF16),
                   jax.ShapeDtypeStruct((B, nb, nk, LANES), BF16)],
        compiler_params=_params(),
        name="attn_proj",
    )(x, norm_w.reshape(1, D), wt, pos_rows, inv_freq)


def _attn_core_kernel(qt_ref, kp_ref, kc_ref, kn_ref, vp_ref, vc_ref, vn_ref, sink_ref, o_ref,
                      k_all, v_all, *, nb, nblk):
    L = LANES
    hd = ATTN_HEAD_DIM
    G = ATTN_GROUP
    c = pl.program_id(1)
    k_all[0:L] = kp_ref[0]
    k_all[L:(nblk + 1) * L] = kc_ref[0]
    k_all[(nblk + 1) * L:] = kn_ref[0]
    v_all[0] = vp_ref[0, 0]
    v_all[1:nblk + 1] = vc_ref[0]
    v_all[nblk + 1] = vn_ref[0, 0]

    key = lax.broadcasted_iota(jnp.int32, (L, G * L), 0)
    qry = lax.broadcasted_iota(jnp.int32, (L, G * L), 1) % L
    prev_mask = jnp.where(key >= qry, 0.0, NEG_INF)
    next_mask = jnp.where(key <= qry, 0.0, NEG_INF)
    zeros = jnp.zeros((hd, G * L), BF16)
    ones = jnp.ones((16, 3 * L), BF16)

    def scores(j, g):
        qg = jnp.concatenate([qt_ref[0, j, (G * g + i) * hd:(G * g + i + 1) * hd, :] for i in range(G)],
                             axis=1)
        qz = jnp.concatenate([qg, zeros] if g % 2 == 0 else [zeros, qg], axis=0)
        return _dot(k_all[j * L:(j + 3) * L, (g // 2) * L:(g // 2 + 1) * L], qz)

    items = [(j, g) for j in range(nblk) for g in range(ATTN_KV_HEADS)]
    ahead = 2
    pending = [scores(*it) for it in items[:ahead]]
    for n, (j, g) in enumerate(items):
        s = pending.pop(0)
        if n + ahead < len(items):
            pending.append(scores(*items[n + ahead]))
        if g == 0:
            blk = c * nblk + j
            prev_bias = prev_mask + jnp.where(blk > 0, 0.0, NEG_INF)
            next_bias = next_mask + jnp.where(blk < nb - 1, 0.0, NEG_INF)
        s = [s[0:L] + prev_bias, s[L:2 * L], s[2 * L:] + next_bias]
        sink = sink_ref[g:g + 1, :]
        m = sink
        for sj in s:
            m = jnp.maximum(m, jnp.max(sj, axis=0, keepdims=True))
        p = jnp.concatenate([jnp.exp2(sj - m).astype(BF16) for sj in s], axis=0)
        vg = jnp.concatenate([v_all[j + i, g * hd:(g + 1) * hd, :] for i in range(3)], axis=1)
        acc = _dot(jnp.concatenate([vg, ones], axis=0), p)
        denom = acc[hd:hd + 1] + jnp.exp2(sink - m)
        out = acc[:hd] * (1.0 / denom)
        for pair in range(G // 2):
            tile = jnp.concatenate([out[:, (2 * pair) * L:(2 * pair + 1) * L],
                                    out[:, (2 * pair + 1) * L:(2 * pair + 2) * L]], axis=0)
            col = (G * g + 2 * pair) * hd
            o_ref[0, j * L:(j + 1) * L, col:col + L] = tile.T.astype(BF16)


def _attn_core(qt, k, vt, sink_rows, *, nblk=4):
    B, nb, nq, L = qt.shape
    nk = k.shape[-1]
    S = nb * L
    nblk = min(nblk, nb)
    prev = lambda b, c: (b, jnp.maximum(c * nblk - 1, 0), 0)
    cur = lambda b, c: (b, c, 0)
    nxt = lambda b, c: (b, jnp.minimum((c + 1) * nblk, nb - 1), 0)
    four = lambda f: (lambda b, c: f(b, c) + (0,))
    return pl.pallas_call(
        functools.partial(_attn_core_kernel, nb=nb, nblk=nblk),
        grid=(B, nb // nblk),
        in_specs=[pl.BlockSpec((1, nblk, nq, L), four(cur)),
                  pl.BlockSpec((1, L, nk), prev), pl.BlockSpec((1, nblk * L, nk), cur),
                  pl.BlockSpec((1, L, nk), nxt),
                  pl.BlockSpec((1, 1, nk, L), four(prev)), pl.BlockSpec((1, nblk, nk, L), four(cur)),
                  pl.BlockSpec((1, 1, nk, L), four(nxt)),
                  _resident(sink_rows.shape)],
        out_specs=pl.BlockSpec((1, nblk * L, nq), cur),
        out_shape=jax.ShapeDtypeStruct((B, S, nq), BF16),
        scratch_shapes=[pltpu.VMEM(((nblk + 2) * L, nk), BF16),
                        pltpu.VMEM((nblk + 2, nk, L), BF16)],
        compiler_params=_params(),
        name="attn_core",
    )(qt, k, k, k, vt, vt, vt, sink_rows)


def _mlstm_mixer(x, norm_w, w_in, b_gate):
    D = x.shape[-1]
    H = MLSTM_HEADS
    hdk = D // 2
    w = w_in.astype(BF16)
    wq = w[:, :hdk]
    wkt = w[:, hdk:2 * hdk].T
    wvo = w[:, 2 * hdk:2 * hdk + 2 * D]
    wgt = w[:, 2 * hdk + 2 * D:].T.reshape(4, H, D).transpose(1, 0, 2).reshape(4 * H, D)
    bias = b_gate.astype(F32).reshape(4, H).T.reshape(4 * H, 1)
    bias_rows = jnp.broadcast_to(bias, (4 * H, LANES))
    q, kt, v, og, rf = _mlstm_proj(x, norm_w, wq, wkt, wvo, wgt, bias_rows)
    return _mlstm_core(q, kt, v, rf), og


def _attn_mixer(x, pos_rows, inv_freq, norm_w, w_in, sink):
    G, L = ATTN_GROUP, LANES
    qt, k, vt = _attn_proj(x, norm_w, w_in.astype(BF16).T, pos_rows, inv_freq)
    sink_rows = jnp.repeat(LOG2E * sink.astype(F32).reshape(ATTN_KV_HEADS, G), L, axis=1)
    return _attn_core(qt, k, vt, sink_rows)


def kernel(x, positions, norm_mix_w, norm_ffn_w, norm_final_w, mlstm_w_in, mlstm_b_gate, mlstm_norm_w, mlstm_w_out, attn_w_in, attn_sink, attn_w_out, ffn_w_in, ffn_w_out):
    depth = norm_mix_w.shape[0]
    B, S = positions.shape
    half = ATTN_HEAD_DIM // 2
    inv_freq = (ROPE_THETA ** (-jnp.arange(half, dtype=F32) / half)).reshape(half, 1)
    pos_rows = positions.reshape(B, 1, S)
    mlstm_wo, attn_wo = mlstm_w_out.astype(BF16), attn_w_out.astype(BF16)
    ffn_wi, ffn_wo = ffn_w_in.astype(BF16), ffn_w_out.astype(BF16)
    for i in range(depth):
        j = i // 2
        final_w = norm_final_w if i == depth - 1 else None
        if i % 2 == 0:
            hs, og = _mlstm_mixer(x, norm_mix_w[i], mlstm_w_in[j], mlstm_b_gate[j])
            x = _layer_tail("mlstm", (hs, og, mlstm_norm_w[j]), x, mlstm_wo, j, norm_ffn_w[i],
                            ffn_wi, ffn_wo, i, final_w)
        else:
            a = _attn_mixer(x, pos_rows, inv_freq, norm_mix_w[i], attn_w_in[j], attn_sink[j])
            x = _layer_tail("attn", (a,), x, attn_wo, j, norm_ffn_w[i], ffn_wi, ffn_wo, i, final_w)
    return x
```

```python
import functools

import jax
import jax.numpy as jnp
from jax import lax
from jax.experimental import pallas as pl
from jax.experimental.pallas import tpu as pltpu

F32 = jnp.float32
BF16 = jnp.bfloat16

EPS = 1e-6
LANES = 128

MLSTM_HEADS = 4
MLSTM_CHUNK = 128
ATTN_HEAD_DIM = 64
ATTN_Q_HEADS = 16
ATTN_KV_HEADS = 4
ATTN_GROUP = ATTN_Q_HEADS // ATTN_KV_HEADS
ATTN_BLOCK = 128
ROPE_THETA = 10000.0

NEG_INF = float("-inf")
LOG2E = 1.4426950408889634
VMEM_LIMIT = 56 * 1024 * 1024


def _params():
    return pltpu.CompilerParams(dimension_semantics=("arbitrary", "arbitrary"),
                                vmem_limit_bytes=VMEM_LIMIT)


def _resident(shape):
    return pl.BlockSpec(shape, lambda *_: (0,) * len(shape), pipeline_mode=pl.Buffered(1))


def _rmsnorm(x, w):
    ms = jnp.mean(x * x, axis=-1, keepdims=True)
    return x * lax.rsqrt(ms + EPS) * w


def _sigmoid(x):
    return 1.0 / (1.0 + jnp.exp(-x))


def _dot(a, b):
    return jnp.dot(a, b, preferred_element_type=F32)


def _dot_nt(a, b):
    return lax.dot_general(a, b, (((1,), (1,)), ((), ())), preferred_element_type=F32)


def _layer_tail_kernel(*refs, d_ff, tf, nsub, mixer, final):
    it = iter(refs)
    if mixer == "mlstm":
        hs_ref, og_ref, hw_ref = next(it), next(it), next(it)
    else:
        a_ref = next(it)
    x_ref, wo_ref, nw_ref, win_ref, wout_ref = (next(it) for _ in range(5))
    fw_ref = next(it) if final else None
    o_ref, x1_ref, hn_ref, act_ref = (next(it) for _ in range(4))

    tm = x_ref.shape[1]
    subs = [slice(r * (tm // nsub), (r + 1) * (tm // nsub)) for r in range(nsub)]
    for rs in subs:
        if mixer == "mlstm":
            dv = hs_ref.shape[-1] // MLSTM_HEADS
            parts = []
            for h in range(MLSTM_HEADS):
                hs = hs_ref[0, rs, h * dv:(h + 1) * dv]
                ms = jnp.mean(hs * hs, axis=-1, keepdims=True)
                parts.append(hs * lax.rsqrt(ms + EPS))
            y = (jnp.concatenate(parts, axis=-1) * hw_ref[...] * og_ref[0, rs, :].astype(F32)).astype(BF16)
        else:
            y = a_ref[0, rs, :]
        x1_ref[rs, :] = x_ref[0, rs, :] + _dot(y, wo_ref[...])
    for rs in subs:
        hn_ref[rs, :] = _rmsnorm(x1_ref[rs, :], nw_ref[...]).astype(BF16)
    for rs in subs:
        for j in range(d_ff // tf):
            hn = hn_ref[rs, :]
            g = _dot(hn, win_ref[:, j * tf:(j + 1) * tf])
            u = _dot(hn, win_ref[:, d_ff + j * tf:d_ff + (j + 1) * tf])
            act_ref[rs, j * tf:(j + 1) * tf] = (g * _sigmoid(g) * u).astype(BF16)
    for rs in subs:
        out = x1_ref[rs, :] + _dot(act_ref[rs, :], wout_ref[...])
        if final:
            out = _rmsnorm(out, fw_ref[...])
        o_ref[0, rs, :] = out


def _layer_tail(mixer, mixer_args, x, w_o, layer_o, norm_w, w_in, w_out, layer, final_w=None,
                *, tm=512, tf=256, nsub=2):
    B, S, D = x.shape
    tm = min(tm, S)
    d_ff = w_out.shape[1]
    final = final_w is not None
    row = pl.BlockSpec((1, tm, D), lambda b, i: (b, i, 0))

    def stacked(shape, l):
        return pl.BlockSpec((None,) + shape, lambda b, i: (l,) + (0,) * len(shape),
                            pipeline_mode=pl.Buffered(1))

    if mixer == "mlstm":
        hs, og, head_w = mixer_args
        args = [hs, og, head_w.reshape(1, D)]
        in_specs = [row, row, _resident((1, D))]
    else:
        args = list(mixer_args)
        in_specs = [row]
    args += [x, w_o, norm_w.reshape(1, D), w_in, w_out]
    in_specs += [row, stacked((w_o.shape[1], D), layer_o), _resident((1, D)),
                 stacked((D, 2 * d_ff), layer), stacked((d_ff, D), layer)]
    if final:
        args.append(final_w.reshape(1, D))
        in_specs.append(_resident((1, D)))
    return pl.pallas_call(
        functools.partial(_layer_tail_kernel, d_ff=d_ff, tf=tf, nsub=nsub, mixer=mixer, final=final),
        grid=(B, S // tm),
        in_specs=in_specs,
        out_specs=row,
        out_shape=jax.ShapeDtypeStruct((B, S, D), F32),
        scratch_shapes=[pltpu.VMEM((tm, D), F32), pltpu.VMEM((tm, D), BF16),
                        pltpu.VMEM((tm, d_ff), BF16)],
        compiler_params=_params(),
        name=mixer + "_tail",
    )(*args)


def _mlstm_proj_kernel(x_ref, nw_ref, w_ref, wgt_ref, bias_ref,
                       q_ref, kt_ref, v_ref, og_ref, rf_ref, wq_ref, wkt_ref, wvo_ref, *, dk, tm, nsub):
    d = v_ref.shape[-1]
    hdk = q_ref.shape[-1]
    ng = wgt_ref.shape[0]
    sub = tm // nsub

    @pl.when((pl.program_id(0) == 0) & (pl.program_id(1) == 0))
    def _():
        for c in range(0, hdk, LANES):
            wq_ref[:, c:c + LANES] = w_ref[:, c:c + LANES].astype(BF16)
            wkt_ref[c:c + LANES, :] = w_ref[:, hdk + c:hdk + c + LANES].T.astype(BF16)
        for c in range(0, 2 * d, LANES):
            wvo_ref[:, c:c + LANES] = w_ref[:, 2 * hdk + c:2 * hdk + c + LANES].astype(BF16)

    for r in range(nsub):
        rs = slice(r * sub, (r + 1) * sub)
        hn = _rmsnorm(x_ref[0, rs, :], nw_ref[...]).astype(BF16)
        og_ref[0, rs, :] = _sigmoid(_dot(hn, wvo_ref[:, d:])).astype(BF16)
        v_ref[0, rs, :] = _dot(hn, wvo_ref[:, :d]).astype(BF16)
        q_ref[0, rs, :] = (_dot(hn, wq_ref[...]) * (dk ** -0.5)).astype(BF16)
        kt = _dot_nt(wkt_ref[...], hn).astype(BF16)
        gt = _dot_nt(wgt_ref[...], hn)
        for jj in range(sub // LANES):
            j = r * (sub // LANES) + jj
            kt_ref[0, j] = kt[:, jj * LANES:(jj + 1) * LANES]
            rf_ref[0, j * ng:(j + 1) * ng, :] = gt[:, jj * LANES:(jj + 1) * LANES] + bias_ref[...]


def _mlstm_proj(x, norm_w, w_in, layer, wgt, bias_rows, *, tm=1024, nsub=4):
    B, S, D = x.shape
    H = MLSTM_HEADS
    hdk = D // 2
    dk = hdk // H
    tm = min(tm, S)
    ng = wgt.shape[0]
    nj = tm // LANES
    NC = S // LANES
    row = lambda b, i: (b, i, 0)
    return pl.pallas_call(
        functools.partial(_mlstm_proj_kernel, dk=dk, tm=tm, nsub=nsub),
        grid=(B, S // tm),
        in_specs=[pl.BlockSpec((1, tm, D), row), _resident((1, D)),
                  pl.BlockSpec((None,) + w_in.shape[1:], lambda b, i: (layer, 0, 0),
                               pipeline_mode=pl.Buffered(1)),
                  _resident((ng, D)), _resident((ng, LANES))],
        out_specs=[pl.BlockSpec((1, tm, hdk), row),
                   pl.BlockSpec((1, nj, hdk, LANES), lambda b, i: (b, i, 0, 0)),
                   pl.BlockSpec((1, tm, D), row),
                   pl.BlockSpec((1, tm, D), row),
                   pl.BlockSpec((1, nj * ng, LANES), row)],
        out_shape=[jax.ShapeDtypeStruct((B, S, hdk), BF16),
                   jax.ShapeDtypeStruct((B, NC, hdk, LANES), BF16),
                   jax.ShapeDtypeStruct((B, S, D), BF16),
                   jax.ShapeDtypeStruct((B, S, D), BF16),
                   jax.ShapeDtypeStruct((B, NC * ng, LANES), F32)],
        scratch_shapes=[pltpu.VMEM((D, hdk), BF16), pltpu.VMEM((hdk, D), BF16),
                        pltpu.VMEM((D, 2 * D), BF16)],
        compiler_params=_params(),
        name="mlstm_proj",
    )(x, norm_w.reshape(1, D), w_in, wgt, bias_rows)


def _log_sigmoid(x):
    return jnp.minimum(x, 0.0) - jnp.log1p(jnp.exp(-jnp.abs(x)))


def _mlstm_core_kernel(q_ref, kt_ref, v_ref, rf_ref, o_ref, st_ref, c_ref, *, nc, dv, unroll):
    L = LANES
    h = pl.program_id(1)
    ng = 4 * MLSTM_HEADS

    def gate_rows(g):
        return rf_ref[0, pl.ds(4 * h + g, nc, stride=ng), :]

    lane = lax.broadcasted_iota(jnp.int32, (nc, L), 1)

    def prefix_sum(x):
        for s in (1, 2, 4, 8, 16, 32, 64):
            x = x + jnp.where(lane >= s, pltpu.roll(x, s, axis=1), 0.0)
        return x

    def suffix_sum(x):
        for s in (1, 2, 4, 8, 16, 32, 64):
            x = x + jnp.where(lane < L - s, pltpu.roll(x, L - s, axis=1), 0.0)
        return x

    for d in range(2):
        log_i = gate_rows(2 * d)
        log_f = _log_sigmoid(gate_rows(2 * d + 1))
        if d == 0:
            b = prefix_sum(log_f)
            b_last = jnp.broadcast_to(b[:, L - 1:L], (nc, L))
        else:
            b = suffix_sum(log_f)
            b_last = jnp.broadcast_to(b[:, 0:1], (nc, L))
        r = log_i - b
        a = b_last + r
        a_max = jnp.broadcast_to(jnp.max(a, axis=1, keepdims=True), (nc, L))
        st_ref[d, 0] = log_f
        st_ref[d, 1] = r
        st_ref[d, 2] = jnp.exp(a - a_max)
        st_ref[d, 3] = b_last
        st_ref[d, 4] = a_max

    def m_scan(i, carry):
        new = []
        for d, c in ((0, i), (1, nc - 1 - i)):
            row = pl.ds(c, 1)
            st_ref[d, 5, row, :] = carry[d]
            new.append(jnp.maximum(st_ref[d, 3, row, :] + carry[d], st_ref[d, 4, row, :]))
        return tuple(new)

    m0 = jnp.zeros((1, L), F32)
    lax.fori_loop(0, nc, m_scan, (m0, m0))
    for d in range(2):
        b_last, a_max, m_prev = st_ref[d, 3], st_ref[d, 4], st_ref[d, 5]
        m_new = jnp.maximum(b_last + m_prev, a_max)
        st_ref[d, 3] = jnp.exp(b_last + m_prev - m_new)
        st_ref[d, 2] = st_ref[d, 2] * jnp.exp(a_max - m_new)
        st_ref[d, 0] = st_ref[d, 0] * LOG2E
        st_ref[d, 1] = st_ref[d, 1] * LOG2E
        st_ref[d, 5] = m_prev * LOG2E

    c_ref[...] = jnp.zeros_like(c_ref)

    t_idx = lax.broadcasted_iota(jnp.int32, (L, L), 0)
    s_idx = lax.broadcasted_iota(jnp.int32, (L, L), 1)
    masks = (s_idx <= t_idx, s_idx >= t_idx)
    ones = jnp.ones((L, L), BF16)

    def tile3(x):
        return jnp.concatenate([x] * (dv // L + 1), axis=1)

    def body(i, carry, assign):
        items = []
        for u in range(unroll):
            c = i * unroll + u
            items += [(0, c), (1, nc - 1 - c)]

        work = []
        for d, c in items:
            rows = pl.ds(pl.multiple_of(c * L, L), L)
            qc = q_ref[0, rows, :]
            kt = kt_ref[0, c]
            vaug = jnp.concatenate([v_ref[0, rows, :], ones], axis=1)
            w = st_ref[d, 2, pl.ds(c, 1), :]
            scores = _dot(qc, kt)
            kv = _dot(kt * w.astype(BF16), vaug)
            work.append((rows, qc, vaug, scores, kv))

        states = [c_ref[0], c_ref[1]]
        prev_states = []
        for (d, c), (rows, qc, vaug, scores, kv) in zip(items, work):
            prev_states.append(states[d].astype(BF16))
            states[d] = tile3(st_ref[d, 3, pl.ds(c, 1), :]) * states[d] + kv
        c_ref[0] = states[0]
        c_ref[1] = states[1]

        for (d, c), (rows, qc, vaug, scores, kv), prev_state in zip(items, work, prev_states):
            log_f = st_ref[d, 0, pl.ds(c, 1), :]
            r = st_ref[d, 1, pl.ds(c, 1), :]
            m_prev = st_ref[d, 5, pl.ds(c, 1), :]
            mask = masks[d]
            r_masked = jnp.where(mask, r, NEG_INF)
            cm = jnp.max(r_masked, axis=1, keepdims=True)
            b_col = jnp.sum(jnp.where(mask, log_f, 0.0), axis=1, keepdims=True)
            mu = jnp.broadcast_to(jnp.maximum(m_prev[:, 0:1], cm), (L, L))
            decay = jnp.exp2(r_masked - mu)
            inter = jnp.exp2(m_prev - mu)
            lhs = jnp.concatenate([(scores * decay).astype(BF16), inter.astype(BF16) * qc], axis=1)
            both = _dot(lhs, jnp.concatenate([vaug, prev_state], axis=0))
            den = jnp.maximum(jnp.abs(both[:, dv:]), jnp.exp2(-(b_col + mu)))
            out = both[:, :dv] * tile3(1.0 / den)[:, :dv]
            if assign:
                o_ref[0, rows, :] = out
            else:
                o_ref[0, rows, :] += out
        return carry

    steps = nc // 2 // unroll
    lax.fori_loop(0, steps, functools.partial(body, assign=True), 0)
    lax.fori_loop(steps, 2 * steps, functools.partial(body, assign=False), 0)


def _mlstm_core(q, kt, v, rf, *, unroll=4):
    B, S, hdk = q.shape
    H = MLSTM_HEADS
    dk = hdk // H
    dv = v.shape[-1] // H
    nc = S // LANES
    return pl.pallas_call(
        functools.partial(_mlstm_core_kernel, nc=nc, dv=dv, unroll=min(unroll, nc // 2)),
        grid=(B, H),
        in_specs=[pl.BlockSpec((1, S, dk), lambda b, h: (b, 0, h)),
                  pl.BlockSpec((1, nc, dk, LANES), lambda b, h: (b, 0, h, 0)),
                  pl.BlockSpec((1, S, dv), lambda b, h: (b, 0, h)),
                  pl.BlockSpec((1, rf.shape[1], LANES), lambda b, h: (b, 0, 0))],
        out_specs=pl.BlockSpec((1, S, dv), lambda b, h: (b, 0, h)),
        out_shape=jax.ShapeDtypeStruct((B, S, H * dv), F32),
        scratch_shapes=[pltpu.VMEM((2, 6, nc, LANES), F32),
                        pltpu.VMEM((2, dk, dv + LANES), F32)],
        compiler_params=_params(),
        name="mlstm_core",
    )(q, kt, v, rf)


def _attn_proj_kernel(x_ref, nw_ref, w_ref, pos_ref, invf_ref, qt_ref, k_ref, vt_ref, wt_ref, *, tm, nsub):
    hd = ATTN_HEAD_DIM

    @pl.when((pl.program_id(0) == 0) & (pl.program_id(1) == 0))
    def _():
        for c in range(0, wt_ref.shape[0], LANES):
            wt_ref[c:c + LANES, :] = w_ref[:, c:c + LANES].T.astype(BF16)

    half = hd // 2
    nq = ATTN_Q_HEADS * hd
    nk = ATTN_KV_HEADS * hd
    scale = LOG2E * hd ** -0.5
    piece = 4 * hd
    sub = tm // nsub
    for r in range(nsub):
        rs = slice(r * sub, (r + 1) * sub)
        hn = _rmsnorm(x_ref[0, rs, :], nw_ref[...]).astype(BF16)
        ang = invf_ref[...] * pos_ref[0, :, rs].astype(F32)
        cos = jnp.cos(ang)
        sin = jnp.sin(ang)
        chunks = [(r * (sub // LANES) + jj, slice(jj * LANES, (jj + 1) * LANES))
                  for jj in range(sub // LANES)]

        def proj(lo, hi):
            return _dot_nt(wt_ref[lo:hi, :], hn)

        def rope(xh):
            x1, x2 = xh[:half], xh[half:]
            return jnp.concatenate([x1 * cos - x2 * sin, x2 * cos + x1 * sin], axis=0)

        vt = proj(nq + nk, nq + 2 * nk).astype(BF16)
        for j, cols in chunks:
            vt_ref[0, j] = vt[:, cols]
        kp = proj(nq, nq + nk)
        kt = jnp.concatenate([rope(kp[g * hd:(g + 1) * hd]) for g in range(ATTN_KV_HEADS)], axis=0)
        for j, cols in chunks:
            for f in range(nk // LANES):
                k_ref[0, j * LANES:(j + 1) * LANES, f * LANES:(f + 1) * LANES] = (
                    kt[f * LANES:(f + 1) * LANES, cols].T.astype(BF16))
        for lo in range(0, nq, piece):
            qp = proj(lo, lo + piece)
            qt = jnp.concatenate([rope(qp[h * hd:(h + 1) * hd]) * scale for h in range(piece // hd)],
                                 axis=0).astype(BF16)
            for j, cols in chunks:
                qt_ref[0, j, lo:lo + piece, :] = qt[:, cols]


def _attn_proj(x, norm_w, w_in, layer, pos_rows, inv_freq, *, tm=1024, nsub=4):
    B, S, D = x.shape
    tm = min(tm, S)
    nq = ATTN_Q_HEADS * ATTN_HEAD_DIM
    nk = ATTN_KV_HEADS * ATTN_HEAD_DIM
    nj = tm // LANES
    nb = S // LANES
    return pl.pallas_call(
        functools.partial(_attn_proj_kernel, tm=tm, nsub=nsub),
        grid=(B, S // tm),
        in_specs=[pl.BlockSpec((1, tm, D), lambda b, i: (b, i, 0)), _resident((1, D)),
                  pl.BlockSpec((None,) + w_in.shape[1:], lambda b, i: (layer, 0, 0),
                               pipeline_mode=pl.Buffered(1)),
                  pl.BlockSpec((1, 1, tm), lambda b, i: (b, 0, i)),
                  _resident((ATTN_HEAD_DIM // 2, 1))],
        out_specs=[pl.BlockSpec((1, nj, nq, LANES), lambda b, i: (b, i, 0, 0)),
                   pl.BlockSpec((1, tm, nk), lambda b, i: (b, i, 0)),
                   pl.BlockSpec((1, nj, nk, LANES), lambda b, i: (b, i, 0, 0))],
        out_shape=[jax.ShapeDtypeStruct((B, nb, nq, LANES), BF16),
                   jax.ShapeDtypeStruct((B, S, nk), BF16),
                   jax.ShapeDtypeStruct((B, nb, nk, LANES), BF16)],
        scratch_shapes=[pltpu.VMEM((nq + 2 * nk, D), BF16)],
        compiler_params=_params(),
        name="attn_proj",
    )(x, norm_w.reshape(1, D), w_in, pos_rows, inv_freq)


def _attn_core_kernel(qt_ref, kp_ref, kc_ref, kn_ref, vp_ref, vc_ref, vn_ref, sink_ref, o_ref,
                      k_all, v_all, *, nb, nblk):
    L = LANES
    hd = ATTN_HEAD_DIM
    G = ATTN_GROUP
    c = pl.program_id(1)
    k_all[0:L] = kp_ref[0]
    k_all[L:(nblk + 1) * L] = kc_ref[0]
    k_all[(nblk + 1) * L:] = kn_ref[0]
    v_all[0] = vp_ref[0, 0]
    v_all[1:nblk + 1] = vc_ref[0]
    v_all[nblk + 1] = vn_ref[0, 0]

    key = lax.broadcasted_iota(jnp.int32, (L, G * L), 0)
    qry = lax.broadcasted_iota(jnp.int32, (L, G * L), 1) % L
    prev_mask = jnp.where(key >= qry, 0.0, NEG_INF)
    next_mask = jnp.where(key <= qry, 0.0, NEG_INF)
    zeros = jnp.zeros((hd, G * L), BF16)
    ones = jnp.ones((16, 3 * L), BF16)

    def scores(j, g):
        qg = jnp.concatenate([qt_ref[0, j, (G * g + i) * hd:(G * g + i + 1) * hd, :] for i in range(G)],
                             axis=1)
        qz = jnp.concatenate([qg, zeros] if g % 2 == 0 else [zeros, qg], axis=0)
        return _dot(k_all[j * L:(j + 3) * L, (g // 2) * L:(g // 2 + 1) * L], qz)

    items = [(j, g) for j in range(nblk) for g in range(ATTN_KV_HEADS)]
    ahead = 2
    pending = [scores(*it) for it in items[:ahead]]
    for n, (j, g) in enumerate(items):
        s = pending.pop(0)
        if n + ahead < len(items):
            pending.append(scores(*items[n + ahead]))
        if g == 0:
            blk = c * nblk + j
            prev_bias = prev_mask + jnp.where(blk > 0, 0.0, NEG_INF)
            next_bias = next_mask + jnp.where(blk < nb - 1, 0.0, NEG_INF)
        s = [s[0:L] + prev_bias, s[L:2 * L], s[2 * L:] + next_bias]
        sink = sink_ref[g:g + 1, :]
        m = sink
        for sj in s:
            m = jnp.maximum(m, jnp.max(sj, axis=0, keepdims=True))
        p = jnp.concatenate([jnp.exp2(sj - m).astype(BF16) for sj in s], axis=0)
        vg = jnp.concatenate([v_all[j + i, g * hd:(g + 1) * hd, :] for i in range(3)], axis=1)
        acc = _dot(jnp.concatenate([vg, ones], axis=0), p)
        denom = acc[hd:hd + 1] + jnp.exp2(sink - m)
        out = acc[:hd] * (1.0 / denom)
        for pair in range(G // 2):
            tile = jnp.concatenate([out[:, (2 * pair) * L:(2 * pair + 1) * L],
                                    out[:, (2 * pair + 1) * L:(2 * pair + 2) * L]], axis=0)
            col = (G * g + 2 * pair) * hd
            o_ref[0, j * L:(j + 1) * L, col:col + L] = tile.T.astype(BF16)


def _attn_core(qt, k, vt, sink_rows, *, nblk=4):
    B, nb, nq, L = qt.shape
    nk = k.shape[-1]
    S = nb * L
    nblk = min(nblk, nb)
    prev = lambda b, c: (b, jnp.maximum(c * nblk - 1, 0), 0)
    cur = lambda b, c: (b, c, 0)
    nxt = lambda b, c: (b, jnp.minimum((c + 1) * nblk, nb - 1), 0)
    four = lambda f: (lambda b, c: f(b, c) + (0,))
    return pl.pallas_call(
        functools.partial(_attn_core_kernel, nb=nb, nblk=nblk),
        grid=(B, nb // nblk),
        in_specs=[pl.BlockSpec((1, nblk, nq, L), four(cur)),
                  pl.BlockSpec((1, L, nk), prev), pl.BlockSpec((1, nblk * L, nk), cur),
                  pl.BlockSpec((1, L, nk), nxt),
                  pl.BlockSpec((1, 1, nk, L), four(prev)), pl.BlockSpec((1, nblk, nk, L), four(cur)),
                  pl.BlockSpec((1, 1, nk, L), four(nxt)),
                  _resident(sink_rows.shape)],
        out_specs=pl.BlockSpec((1, nblk * L, nq), cur),
        out_shape=jax.ShapeDtypeStruct((B, S, nq), BF16),
        scratch_shapes=[pltpu.VMEM(((nblk + 2) * L, nk), BF16),
                        pltpu.VMEM((nblk + 2, nk, L), BF16)],
        compiler_params=_params(),
        name="attn_core",
    )(qt, k, k, k, vt, vt, vt, sink_rows)


def _mlstm_mixer(x, norm_w, w_in, layer, b_gate):
    D = x.shape[-1]
    H = MLSTM_HEADS
    wgt = w_in[layer, :, 3 * D:].T.reshape(4, H, D).transpose(1, 0, 2).reshape(4 * H, D).astype(BF16)
    bias = b_gate.astype(F32).reshape(4, H).T.reshape(4 * H, 1)
    bias_rows = jnp.broadcast_to(bias, (4 * H, LANES))
    q, kt, v, og, rf = _mlstm_proj(x, norm_w, w_in, layer, wgt, bias_rows)
    return _mlstm_core(q, kt, v, rf), og


def _attn_mixer(x, pos_rows, inv_freq, norm_w, w_in, layer, sink):
    G, L = ATTN_GROUP, LANES
    qt, k, vt = _attn_proj(x, norm_w, w_in, layer, pos_rows, inv_freq)
    sink_rows = jnp.repeat(LOG2E * sink.astype(F32).reshape(ATTN_KV_HEADS, G), L, axis=1)
    return _attn_core(qt, k, vt, sink_rows)


def kernel(x, positions, norm_mix_w, norm_ffn_w, norm_final_w, mlstm_w_in, mlstm_b_gate, mlstm_norm_w, mlstm_w_out, attn_w_in, attn_sink, attn_w_out, ffn_w_in, ffn_w_out):
    depth = norm_mix_w.shape[0]
    B, S = positions.shape
    half = ATTN_HEAD_DIM // 2
    inv_freq = (ROPE_THETA ** (-jnp.arange(half, dtype=F32) / half)).reshape(half, 1)
    pos_rows = positions.reshape(B, 1, S)
    mlstm_wo, attn_wo = mlstm_w_out.astype(BF16), attn_w_out.astype(BF16)
    ffn_wi, ffn_wo = ffn_w_in.astype(BF16), ffn_w_out.astype(BF16)
    for i in range(depth):
        j = i // 2
        final_w = norm_final_w if i == depth - 1 else None
        if i % 2 == 0:
            hs, og = _mlstm_mixer(x, norm_mix_w[i], mlstm_w_in, j, mlstm_b_gate[j])
            x = _layer_tail("mlstm", (hs, og, mlstm_norm_w[j]), x, mlstm_wo, j, norm_ffn_w[i],
                            ffn_wi, ffn_wo, i, final_w)
        else:
            a = _attn_mixer(x, pos_rows, inv_freq, norm_mix_w[i], attn_w_in, j, attn_sink[j])
            x = _layer_tail("attn", (a,), x, attn_wo, j, norm_ffn_w[i], ffn_wi, ffn_wo, i, final_w)
    return x
```

```python
import functools

import jax
import jax.numpy as jnp
from jax import lax
from jax.experimental import pallas as pl
from jax.experimental.pallas import tpu as pltpu

F32 = jnp.float32
BF16 = jnp.bfloat16

EPS = 1e-6
LANES = 128

MLSTM_HEADS = 4
MLSTM_CHUNK = 128
ATTN_HEAD_DIM = 64
ATTN_Q_HEADS = 16
ATTN_KV_HEADS = 4
ATTN_GROUP = ATTN_Q_HEADS // ATTN_KV_HEADS
ATTN_BLOCK = 128
ROPE_THETA = 10000.0

NEG_INF = float("-inf")
LOG2E = 1.4426950408889634
VMEM_LIMIT = 56 * 1024 * 1024


def _params():
    return pltpu.CompilerParams(dimension_semantics=("arbitrary", "arbitrary"),
                                vmem_limit_bytes=VMEM_LIMIT)


def _resident(shape):
    return pl.BlockSpec(shape, lambda *_: (0,) * len(shape), pipeline_mode=pl.Buffered(1))


def _rmsnorm(x, w):
    ms = jnp.mean(x * x, axis=-1, keepdims=True)
    return x * lax.rsqrt(ms + EPS) * w


def _sigmoid(x):
    return 1.0 / (1.0 + jnp.exp(-x))


def _dot(a, b):
    return jnp.dot(a, b, preferred_element_type=F32)


def _dot_nt(a, b):
    return lax.dot_general(a, b, (((1,), (1,)), ((), ())), preferred_element_type=F32)


def _layer_tail_kernel(*refs, d_ff, tf, nsub, mixer, final):
    it = iter(refs)
    if mixer == "mlstm":
        hs_ref, og_ref, hw_ref = next(it), next(it), next(it)
    else:
        a_ref = next(it)
    x_ref, wo_ref, nw_ref, win_ref, wout_ref = (next(it) for _ in range(5))
    fw_ref = next(it) if final else None
    o_ref, x1_ref, hn_ref, act_ref = (next(it) for _ in range(4))

    tm = x_ref.shape[1]
    subs = [slice(r * (tm // nsub), (r + 1) * (tm // nsub)) for r in range(nsub)]
    for rs in subs:
        if mixer == "mlstm":
            dv = hs_ref.shape[-1] // MLSTM_HEADS
            parts = []
            for h in range(MLSTM_HEADS):
                hs = hs_ref[0, rs, h * dv:(h + 1) * dv]
                ms = jnp.mean(hs * hs, axis=-1, keepdims=True)
                parts.append(hs * lax.rsqrt(ms + EPS))
            y = (jnp.concatenate(parts, axis=-1) * hw_ref[...] * og_ref[0, rs, :].astype(F32)).astype(BF16)
        else:
            y = a_ref[0, rs, :]
        x1_ref[rs, :] = x_ref[0, rs, :] + _dot(y, wo_ref[...])
    for rs in subs:
        hn_ref[rs, :] = _rmsnorm(x1_ref[rs, :], nw_ref[...]).astype(BF16)
    for rs in subs:
        for j in range(d_ff // tf):
            hn = hn_ref[rs, :]
            g = _dot(hn, win_ref[:, j * tf:(j + 1) * tf])
            u = _dot(hn, win_ref[:, d_ff + j * tf:d_ff + (j + 1) * tf])
            act_ref[rs, j * tf:(j + 1) * tf] = (g * _sigmoid(g) * u).astype(BF16)
    for rs in subs:
        out = x1_ref[rs, :] + _dot(act_ref[rs, :], wout_ref[...])
        if final:
            out = _rmsnorm(out, fw_ref[...])
        o_ref[0, rs, :] = out


def _layer_tail(mixer, mixer_args, x, w_o, layer_o, norm_w, w_in, w_out, final_w=None,
                *, tm=512, tf=256, nsub=2):
    B, S, D = x.shape
    tm = min(tm, S)
    d_ff = w_out.shape[0]
    final = final_w is not None
    row = pl.BlockSpec((1, tm, D), lambda b, i: (b, i, 0))

    def stacked(shape, l):
        return pl.BlockSpec((None,) + shape, lambda b, i: (l,) + (0,) * len(shape),
                            pipeline_mode=pl.Buffered(1))

    if mixer == "mlstm":
        hs, og, head_w = mixer_args
        args = [hs, og, head_w.reshape(1, D)]
        in_specs = [row, row, _resident((1, D))]
    else:
        args = list(mixer_args)
        in_specs = [row]
    args += [x, w_o, norm_w.reshape(1, D), w_in, w_out]
    in_specs += [row, stacked((w_o.shape[1], D), layer_o), _resident((1, D)),
                 _resident((D, 2 * d_ff)), _resident((d_ff, D))]
    if final:
        args.append(final_w.reshape(1, D))
        in_specs.append(_resident((1, D)))
    return pl.pallas_call(
        functools.partial(_layer_tail_kernel, d_ff=d_ff, tf=tf, nsub=nsub, mixer=mixer, final=final),
        grid=(B, S // tm),
        in_specs=in_specs,
        out_specs=row,
        out_shape=jax.ShapeDtypeStruct((B, S, D), F32),
        scratch_shapes=[pltpu.VMEM((tm, D), F32), pltpu.VMEM((tm, D), BF16),
                        pltpu.VMEM((tm, d_ff), BF16)],
        compiler_params=_params(),
        name=mixer + "_tail",
    )(*args)


def _mlstm_proj_kernel(x_ref, nw_ref, w_ref, wgt_ref, bias_ref,
                       q_ref, kt_ref, v_ref, og_ref, rf_ref, wq_ref, wkt_ref, wvo_ref, *, dk, tm, nsub):
    d = v_ref.shape[-1]
    hdk = q_ref.shape[-1]
    ng = wgt_ref.shape[0]
    sub = tm // nsub

    @pl.when((pl.program_id(0) == 0) & (pl.program_id(1) == 0))
    def _():
        for c in range(0, hdk, LANES):
            wq_ref[:, c:c + LANES] = w_ref[:, c:c + LANES].astype(BF16)
            wkt_ref[c:c + LANES, :] = w_ref[:, hdk + c:hdk + c + LANES].T.astype(BF16)
        for c in range(0, 2 * d, LANES):
            wvo_ref[:, c:c + LANES] = w_ref[:, 2 * hdk + c:2 * hdk + c + LANES].astype(BF16)

    for r in range(nsub):
        rs = slice(r * sub, (r + 1) * sub)
        hn = _rmsnorm(x_ref[0, rs, :], nw_ref[...]).astype(BF16)
        og_ref[0, rs, :] = _sigmoid(_dot(hn, wvo_ref[:, d:])).astype(BF16)
        v_ref[0, rs, :] = _dot(hn, wvo_ref[:, :d]).astype(BF16)
        q_ref[0, rs, :] = (_dot(hn, wq_ref[...]) * (dk ** -0.5)).astype(BF16)
        kt = _dot_nt(wkt_ref[...], hn).astype(BF16)
        gt = _dot_nt(wgt_ref[...], hn)
        for jj in range(sub // LANES):
            j = r * (sub // LANES) + jj
            kt_ref[0, j] = kt[:, jj * LANES:(jj + 1) * LANES]
            rf_ref[0, j * ng:(j + 1) * ng, :] = gt[:, jj * LANES:(jj + 1) * LANES] + bias_ref[...]


def _mlstm_proj(x, norm_w, w_in, layer, wgt, bias_rows, *, tm=1024, nsub=4):
    B, S, D = x.shape
    H = MLSTM_HEADS
    hdk = D // 2
    dk = hdk // H
    tm = min(tm, S)
    ng = wgt.shape[0]
    nj = tm // LANES
    NC = S // LANES
    row = lambda b, i: (b, i, 0)
    return pl.pallas_call(
        functools.partial(_mlstm_proj_kernel, dk=dk, tm=tm, nsub=nsub),
        grid=(B, S // tm),
        in_specs=[pl.BlockSpec((1, tm, D), row), _resident((1, D)),
                  pl.BlockSpec((None,) + w_in.shape[1:], lambda b, i: (layer, 0, 0),
                               pipeline_mode=pl.Buffered(1)),
                  _resident((ng, D)), _resident((ng, LANES))],
        out_specs=[pl.BlockSpec((1, tm, hdk), row),
                   pl.BlockSpec((1, nj, hdk, LANES), lambda b, i: (b, i, 0, 0)),
                   pl.BlockSpec((1, tm, D), row),
                   pl.BlockSpec((1, tm, D), row),
                   pl.BlockSpec((1, nj * ng, LANES), row)],
        out_shape=[jax.ShapeDtypeStruct((B, S, hdk), BF16),
                   jax.ShapeDtypeStruct((B, NC, hdk, LANES), BF16),
                   jax.ShapeDtypeStruct((B, S, D), BF16),
                   jax.ShapeDtypeStruct((B, S, D), BF16),
                   jax.ShapeDtypeStruct((B, NC * ng, LANES), F32)],
        scratch_shapes=[pltpu.VMEM((D, hdk), BF16), pltpu.VMEM((hdk, D), BF16),
                        pltpu.VMEM((D, 2 * D), BF16)],
        compiler_params=_params(),
        name="mlstm_proj",
    )(x, norm_w.reshape(1, D), w_in, wgt, bias_rows)


FFN_CAST_BLOCKS = 8


def _ffn_cast_specs(ffn_w_in, ffn_w_out, layer, step_of, total_steps):
    nblk = min(FFN_CAST_BLOCKS, total_steps)
    rep = total_steps // nblk
    args, in_specs, out_specs, out_shape = [], [], [], []
    for w in (ffn_w_in, ffn_w_out):
        rows, cols = w.shape[1] // nblk, w.shape[2]
        args.append(w)
        in_specs.append(pl.BlockSpec((None, rows, cols),
                                     lambda *g, l=layer: (l, step_of(*g) // rep, 0)))
        out_specs.append(pl.BlockSpec((rows, cols), lambda *g: (step_of(*g) // rep, 0)))
        out_shape.append(jax.ShapeDtypeStruct(w.shape[1:], BF16))
    return rep, args, in_specs, out_specs, out_shape


def _ffn_cast(step, rep, srcs, dsts):
    @pl.when(step % rep == 0)
    def _():
        for src, dst in zip(srcs, dsts):
            dst[...] = src[...].astype(BF16)


def _log_sigmoid(x):
    return jnp.minimum(x, 0.0) - jnp.log1p(jnp.exp(-jnp.abs(x)))


def _mlstm_core_kernel(q_ref, kt_ref, v_ref, rf_ref, wi_ref, wo_ref, o_ref, wi_out, wo_out, st_ref, c_ref,
                       *, nc, dv, unroll, cast_rep):
    L = LANES
    h = pl.program_id(1)
    ng = 4 * MLSTM_HEADS
    _ffn_cast(pl.program_id(0) * pl.num_programs(1) + h, cast_rep, (wi_ref, wo_ref), (wi_out, wo_out))

    def gate_rows(g):
        return rf_ref[0, pl.ds(4 * h + g, nc, stride=ng), :]

    lane = lax.broadcasted_iota(jnp.int32, (nc, L), 1)

    def prefix_sum(x):
        for s in (1, 2, 4, 8, 16, 32, 64):
            x = x + jnp.where(lane >= s, pltpu.roll(x, s, axis=1), 0.0)
        return x

    def suffix_sum(x):
        for s in (1, 2, 4, 8, 16, 32, 64):
            x = x + jnp.where(lane < L - s, pltpu.roll(x, L - s, axis=1), 0.0)
        return x

    for d in range(2):
        log_i = gate_rows(2 * d)
        log_f = _log_sigmoid(gate_rows(2 * d + 1))
        if d == 0:
            b = prefix_sum(log_f)
            b_last = jnp.broadcast_to(b[:, L - 1:L], (nc, L))
        else:
            b = suffix_sum(log_f)
            b_last = jnp.broadcast_to(b[:, 0:1], (nc, L))
        r = log_i - b
        a = b_last + r
        a_max = jnp.broadcast_to(jnp.max(a, axis=1, keepdims=True), (nc, L))
        st_ref[d, 0] = log_f
        st_ref[d, 1] = r
        st_ref[d, 2] = jnp.exp(a - a_max)
        st_ref[d, 3] = b_last
        st_ref[d, 4] = a_max

    def m_scan(i, carry):
        new = []
        for d, c in ((0, i), (1, nc - 1 - i)):
            row = pl.ds(c, 1)
            st_ref[d, 5, row, :] = carry[d]
            new.append(jnp.maximum(st_ref[d, 3, row, :] + carry[d], st_ref[d, 4, row, :]))
        return tuple(new)

    m0 = jnp.zeros((1, L), F32)
    lax.fori_loop(0, nc, m_scan, (m0, m0))
    for d in range(2):
        b_last, a_max, m_prev = st_ref[d, 3], st_ref[d, 4], st_ref[d, 5]
        m_new = jnp.maximum(b_last + m_prev, a_max)
        st_ref[d, 3] = jnp.exp(b_last + m_prev - m_new)
        st_ref[d, 2] = st_ref[d, 2] * jnp.exp(a_max - m_new)
        st_ref[d, 0] = st_ref[d, 0] * LOG2E
        st_ref[d, 1] = st_ref[d, 1] * LOG2E
        st_ref[d, 5] = m_prev * LOG2E

    c_ref[...] = jnp.zeros_like(c_ref)

    t_idx = lax.broadcasted_iota(jnp.int32, (L, L), 0)
    s_idx = lax.broadcasted_iota(jnp.int32, (L, L), 1)
    masks = (s_idx <= t_idx, s_idx >= t_idx)
    ones = jnp.ones((L, L), BF16)

    def tile3(x):
        return jnp.concatenate([x] * (dv // L + 1), axis=1)

    def body(i, carry, assign):
        items = []
        for u in range(unroll):
            c = i * unroll + u
            items += [(0, c), (1, nc - 1 - c)]

        work = []
        for d, c in items:
            rows = pl.ds(pl.multiple_of(c * L, L), L)
            qc = q_ref[0, rows, :]
            kt = kt_ref[0, c]
            vaug = jnp.concatenate([v_ref[0, rows, :], ones], axis=1)
            w = st_ref[d, 2, pl.ds(c, 1), :]
            scores = _dot(qc, kt)
            kv = _dot(kt * w.astype(BF16), vaug)
            work.append((rows, qc, vaug, scores, kv))

        states = [c_ref[0], c_ref[1]]
        prev_states = []
        for (d, c), (rows, qc, vaug, scores, kv) in zip(items, work):
            prev_states.append(states[d].astype(BF16))
            states[d] = tile3(st_ref[d, 3, pl.ds(c, 1), :]) * states[d] + kv
        c_ref[0] = states[0]
        c_ref[1] = states[1]

        for (d, c), (rows, qc, vaug, scores, kv), prev_state in zip(items, work, prev_states):
            log_f = st_ref[d, 0, pl.ds(c, 1), :]
            r = st_ref[d, 1, pl.ds(c, 1), :]
            m_prev = st_ref[d, 5, pl.ds(c, 1), :]
            mask = masks[d]
            r_masked = jnp.where(mask, r, NEG_INF)
            cm = jnp.max(r_masked, axis=1, keepdims=True)
            b_col = jnp.sum(jnp.where(mask, log_f, 0.0), axis=1, keepdims=True)
            mu = jnp.broadcast_to(jnp.maximum(m_prev[:, 0:1], cm), (L, L))
            decay = jnp.exp2(r_masked - mu)
            inter = jnp.exp2(m_prev - mu)
            lhs = jnp.concatenate([(scores * decay).astype(BF16), inter.astype(BF16) * qc], axis=1)
            both = _dot(lhs, jnp.concatenate([vaug, prev_state], axis=0))
            den = jnp.maximum(jnp.abs(both[:, dv:]), jnp.exp2(-(b_col + mu)))
            out = both[:, :dv] * tile3(1.0 / den)[:, :dv]
            if assign:
                o_ref[0, rows, :] = out
            else:
                o_ref[0, rows, :] += out
        return carry

    steps = nc // 2 // unroll
    lax.fori_loop(0, steps, functools.partial(body, assign=True), 0)
    lax.fori_loop(steps, 2 * steps, functools.partial(body, assign=False), 0)


def _mlstm_core(q, kt, v, rf, ffn_w_in, ffn_w_out, layer, *, unroll=4):
    B, S, hdk = q.shape
    H = MLSTM_HEADS
    dk = hdk // H
    dv = v.shape[-1] // H
    nc = S // LANES
    rep, w_args, w_in_specs, w_out_specs, w_out_shape = _ffn_cast_specs(
        ffn_w_in, ffn_w_out, layer, lambda b, h: b * H + h, B * H)
    return pl.pallas_call(
        functools.partial(_mlstm_core_kernel, nc=nc, dv=dv, unroll=min(unroll, nc // 2), cast_rep=rep),
        grid=(B, H),
        in_specs=[pl.BlockSpec((1, S, dk), lambda b, h: (b, 0, h)),
                  pl.BlockSpec((1, nc, dk, LANES), lambda b, h: (b, 0, h, 0)),
                  pl.BlockSpec((1, S, dv), lambda b, h: (b, 0, h)),
                  pl.BlockSpec((1, rf.shape[1], LANES), lambda b, h: (b, 0, 0))] + w_in_specs,
        out_specs=[pl.BlockSpec((1, S, dv), lambda b, h: (b, 0, h))] + w_out_specs,
        out_shape=[jax.ShapeDtypeStruct((B, S, H * dv), F32)] + w_out_shape,
        scratch_shapes=[pltpu.VMEM((2, 6, nc, LANES), F32),
                        pltpu.VMEM((2, dk, dv + LANES), F32)],
        compiler_params=_params(),
        name="mlstm_core",
    )(q, kt, v, rf, *w_args)


def _attn_proj_kernel(x_ref, nw_ref, w_ref, pos_ref, invf_ref, qt_ref, k_ref, vt_ref, wt_ref, *, tm, nsub):
    hd = ATTN_HEAD_DIM

    @pl.when((pl.program_id(0) == 0) & (pl.program_id(1) == 0))
    def _():
        for c in range(0, wt_ref.shape[0], LANES):
            wt_ref[c:c + LANES, :] = w_ref[:, c:c + LANES].T.astype(BF16)

    half = hd // 2
    nq = ATTN_Q_HEADS * hd
    nk = ATTN_KV_HEADS * hd
    scale = LOG2E * hd ** -0.5
    piece = 4 * hd
    sub = tm // nsub
    for r in range(nsub):
        rs = slice(r * sub, (r + 1) * sub)
        hn = _rmsnorm(x_ref[0, rs, :], nw_ref[...]).astype(BF16)
        ang = invf_ref[...] * pos_ref[0, :, rs].astype(F32)
        cos = jnp.cos(ang)
        sin = jnp.sin(ang)
        chunks = [(r * (sub // LANES) + jj, slice(jj * LANES, (jj + 1) * LANES))
                  for jj in range(sub // LANES)]

        def proj(lo, hi):
            return _dot_nt(wt_ref[lo:hi, :], hn)

        def rope(xh):
            x1, x2 = xh[:half], xh[half:]
            return jnp.concatenate([x1 * cos - x2 * sin, x2 * cos + x1 * sin], axis=0)

        vt = proj(nq + nk, nq + 2 * nk).astype(BF16)
        for j, cols in chunks:
            vt_ref[0, j] = vt[:, cols]
        kp = proj(nq, nq + nk)
        kt = jnp.concatenate([rope(kp[g * hd:(g + 1) * hd]) for g in range(ATTN_KV_HEADS)], axis=0)
        for j, cols in chunks:
            for f in range(nk // LANES):
                k_ref[0, j * LANES:(j + 1) * LANES, f * LANES:(f + 1) * LANES] = (
                    kt[f * LANES:(f + 1) * LANES, cols].T.astype(BF16))
        for lo in range(0, nq, piece):
            qp = proj(lo, lo + piece)
            qt = jnp.concatenate([rope(qp[h * hd:(h + 1) * hd]) * scale for h in range(piece // hd)],
                                 axis=0).astype(BF16)
            for j, cols in chunks:
                qt_ref[0, j, lo:lo + piece, :] = qt[:, cols]


def _attn_proj(x, norm_w, w_in, layer, pos_rows, inv_freq, *, tm=1024, nsub=4):
    B, S, D = x.shape
    tm = min(tm, S)
    nq = ATTN_Q_HEADS * ATTN_HEAD_DIM
    nk = ATTN_KV_HEADS * ATTN_HEAD_DIM
    nj = tm // LANES
    nb = S // LANES
    return pl.pallas_call(
        functools.partial(_attn_proj_kernel, tm=tm, nsub=nsub),
        grid=(B, S // tm),
        in_specs=[pl.BlockSpec((1, tm, D), lambda b, i: (b, i, 0)), _resident((1, D)),
                  pl.BlockSpec((None,) + w_in.shape[1:], lambda b, i: (layer, 0, 0),
                               pipeline_mode=pl.Buffered(1)),
                  pl.BlockSpec((1, 1, tm), lambda b, i: (b, 0, i)),
                  _resident((ATTN_HEAD_DIM // 2, 1))],
        out_specs=[pl.BlockSpec((1, nj, nq, LANES), lambda b, i: (b, i, 0, 0)),
                   pl.BlockSpec((1, tm, nk), lambda b, i: (b, i, 0)),
                   pl.BlockSpec((1, nj, nk, LANES), lambda b, i: (b, i, 0, 0))],
        out_shape=[jax.ShapeDtypeStruct((B, nb, nq, LANES), BF16),
                   jax.ShapeDtypeStruct((B, S, nk), BF16),
                   jax.ShapeDtypeStruct((B, nb, nk, LANES), BF16)],
        scratch_shapes=[pltpu.VMEM((nq + 2 * nk, D), BF16)],
        compiler_params=_params(),
        name="attn_proj",
    )(x, norm_w.reshape(1, D), w_in, pos_rows, inv_freq)


def _attn_core_kernel(qt_ref, kp_ref, kc_ref, kn_ref, vp_ref, vc_ref, vn_ref, sink_ref, wi_ref, wo_ref,
                      o_ref, wi_out, wo_out, k_all, v_all, *, nb, nblk, cast_rep):
    L = LANES
    hd = ATTN_HEAD_DIM
    G = ATTN_GROUP
    c = pl.program_id(1)
    _ffn_cast(pl.program_id(0) * pl.num_programs(1) + c, cast_rep, (wi_ref, wo_ref), (wi_out, wo_out))
    k_all[0:L] = kp_ref[0]
    k_all[L:(nblk + 1) * L] = kc_ref[0]
    k_all[(nblk + 1) * L:] = kn_ref[0]
    v_all[0] = vp_ref[0, 0]
    v_all[1:nblk + 1] = vc_ref[0]
    v_all[nblk + 1] = vn_ref[0, 0]

    key = lax.broadcasted_iota(jnp.int32, (L, G * L), 0)
    qry = lax.broadcasted_iota(jnp.int32, (L, G * L), 1) % L
    prev_mask = jnp.where(key >= qry, 0.0, NEG_INF)
    next_mask = jnp.where(key <= qry, 0.0, NEG_INF)
    zeros = jnp.zeros((hd, G * L), BF16)
    ones = jnp.ones((16, 3 * L), BF16)

    def scores(j, g):
        qg = jnp.concatenate([qt_ref[0, j, (G * g + i) * hd:(G * g + i + 1) * hd, :] for i in range(G)],
                             axis=1)
        qz = jnp.concatenate([qg, zeros] if g % 2 == 0 else [zeros, qg], axis=0)
        return _dot(k_all[j * L:(j + 3) * L, (g // 2) * L:(g // 2 + 1) * L], qz)

    items = [(j, g) for j in range(nblk) for g in range(ATTN_KV_HEADS)]
    ahead = 2
    pending = [scores(*it) for it in items[:ahead]]
    for n, (j, g) in enumerate(items):
        s = pending.pop(0)
        if n + ahead < len(items):
            pending.append(scores(*items[n + ahead]))
        if g == 0:
            blk = c * nblk + j
            prev_bias = prev_mask + jnp.where(blk > 0, 0.0, NEG_INF)
            next_bias = next_mask + jnp.where(blk < nb - 1, 0.0, NEG_INF)
        s = [s[0:L] + prev_bias, s[L:2 * L], s[2 * L:] + next_bias]
        sink = sink_ref[g:g + 1, :]
        m = sink
        for sj in s:
            m = jnp.maximum(m, jnp.max(sj, axis=0, keepdims=True))
        p = jnp.concatenate([jnp.exp2(sj - m).astype(BF16) for sj in s], axis=0)
        vg = jnp.concatenate([v_all[j + i, g * hd:(g + 1) * hd, :] for i in range(3)], axis=1)
        acc = _dot(jnp.concatenate([vg, ones], axis=0), p)
        denom = acc[hd:hd + 1] + jnp.exp2(sink - m)
        out = acc[:hd] * (1.0 / denom)
        for pair in range(G // 2):
            tile = jnp.concatenate([out[:, (2 * pair) * L:(2 * pair + 1) * L],
                                    out[:, (2 * pair + 1) * L:(2 * pair + 2) * L]], axis=0)
            col = (G * g + 2 * pair) * hd
            o_ref[0, j * L:(j + 1) * L, col:col + L] = tile.T.astype(BF16)


def _attn_core(qt, k, vt, sink_rows, ffn_w_in, ffn_w_out, layer, *, nblk=4):
    B, nb, nq, L = qt.shape
    nk = k.shape[-1]
    S = nb * L
    nblk = min(nblk, nb)
    steps = nb // nblk
    prev = lambda b, c: (b, jnp.maximum(c * nblk - 1, 0), 0)
    cur = lambda b, c: (b, c, 0)
    nxt = lambda b, c: (b, jnp.minimum((c + 1) * nblk, nb - 1), 0)
    four = lambda f: (lambda b, c: f(b, c) + (0,))
    rep, w_args, w_in_specs, w_out_specs, w_out_shape = _ffn_cast_specs(
        ffn_w_in, ffn_w_out, layer, lambda b, c: b * steps + c, B * steps)
    return pl.pallas_call(
        functools.partial(_attn_core_kernel, nb=nb, nblk=nblk, cast_rep=rep),
        grid=(B, steps),
        in_specs=[pl.BlockSpec((1, nblk, nq, L), four(cur)),
                  pl.BlockSpec((1, L, nk), prev), pl.BlockSpec((1, nblk * L, nk), cur),
                  pl.BlockSpec((1, L, nk), nxt),
                  pl.BlockSpec((1, 1, nk, L), four(prev)), pl.BlockSpec((1, nblk, nk, L), four(cur)),
                  pl.BlockSpec((1, 1, nk, L), four(nxt)),
                  _resident(sink_rows.shape)] + w_in_specs,
        out_specs=[pl.BlockSpec((1, nblk * L, nq), cur)] + w_out_specs,
        out_shape=[jax.ShapeDtypeStruct((B, S, nq), BF16)] + w_out_shape,
        scratch_shapes=[pltpu.VMEM(((nblk + 2) * L, nk), BF16),
                        pltpu.VMEM((nblk + 2, nk, L), BF16)],
        compiler_params=_params(),
        name="attn_core",
    )(qt, k, k, k, vt, vt, vt, sink_rows, *w_args)


def _mlstm_mixer(x, norm_w, w_in, layer, b_gate, ffn_w_in, ffn_w_out, ffn_layer):
    D = x.shape[-1]
    H = MLSTM_HEADS
    wgt = w_in[layer, :, 3 * D:].T.reshape(4, H, D).transpose(1, 0, 2).reshape(4 * H, D).astype(BF16)
    bias = b_gate.astype(F32).reshape(4, H).T.reshape(4 * H, 1)
    bias_rows = jnp.broadcast_to(bias, (4 * H, LANES))
    q, kt, v, og, rf = _mlstm_proj(x, norm_w, w_in, layer, wgt, bias_rows)
    hs, wi, wo = _mlstm_core(q, kt, v, rf, ffn_w_in, ffn_w_out, ffn_layer)
    return (hs, og), wi, wo


def _attn_mixer(x, pos_rows, inv_freq, norm_w, w_in, layer, sink, ffn_w_in, ffn_w_out, ffn_layer):
    G, L = ATTN_GROUP, LANES
    qt, k, vt = _attn_proj(x, norm_w, w_in, layer, pos_rows, inv_freq)
    sink_rows = jnp.repeat(LOG2E * sink.astype(F32).reshape(ATTN_KV_HEADS, G), L, axis=1)
    a, wi, wo = _attn_core(qt, k, vt, sink_rows, ffn_w_in, ffn_w_out, ffn_layer)
    return (a,), wi, wo


def kernel(x, positions, norm_mix_w, norm_ffn_w, norm_final_w, mlstm_w_in, mlstm_b_gate, mlstm_norm_w, mlstm_w_out, attn_w_in, attn_sink, attn_w_out, ffn_w_in, ffn_w_out):
    depth = norm_mix_w.shape[0]
    B, S = positions.shape
    half = ATTN_HEAD_DIM // 2
    inv_freq = (ROPE_THETA ** (-jnp.arange(half, dtype=F32) / half)).reshape(half, 1)
    pos_rows = positions.reshape(B, 1, S)
    mlstm_wo, attn_wo = mlstm_w_out.astype(BF16), attn_w_out.astype(BF16)
    for i in range(depth):
        j = i // 2
        final_w = norm_final_w if i == depth - 1 else None
        if i % 2 == 0:
            (hs, og), wi, wo = _mlstm_mixer(x, norm_mix_w[i], mlstm_w_in, j, mlstm_b_gate[j],
                                            ffn_w_in, ffn_w_out, i)
            x = _layer_tail("mlstm", (hs, og, mlstm_norm_w[j]), x, mlstm_wo, j, norm_ffn_w[i], wi, wo,
                            final_w)
        else:
            mixed, wi, wo = _attn_mixer(x, pos_rows, inv_freq, norm_mix_w[i], attn_w_in, j, attn_sink[j],
                                        ffn_w_in, ffn_w_out, i)
            x = _layer_tail("attn", mixed, x, attn_wo, j, norm_ffn_w[i], wi, wo, final_w)
    return x
```

```python
import functools

import jax
import jax.numpy as jnp
from jax import lax
from jax.experimental import pallas as pl
from jax.experimental.pallas import tpu as pltpu

F32 = jnp.float32
BF16 = jnp.bfloat16

EPS = 1e-6
LANES = 128

MLSTM_HEADS = 4
MLSTM_CHUNK = 128
ATTN_HEAD_DIM = 64
ATTN_Q_HEADS = 16
ATTN_KV_HEADS = 4
ATTN_GROUP = ATTN_Q_HEADS // ATTN_KV_HEADS
ATTN_BLOCK = 128
ROPE_THETA = 10000.0

NEG_INF = float("-inf")
LOG2E = 1.4426950408889634
VMEM_LIMIT = 56 * 1024 * 1024


def _params():
    return pltpu.CompilerParams(dimension_semantics=("arbitrary", "arbitrary"),
                                vmem_limit_bytes=VMEM_LIMIT)


def _resident(shape):
    return pl.BlockSpec(shape, lambda *_: (0,) * len(shape), pipeline_mode=pl.Buffered(1))


def _rmsnorm(x, w):
    ms = jnp.mean(x * x, axis=-1, keepdims=True)
    return x * lax.rsqrt(ms + EPS) * w


def _sigmoid(x):
    return 1.0 / (1.0 + jnp.exp(-x))


def _dot(a, b):
    return jnp.dot(a, b, preferred_element_type=F32)


def _dot_nt(a, b):
    return lax.dot_general(a, b, (((1,), (1,)), ((), ())), preferred_element_type=F32)


def _layer_tail_kernel(*refs, d_ff, tf, nsub, mixer, final):
    it = iter(refs)
    if mixer == "mlstm":
        hs_ref, og_ref, hw_ref = next(it), next(it), next(it)
    else:
        a_ref = next(it)
    x_ref, wo_ref, nw_ref, win_ref, wout_ref = (next(it) for _ in range(5))
    fw_ref = next(it) if final else None
    o_ref, x1_ref, hn_ref, act_ref = (next(it) for _ in range(4))

    tm = x_ref.shape[1]
    subs = [slice(r * (tm // nsub), (r + 1) * (tm // nsub)) for r in range(nsub)]
    for rs in subs:
        if mixer == "mlstm":
            dv = hs_ref.shape[-1] // MLSTM_HEADS
            parts = []
            for h in range(MLSTM_HEADS):
                hs = hs_ref[0, rs, h * dv:(h + 1) * dv]
                ms = jnp.mean(hs * hs, axis=-1, keepdims=True)
                parts.append(hs * lax.rsqrt(ms + EPS))
            y = (jnp.concatenate(parts, axis=-1) * hw_ref[...] * og_ref[0, rs, :].astype(F32)).astype(BF16)
        else:
            y = a_ref[0, rs, :]
        x1_ref[rs, :] = x_ref[0, rs, :] + _dot(y, wo_ref[...])
    for rs in subs:
        hn_ref[rs, :] = _rmsnorm(x1_ref[rs, :], nw_ref[...]).astype(BF16)
    for rs in subs:
        for j in range(d_ff // tf):
            hn = hn_ref[rs, :]
            g = _dot(hn, win_ref[:, j * tf:(j + 1) * tf])
            u = _dot(hn, win_ref[:, d_ff + j * tf:d_ff + (j + 1) * tf])
            act_ref[rs, j * tf:(j + 1) * tf] = (g * _sigmoid(g) * u).astype(BF16)
    for rs in subs:
        out = x1_ref[rs, :] + _dot(act_ref[rs, :], wout_ref[...])
        if final:
            out = _rmsnorm(out, fw_ref[...])
        o_ref[0, rs, :] = out


def _layer_tail(mixer, mixer_args, x, w_o, layer_o, norm_w, w_in, w_out, final_w=None,
                *, tm=512, tf=256, nsub=2):
    B, S, D = x.shape
    tm = min(tm, S)
    d_ff = w_out.shape[0]
    final = final_w is not None
    row = pl.BlockSpec((1, tm, D), lambda b, i: (b, i, 0))

    def stacked(shape, l):
        return pl.BlockSpec((None,) + shape, lambda b, i: (l,) + (0,) * len(shape),
                            pipeline_mode=pl.Buffered(1))

    if mixer == "mlstm":
        hs, og, head_w = mixer_args
        args = [hs, og, head_w.reshape(1, D)]
        in_specs = [row, row, _resident((1, D))]
    else:
        args = list(mixer_args)
        in_specs = [row]
    args += [x, w_o, norm_w.reshape(1, D), w_in, w_out]
    in_specs += [row, stacked((w_o.shape[1], D), layer_o), _resident((1, D)),
                 _resident((D, 2 * d_ff)), _resident((d_ff, D))]
    if final:
        args.append(final_w.reshape(1, D))
        in_specs.append(_resident((1, D)))
    return pl.pallas_call(
        functools.partial(_layer_tail_kernel, d_ff=d_ff, tf=tf, nsub=nsub, mixer=mixer, final=final),
        grid=(B, S // tm),
        in_specs=in_specs,
        out_specs=row,
        out_shape=jax.ShapeDtypeStruct((B, S, D), F32),
        scratch_shapes=[pltpu.VMEM((tm, D), F32), pltpu.VMEM((tm, D), BF16),
                        pltpu.VMEM((tm, d_ff), BF16)],
        compiler_params=_params(),
        name=mixer + "_tail",
    )(*args)


def _mlstm_proj_kernel(x_ref, nw_ref, w_ref, wgt_ref, bias_ref,
                       q_ref, kt_ref, v_ref, og_ref, rf_ref, wkt_ref, *, dk, tm, nsub):
    d = v_ref.shape[-1]
    hdk = q_ref.shape[-1]
    ng = wgt_ref.shape[0]
    sub = tm // nsub

    @pl.when((pl.program_id(0) == 0) & (pl.program_id(1) == 0))
    def _():
        for c in range(0, hdk, LANES):
            wkt_ref[c:c + LANES, :] = w_ref[:, hdk + c:hdk + c + LANES].astype(F32).T.astype(BF16)

    wgt = wgt_ref[...].astype(BF16)
    for r in range(nsub):
        rs = slice(r * sub, (r + 1) * sub)
        hn = _rmsnorm(x_ref[0, rs, :], nw_ref[...]).astype(BF16)
        og_ref[0, rs, :] = _sigmoid(_dot(hn, w_ref[:, 2 * hdk + d:2 * hdk + 2 * d])).astype(BF16)
        v_ref[0, rs, :] = _dot(hn, w_ref[:, 2 * hdk:2 * hdk + d]).astype(BF16)
        q_ref[0, rs, :] = (_dot(hn, w_ref[:, :hdk]) * (dk ** -0.5)).astype(BF16)
        kt = _dot_nt(wkt_ref[...], hn).astype(BF16)
        gt = _dot_nt(wgt, hn)
        for jj in range(sub // LANES):
            j = r * (sub // LANES) + jj
            kt_ref[0, j] = kt[:, jj * LANES:(jj + 1) * LANES]
            rf_ref[0, j * ng:(j + 1) * ng, :] = gt[:, jj * LANES:(jj + 1) * LANES] + bias_ref[...]


def _mlstm_proj(x, norm_w, w_in, layer, wgt, bias_rows, *, tm=1024, nsub=4):
    B, S, D = x.shape
    H = MLSTM_HEADS
    hdk = D // 2
    dk = hdk // H
    tm = min(tm, S)
    ng = wgt.shape[0]
    nj = tm // LANES
    NC = S // LANES
    row = lambda b, i: (b, i, 0)
    return pl.pallas_call(
        functools.partial(_mlstm_proj_kernel, dk=dk, tm=tm, nsub=nsub),
        grid=(B, S // tm),
        in_specs=[pl.BlockSpec((1, tm, D), row), _resident((1, D)),
                  pl.BlockSpec((None,) + w_in.shape[1:], lambda b, i: (layer, 0, 0),
                               pipeline_mode=pl.Buffered(1)),
                  _resident((ng, D)), _resident((ng, LANES))],
        out_specs=[pl.BlockSpec((1, tm, hdk), row),
                   pl.BlockSpec((1, nj, hdk, LANES), lambda b, i: (b, i, 0, 0)),
                   pl.BlockSpec((1, tm, D), row),
                   pl.BlockSpec((1, tm, D), row),
                   pl.BlockSpec((1, nj * ng, LANES), row)],
        out_shape=[jax.ShapeDtypeStruct((B, S, hdk), BF16),
                   jax.ShapeDtypeStruct((B, NC, hdk, LANES), BF16),
                   jax.ShapeDtypeStruct((B, S, D), BF16),
                   jax.ShapeDtypeStruct((B, S, D), BF16),
                   jax.ShapeDtypeStruct((B, NC * ng, LANES), F32)],
        scratch_shapes=[pltpu.VMEM((hdk, D), BF16)],
        compiler_params=_params(),
        name="mlstm_proj",
    )(x, norm_w.reshape(1, D), w_in, wgt, bias_rows)


FFN_CAST_BLOCKS = 8


def _ffn_cast_specs(ffn_w_in, ffn_w_out, layer, step_of, total_steps):
    nblk = min(FFN_CAST_BLOCKS, total_steps)
    rep = total_steps // nblk
    args, in_specs, out_specs, out_shape = [], [], [], []
    for w in (ffn_w_in, ffn_w_out):
        rows, cols = w.shape[1] // nblk, w.shape[2]
        args.append(w)
        in_specs.append(pl.BlockSpec((None, rows, cols),
                                     lambda *g, l=layer: (l, step_of(*g) // rep, 0)))
        out_specs.append(pl.BlockSpec((rows, cols), lambda *g: (step_of(*g) // rep, 0)))
        out_shape.append(jax.ShapeDtypeStruct(w.shape[1:], BF16))
    return rep, args, in_specs, out_specs, out_shape


def _ffn_cast(step, rep, srcs, dsts):
    @pl.when(step % rep == 0)
    def _():
        for src, dst in zip(srcs, dsts):
            dst[...] = src[...].astype(BF16)


def _log_sigmoid(x):
    return jnp.minimum(x, 0.0) - jnp.log1p(jnp.exp(-jnp.abs(x)))


def _mlstm_core_kernel(q_ref, kt_ref, v_ref, rf_ref, wi_ref, wo_ref, o_ref, wi_out, wo_out, st_ref, c_ref,
                       *, nc, dv, unroll, cast_rep):
    L = LANES
    h = pl.program_id(1)
    ng = 4 * MLSTM_HEADS
    _ffn_cast(pl.program_id(0) * pl.num_programs(1) + h, cast_rep, (wi_ref, wo_ref), (wi_out, wo_out))

    def gate_rows(g):
        return rf_ref[0, pl.ds(4 * h + g, nc, stride=ng), :]

    lane = lax.broadcasted_iota(jnp.int32, (nc, L), 1)

    def prefix_sum(x):
        for s in (1, 2, 4, 8, 16, 32, 64):
            x = x + jnp.where(lane >= s, pltpu.roll(x, s, axis=1), 0.0)
        return x

    def suffix_sum(x):
        for s in (1, 2, 4, 8, 16, 32, 64):
            x = x + jnp.where(lane < L - s, pltpu.roll(x, L - s, axis=1), 0.0)
        return x

    for d in range(2):
        log_i = gate_rows(2 * d)
        log_f = _log_sigmoid(gate_rows(2 * d + 1))
        if d == 0:
            b = prefix_sum(log_f)
            b_last = jnp.broadcast_to(b[:, L - 1:L], (nc, L))
        else:
            b = suffix_sum(log_f)
            b_last = jnp.broadcast_to(b[:, 0:1], (nc, L))
        r = log_i - b
        a = b_last + r
        a_max = jnp.broadcast_to(jnp.max(a, axis=1, keepdims=True), (nc, L))
        st_ref[d, 0] = log_f
        st_ref[d, 1] = r
        st_ref[d, 2] = jnp.exp(a - a_max)
        st_ref[d, 3] = b_last
        st_ref[d, 4] = a_max

    def m_scan(i, carry):
        new = []
        for d, c in ((0, i), (1, nc - 1 - i)):
            row = pl.ds(c, 1)
            st_ref[d, 5, row, :] = carry[d]
            new.append(jnp.maximum(st_ref[d, 3, row, :] + carry[d], st_ref[d, 4, row, :]))
        return tuple(new)

    m0 = jnp.zeros((1, L), F32)
    lax.fori_loop(0, nc, m_scan, (m0, m0))
    for d in range(2):
        b_last, a_max, m_prev = st_ref[d, 3], st_ref[d, 4], st_ref[d, 5]
        m_new = jnp.maximum(b_last + m_prev, a_max)
        st_ref[d, 3] = jnp.exp(b_last + m_prev - m_new)
        st_ref[d, 2] = st_ref[d, 2] * jnp.exp(a_max - m_new)
        st_ref[d, 0] = st_ref[d, 0] * LOG2E
        st_ref[d, 1] = st_ref[d, 1] * LOG2E
        st_ref[d, 5] = m_prev * LOG2E

    c_ref[...] = jnp.zeros_like(c_ref)

    t_idx = lax.broadcasted_iota(jnp.int32, (L, L), 0)
    s_idx = lax.broadcasted_iota(jnp.int32, (L, L), 1)
    masks = (s_idx <= t_idx, s_idx >= t_idx)
    ones = jnp.ones((L, L), BF16)

    def tile3(x):
        return jnp.concatenate([x] * (dv // L + 1), axis=1)

    def body(i, carry, assign):
        items = []
        for u in range(unroll):
            c = i * unroll + u
            items += [(0, c), (1, nc - 1 - c)]

        work = []
        for d, c in items:
            rows = pl.ds(pl.multiple_of(c * L, L), L)
            qc = q_ref[0, rows, :]
            kt = kt_ref[0, c]
            vaug = jnp.concatenate([v_ref[0, rows, :], ones], axis=1)
            w = st_ref[d, 2, pl.ds(c, 1), :]
            scores = _dot(qc, kt)
            kv = _dot(kt * w.astype(BF16), vaug)
            work.append((rows, qc, vaug, scores, kv))

        states = [c_ref[0], c_ref[1]]
        prev_states = []
        for (d, c), (rows, qc, vaug, scores, kv) in zip(items, work):
            prev_states.append(states[d].astype(BF16))
            states[d] = tile3(st_ref[d, 3, pl.ds(c, 1), :]) * states[d] + kv
        c_ref[0] = states[0]
        c_ref[1] = states[1]

        for (d, c), (rows, qc, vaug, scores, kv), prev_state in zip(items, work, prev_states):
            log_f = st_ref[d, 0, pl.ds(c, 1), :]
            r = st_ref[d, 1, pl.ds(c, 1), :]
            m_prev = st_ref[d, 5, pl.ds(c, 1), :]
            mask = masks[d]
            r_masked = jnp.where(mask, r, NEG_INF)
            cm = jnp.max(r_masked, axis=1, keepdims=True)
            b_col = jnp.sum(jnp.where(mask, log_f, 0.0), axis=1, keepdims=True)
            mu = jnp.broadcast_to(jnp.maximum(m_prev[:, 0:1], cm), (L, L))
            decay = jnp.exp2(r_masked - mu)
            inter = jnp.exp2(m_prev - mu)
            lhs = jnp.concatenate([(scores * decay).astype(BF16), inter.astype(BF16) * qc], axis=1)
            both = _dot(lhs, jnp.concatenate([vaug, prev_state], axis=0))
            den = jnp.maximum(jnp.abs(both[:, dv:]), jnp.exp2(-(b_col + mu)))
            out = both[:, :dv] * tile3(1.0 / den)[:, :dv]
            if assign:
                o_ref[0, rows, :] = out
            else:
                o_ref[0, rows, :] += out
        return carry

    steps = nc // 2 // unroll
    lax.fori_loop(0, steps, functools.partial(body, assign=True), 0)
    lax.fori_loop(steps, 2 * steps, functools.partial(body, assign=False), 0)


def _mlstm_core(q, kt, v, rf, ffn_w_in, ffn_w_out, layer, *, unroll=4):
    B, S, hdk = q.shape
    H = MLSTM_HEADS
    dk = hdk // H
    dv = v.shape[-1] // H
    nc = S // LANES
    rep, w_args, w_in_specs, w_out_specs, w_out_shape = _ffn_cast_specs(
        ffn_w_in, ffn_w_out, layer, lambda b, h: b * H + h, B * H)
    return pl.pallas_call(
        functools.partial(_mlstm_core_kernel, nc=nc, dv=dv, unroll=min(unroll, nc // 2), cast_rep=rep),
        grid=(B, H),
        in_specs=[pl.BlockSpec((1, S, dk), lambda b, h: (b, 0, h)),
                  pl.BlockSpec((1, nc, dk, LANES), lambda b, h: (b, 0, h, 0)),
                  pl.BlockSpec((1, S, dv), lambda b, h: (b, 0, h)),
                  pl.BlockSpec((1, rf.shape[1], LANES), lambda b, h: (b, 0, 0))] + w_in_specs,
        out_specs=[pl.BlockSpec((1, S, dv), lambda b, h: (b, 0, h))] + w_out_specs,
        out_shape=[jax.ShapeDtypeStruct((B, S, H * dv), F32)] + w_out_shape,
        scratch_shapes=[pltpu.VMEM((2, 6, nc, LANES), F32),
                        pltpu.VMEM((2, dk, dv + LANES), F32)],
        compiler_params=_params(),
        name="mlstm_core",
    )(q, kt, v, rf, *w_args)


def _attn_proj_kernel(x_ref, nw_ref, w_ref, pos_ref, invf_ref, qt_ref, k_ref, vt_ref, wt_ref, *, tm, nsub):
    hd = ATTN_HEAD_DIM

    @pl.when((pl.program_id(0) == 0) & (pl.program_id(1) == 0))
    def _():
        for c in range(0, wt_ref.shape[0], LANES):
            wt_ref[c:c + LANES, :] = w_ref[:, c:c + LANES].T.astype(BF16)

    half = hd // 2
    nq = ATTN_Q_HEADS * hd
    nk = ATTN_KV_HEADS * hd
    scale = LOG2E * hd ** -0.5
    piece = 4 * hd
    sub = tm // nsub
    for r in range(nsub):
        rs = slice(r * sub, (r + 1) * sub)
        hn = _rmsnorm(x_ref[0, rs, :], nw_ref[...]).astype(BF16)
        ang = invf_ref[...] * pos_ref[0, :, rs].astype(F32)
        cos = jnp.cos(ang)
        sin = jnp.sin(ang)
        chunks = [(r * (sub // LANES) + jj, slice(jj * LANES, (jj + 1) * LANES))
                  for jj in range(sub // LANES)]

        def proj(lo, hi):
            return _dot_nt(wt_ref[lo:hi, :], hn)

        def rope(xh):
            x1, x2 = xh[:half], xh[half:]
            return jnp.concatenate([x1 * cos - x2 * sin, x2 * cos + x1 * sin], axis=0)

        vt = proj(nq + nk, nq + 2 * nk).astype(BF16)
        for j, cols in chunks:
            vt_ref[0, j] = vt[:, cols]
        kp = proj(nq, nq + nk)
        kt = jnp.concatenate([rope(kp[g * hd:(g + 1) * hd]) for g in range(ATTN_KV_HEADS)], axis=0)
        for j, cols in chunks:
            for f in range(nk // LANES):
                k_ref[0, j * LANES:(j + 1) * LANES, f * LANES:(f + 1) * LANES] = (
                    kt[f * LANES:(f + 1) * LANES, cols].T.astype(BF16))
        for lo in range(0, nq, piece):
            qp = proj(lo, lo + piece)
            qt = jnp.concatenate([rope(qp[h * hd:(h + 1) * hd]) * scale for h in range(piece // hd)],
                                 axis=0).astype(BF16)
            for j, cols in chunks:
                qt_ref[0, j, lo:lo + piece, :] = qt[:, cols]


def _attn_proj(x, norm_w, w_in, layer, pos_rows, inv_freq, *, tm=1024, nsub=4):
    B, S, D = x.shape
    tm = min(tm, S)
    nq = ATTN_Q_HEADS * ATTN_HEAD_DIM
    nk = ATTN_KV_HEADS * ATTN_HEAD_DIM
    nj = tm // LANES
    nb = S // LANES
    return pl.pallas_call(
        functools.partial(_attn_proj_kernel, tm=tm, nsub=nsub),
        grid=(B, S // tm),
        in_specs=[pl.BlockSpec((1, tm, D), lambda b, i: (b, i, 0)), _resident((1, D)),
                  pl.BlockSpec((None,) + w_in.shape[1:], lambda b, i: (layer, 0, 0),
                               pipeline_mode=pl.Buffered(1)),
                  pl.BlockSpec((1, 1, tm), lambda b, i: (b, 0, i)),
                  _resident((ATTN_HEAD_DIM // 2, 1))],
        out_specs=[pl.BlockSpec((1, nj, nq, LANES), lambda b, i: (b, i, 0, 0)),
                   pl.BlockSpec((1, tm, nk), lambda b, i: (b, i, 0)),
                   pl.BlockSpec((1, nj, nk, LANES), lambda b, i: (b, i, 0, 0))],
        out_shape=[jax.ShapeDtypeStruct((B, nb, nq, LANES), BF16),
                   jax.ShapeDtypeStruct((B, S, nk), BF16),
                   jax.ShapeDtypeStruct((B, nb, nk, LANES), BF16)],
        scratch_shapes=[pltpu.VMEM((nq + 2 * nk, D), BF16)],
        compiler_params=_params(),
        name="attn_proj",
    )(x, norm_w.reshape(1, D), w_in, pos_rows, inv_freq)


def _attn_core_kernel(qt_ref, kp_ref, kc_ref, kn_ref, vp_ref, vc_ref, vn_ref, sink_ref, wi_ref, wo_ref,
                      o_ref, wi_out, wo_out, k_all, v_all, *, nb, nblk, cast_rep):
    L = LANES
    hd = ATTN_HEAD_DIM
    G = ATTN_GROUP
    c = pl.program_id(1)
    _ffn_cast(pl.program_id(0) * pl.num_programs(1) + c, cast_rep, (wi_ref, wo_ref), (wi_out, wo_out))
    k_all[0:L] = kp_ref[0]
    k_all[L:(nblk + 1) * L] = kc_ref[0]
    k_all[(nblk + 1) * L:] = kn_ref[0]
    v_all[0] = vp_ref[0, 0]
    v_all[1:nblk + 1] = vc_ref[0]
    v_all[nblk + 1] = vn_ref[0, 0]

    key = lax.broadcasted_iota(jnp.int32, (L, G * L), 0)
    qry = lax.broadcasted_iota(jnp.int32, (L, G * L), 1) % L
    prev_mask = jnp.where(key >= qry, 0.0, NEG_INF)
    next_mask = jnp.where(key <= qry, 0.0, NEG_INF)
    zeros = jnp.zeros((hd, G * L), BF16)
    ones = jnp.ones((16, 3 * L), BF16)

    def scores(j, g):
        qg = jnp.concatenate([qt_ref[0, j, (G * g + i) * hd:(G * g + i + 1) * hd, :] for i in range(G)],
                             axis=1)
        qz = jnp.concatenate([qg, zeros] if g % 2 == 0 else [zeros, qg], axis=0)
        return _dot(k_all[j * L:(j + 3) * L, (g // 2) * L:(g // 2 + 1) * L], qz)

    items = [(j, g) for j in range(nblk) for g in range(ATTN_KV_HEADS)]
    ahead = 2
    pending = [scores(*it) for it in items[:ahead]]
    for n, (j, g) in enumerate(items):
        s = pending.pop(0)
        if n + ahead < len(items):
            pending.append(scores(*items[n + ahead]))
        if g == 0:
            blk = c * nblk + j
            prev_bias = prev_mask + jnp.where(blk > 0, 0.0, NEG_INF)
            next_bias = next_mask + jnp.where(blk < nb - 1, 0.0, NEG_INF)
        s = [s[0:L] + prev_bias, s[L:2 * L], s[2 * L:] + next_bias]
        sink = sink_ref[g:g + 1, :]
        m = sink
        for sj in s:
            m = jnp.maximum(m, jnp.max(sj, axis=0, keepdims=True))
        p = jnp.concatenate([jnp.exp2(sj - m).astype(BF16) for sj in s], axis=0)
        vg = jnp.concatenate([v_all[j + i, g * hd:(g + 1) * hd, :] for i in range(3)], axis=1)
        acc = _dot(jnp.concatenate([vg, ones], axis=0), p)
        denom = acc[hd:hd + 1] + jnp.exp2(sink - m)
        out = acc[:hd] * (1.0 / denom)
        for pair in range(G // 2):
            tile = jnp.concatenate([out[:, (2 * pair) * L:(2 * pair + 1) * L],
                                    out[:, (2 * pair + 1) * L:(2 * pair + 2) * L]], axis=0)
            col = (G * g + 2 * pair) * hd
            o_ref[0, j * L:(j + 1) * L, col:col + L] = tile.T.astype(BF16)


def _attn_core(qt, k, vt, sink_rows, ffn_w_in, ffn_w_out, layer, *, nblk=8):
    B, nb, nq, L = qt.shape
    nk = k.shape[-1]
    S = nb * L
    nblk = min(nblk, nb)
    steps = nb // nblk
    prev = lambda b, c: (b, jnp.maximum(c * nblk - 1, 0), 0)
    cur = lambda b, c: (b, c, 0)
    nxt = lambda b, c: (b, jnp.minimum((c + 1) * nblk, nb - 1), 0)
    four = lambda f: (lambda b, c: f(b, c) + (0,))
    rep, w_args, w_in_specs, w_out_specs, w_out_shape = _ffn_cast_specs(
        ffn_w_in, ffn_w_out, layer, lambda b, c: b * steps + c, B * steps)
    return pl.pallas_call(
        functools.partial(_attn_core_kernel, nb=nb, nblk=nblk, cast_rep=rep),
        grid=(B, steps),
        in_specs=[pl.BlockSpec((1, nblk, nq, L), four(cur)),
                  pl.BlockSpec((1, L, nk), prev), pl.BlockSpec((1, nblk * L, nk), cur),
                  pl.BlockSpec((1, L, nk), nxt),
                  pl.BlockSpec((1, 1, nk, L), four(prev)), pl.BlockSpec((1, nblk, nk, L), four(cur)),
                  pl.BlockSpec((1, 1, nk, L), four(nxt)),
                  _resident(sink_rows.shape)] + w_in_specs,
        out_specs=[pl.BlockSpec((1, nblk * L, nq), cur)] + w_out_specs,
        out_shape=[jax.ShapeDtypeStruct((B, S, nq), BF16)] + w_out_shape,
        scratch_shapes=[pltpu.VMEM(((nblk + 2) * L, nk), BF16),
                        pltpu.VMEM((nblk + 2, nk, L), BF16)],
        compiler_params=_params(),
        name="attn_core",
    )(qt, k, k, k, vt, vt, vt, sink_rows, *w_args)


def _mlstm_mixer(x, norm_w, w_qkvo, w_gates, layer, b_gate, ffn_w_in, ffn_w_out, ffn_layer):
    D = x.shape[-1]
    H = MLSTM_HEADS
    wgt = w_gates[layer].T.reshape(4, H, D).transpose(1, 0, 2).reshape(4 * H, D)
    bias = b_gate.astype(F32).reshape(4, H).T.reshape(4 * H, 1)
    bias_rows = jnp.broadcast_to(bias, (4 * H, LANES))
    q, kt, v, og, rf = _mlstm_proj(x, norm_w, w_qkvo, layer, wgt, bias_rows)
    hs, wi, wo = _mlstm_core(q, kt, v, rf, ffn_w_in, ffn_w_out, ffn_layer)
    return (hs, og), wi, wo


def _attn_mixer(x, pos_rows, inv_freq, norm_w, w_in, layer, sink, ffn_w_in, ffn_w_out, ffn_layer):
    G, L = ATTN_GROUP, LANES
    qt, k, vt = _attn_proj(x, norm_w, w_in, layer, pos_rows, inv_freq)
    sink_rows = jnp.repeat(LOG2E * sink.astype(F32).reshape(ATTN_KV_HEADS, G), L, axis=1)
    a, wi, wo = _attn_core(qt, k, vt, sink_rows, ffn_w_in, ffn_w_out, ffn_layer)
    return (a,), wi, wo


def kernel(x, positions, norm_mix_w, norm_ffn_w, norm_final_w, mlstm_w_in, mlstm_b_gate, mlstm_norm_w, mlstm_w_out, attn_w_in, attn_sink, attn_w_out, ffn_w_in, ffn_w_out):
    depth = norm_mix_w.shape[0]
    B, S = positions.shape
    half = ATTN_HEAD_DIM // 2
    inv_freq = (ROPE_THETA ** (-jnp.arange(half, dtype=F32) / half)).reshape(half, 1)
    pos_rows = positions.reshape(B, 1, S)
    mlstm_wo, attn_wo = mlstm_w_out.astype(BF16), attn_w_out.astype(BF16)
    D = x.shape[-1]
    mlstm_qkvo = mlstm_w_in[:, :, :3 * D].astype(BF16)
    mlstm_gates = mlstm_w_in[:, :, 3 * D:]
    for i in range(depth):
        j = i // 2
        final_w = norm_final_w if i == depth - 1 else None
        if i % 2 == 0:
            (hs, og), wi, wo = _mlstm_mixer(x, norm_mix_w[i], mlstm_qkvo, mlstm_gates, j, mlstm_b_gate[j],
                                            ffn_w_in, ffn_w_out, i)
            x = _layer_tail("mlstm", (hs, og, mlstm_norm_w[j]), x, mlstm_wo, j, norm_ffn_w[i], wi, wo,
                            final_w)
        else:
            mixed, wi, wo = _attn_mixer(x, pos_rows, inv_freq, norm_mix_w[i], attn_w_in, j, attn_sink[j],
                                        ffn_w_in, ffn_w_out, i)
            x = _layer_tail("attn", mixed, x, attn_wo, j, norm_ffn_w[i], wi, wo, final_w)
    return x
```

```python
import functools

import jax
import jax.numpy as jnp
from jax import lax
from jax.experimental import pallas as pl
from jax.experimental.pallas import tpu as pltpu

F32 = jnp.float32
BF16 = jnp.bfloat16

EPS = 1e-6
LANES = 128

MLSTM_HEADS = 4
MLSTM_CHUNK = 128
ATTN_HEAD_DIM = 64
ATTN_Q_HEADS = 16
ATTN_KV_HEADS = 4
ATTN_GROUP = ATTN_Q_HEADS // ATTN_KV_HEADS
ATTN_BLOCK = 128
ROPE_THETA = 10000.0

NEG_INF = float("-inf")
LOG2E = 1.4426950408889634
VMEM_LIMIT = 56 * 1024 * 1024


def _params():
    return pltpu.CompilerParams(dimension_semantics=("arbitrary", "arbitrary"),
                                vmem_limit_bytes=VMEM_LIMIT)


def _resident(shape):
    return pl.BlockSpec(shape, lambda *_: (0,) * len(shape), pipeline_mode=pl.Buffered(1))


def _rmsnorm(x, w):
    ms = jnp.mean(x * x, axis=-1, keepdims=True)
    return x * lax.rsqrt(ms + EPS) * w


def _sigmoid(x):
    return 1.0 / (1.0 + jnp.exp(-x))


def _dot(a, b):
    return jnp.dot(a, b, preferred_element_type=F32)


def _dot_nt(a, b):
    return lax.dot_general(a, b, (((1,), (1,)), ((), ())), preferred_element_type=F32)


def _layer_tail_kernel(*refs, d_ff, tf, nsub, mixer, final):
    it = iter(refs)
    if mixer == "mlstm":
        hs_ref, og_ref, hw_ref = next(it), next(it), next(it)
    else:
        a_ref = next(it)
    x_ref, wo_ref, nw_ref, win_ref, wout_ref = (next(it) for _ in range(5))
    fw_ref = next(it) if final else None
    o_ref, x1_ref, hn_ref, act_ref = (next(it) for _ in range(4))

    tm = x_ref.shape[1]
    subs = [slice(r * (tm // nsub), (r + 1) * (tm // nsub)) for r in range(nsub)]
    for rs in subs:
        if mixer == "mlstm":
            dv = hs_ref.shape[-1] // MLSTM_HEADS
            parts = []
            for h in range(MLSTM_HEADS):
                hs = hs_ref[0, rs, h * dv:(h + 1) * dv]
                ms = jnp.mean(hs * hs, axis=-1, keepdims=True)
                parts.append(hs * lax.rsqrt(ms + EPS))
            y = (jnp.concatenate(parts, axis=-1) * hw_ref[...] * og_ref[0, rs, :].astype(F32)).astype(BF16)
        else:
            y = a_ref[0, rs, :]
        x1_ref[rs, :] = x_ref[0, rs, :] + _dot(y, wo_ref[...])
    for rs in subs:
        hn_ref[rs, :] = _rmsnorm(x1_ref[rs, :], nw_ref[...]).astype(BF16)
    for rs in subs:
        for j in range(d_ff // tf):
            hn = hn_ref[rs, :]
            g = _dot(hn, win_ref[:, j * tf:(j + 1) * tf])
            u = _dot(hn, win_ref[:, d_ff + j * tf:d_ff + (j + 1) * tf])
            act_ref[rs, j * tf:(j + 1) * tf] = (g * _sigmoid(g) * u).astype(BF16)
    for rs in subs:
        out = x1_ref[rs, :] + _dot(act_ref[rs, :], wout_ref[...])
        if final:
            out = _rmsnorm(out, fw_ref[...])
        o_ref[0, rs, :] = out


def _layer_tail(mixer, mixer_args, x, w_o, layer_o, norm_w, w_in, w_out, final_w=None,
                *, tm=512, tf=256, nsub=2):
    B, S, D = x.shape
    tm = min(tm, S)
    d_ff = w_out.shape[0]
    final = final_w is not None
    row = pl.BlockSpec((1, tm, D), lambda b, i: (b, i, 0))

    def stacked(shape, l):
        return pl.BlockSpec((None,) + shape, lambda b, i: (l,) + (0,) * len(shape),
                            pipeline_mode=pl.Buffered(1))

    if mixer == "mlstm":
        hs, og, head_w = mixer_args
        args = [hs, og, head_w.reshape(1, D)]
        in_specs = [row, row, _resident((1, D))]
    else:
        args = list(mixer_args)
        in_specs = [row]
    args += [x, w_o, norm_w.reshape(1, D), w_in, w_out]
    in_specs += [row, stacked((w_o.shape[1], D), layer_o), _resident((1, D)),
                 _resident((D, 2 * d_ff)), _resident((d_ff, D))]
    if final:
        args.append(final_w.reshape(1, D))
        in_specs.append(_resident((1, D)))
    return pl.pallas_call(
        functools.partial(_layer_tail_kernel, d_ff=d_ff, tf=tf, nsub=nsub, mixer=mixer, final=final),
        grid=(B, S // tm),
        in_specs=in_specs,
        out_specs=row,
        out_shape=jax.ShapeDtypeStruct((B, S, D), F32),
        scratch_shapes=[pltpu.VMEM((tm, D), F32), pltpu.VMEM((tm, D), BF16),
                        pltpu.VMEM((tm, d_ff), BF16)],
        compiler_params=_params(),
        name=mixer + "_tail",
    )(*args)


def _mlstm_proj_kernel(x_ref, nw_ref, wt_ref, bias_ref,
                       q_ref, kt_ref, v_ref, og_ref, rf_ref, w_ref, wkt_ref, *, dk, tm, nsub):
    d = v_ref.shape[-1]
    hdk = q_ref.shape[-1]
    ng = rf_ref.shape[1] // (tm // LANES)
    sub = tm // nsub

    @pl.when((pl.program_id(0) == 0) & (pl.program_id(1) == 0))
    def _():
        for c in range(0, hdk, LANES):
            w_ref[:, c:c + LANES] = wt_ref[c:c + LANES, :].T.astype(BF16)
            wkt_ref[c:c + LANES, :] = wt_ref[hdk + c:hdk + c + LANES, :].astype(BF16)
        for c in range(0, 2 * d, LANES):
            w_ref[:, hdk + c:hdk + c + LANES] = wt_ref[2 * hdk + c:2 * hdk + c + LANES, :].T.astype(BF16)

    wgt = wt_ref[2 * hdk + 2 * d:, :].astype(BF16)
    for r in range(nsub):
        rs = slice(r * sub, (r + 1) * sub)
        hn = _rmsnorm(x_ref[0, rs, :], nw_ref[...]).astype(BF16)
        og_ref[0, rs, :] = _sigmoid(_dot(hn, w_ref[:, hdk + d:])).astype(BF16)
        v_ref[0, rs, :] = _dot(hn, w_ref[:, hdk:hdk + d]).astype(BF16)
        q_ref[0, rs, :] = (_dot(hn, w_ref[:, :hdk]) * (dk ** -0.5)).astype(BF16)
        kt = _dot_nt(wkt_ref[...], hn).astype(BF16)
        gt = _dot_nt(wgt, hn)
        for jj in range(sub // LANES):
            j = r * (sub // LANES) + jj
            kt_ref[0, j] = kt[:, jj * LANES:(jj + 1) * LANES]
            rf_ref[0, j * ng:(j + 1) * ng, :] = gt[:, jj * LANES:(jj + 1) * LANES] + bias_ref[...]


def _mlstm_proj(x, norm_w, w_t, layer, bias_rows, *, tm=1024, nsub=4):
    B, S, D = x.shape
    H = MLSTM_HEADS
    hdk = D // 2
    dk = hdk // H
    tm = min(tm, S)
    ng = bias_rows.shape[0]
    nj = tm // LANES
    NC = S // LANES
    row = lambda b, i: (b, i, 0)
    return pl.pallas_call(
        functools.partial(_mlstm_proj_kernel, dk=dk, tm=tm, nsub=nsub),
        grid=(B, S // tm),
        in_specs=[pl.BlockSpec((1, tm, D), row), _resident((1, D)),
                  pl.BlockSpec((None,) + w_t.shape[1:], lambda b, i: (layer, 0, 0),
                               pipeline_mode=pl.Buffered(1)),
                  _resident((ng, LANES))],
        out_specs=[pl.BlockSpec((1, tm, hdk), row),
                   pl.BlockSpec((1, nj, hdk, LANES), lambda b, i: (b, i, 0, 0)),
                   pl.BlockSpec((1, tm, D), row),
                   pl.BlockSpec((1, tm, D), row),
                   pl.BlockSpec((1, nj * ng, LANES), row)],
        out_shape=[jax.ShapeDtypeStruct((B, S, hdk), BF16),
                   jax.ShapeDtypeStruct((B, NC, hdk, LANES), BF16),
                   jax.ShapeDtypeStruct((B, S, D), BF16),
                   jax.ShapeDtypeStruct((B, S, D), BF16),
                   jax.ShapeDtypeStruct((B, NC * ng, LANES), F32)],
        scratch_shapes=[pltpu.VMEM((D, hdk + 2 * D), BF16), pltpu.VMEM((hdk, D), BF16)],
        compiler_params=_params(),
        name="mlstm_proj",
    )(x, norm_w.reshape(1, D), w_t, bias_rows)


FFN_CAST_BLOCKS = 8


def _ffn_cast_specs(ffn_w_in, ffn_w_out, layer, step_of, total_steps):
    nblk = min(FFN_CAST_BLOCKS, total_steps)
    rep = total_steps // nblk
    args, in_specs, out_specs, out_shape = [], [], [], []
    for w in (ffn_w_in, ffn_w_out):
        rows, cols = w.shape[1] // nblk, w.shape[2]
        args.append(w)
        in_specs.append(pl.BlockSpec((None, rows, cols),
                                     lambda *g, l=layer: (l, step_of(*g) // rep, 0)))
        out_specs.append(pl.BlockSpec((rows, cols), lambda *g: (step_of(*g) // rep, 0)))
        out_shape.append(jax.ShapeDtypeStruct(w.shape[1:], BF16))
    return rep, args, in_specs, out_specs, out_shape


def _ffn_cast(step, rep, srcs, dsts):
    @pl.when(step % rep == 0)
    def _():
        for src, dst in zip(srcs, dsts):
            dst[...] = src[...].astype(BF16)


def _log_sigmoid(x):
    return jnp.minimum(x, 0.0) - jnp.log1p(jnp.exp(-jnp.abs(x)))


def _mlstm_core_kernel(q_ref, kt_ref, v_ref, rf_ref, wi_ref, wo_ref, o_ref, wi_out, wo_out, st_ref, c_ref,
                       *, nc, dv, unroll, cast_rep):
    L = LANES
    h = pl.program_id(1)
    ng = 4 * MLSTM_HEADS
    _ffn_cast(pl.program_id(0) * pl.num_programs(1) + h, cast_rep, (wi_ref, wo_ref), (wi_out, wo_out))

    def gate_rows(g):
        return rf_ref[0, pl.ds(MLSTM_HEADS * g + h, nc, stride=ng), :]

    lane = lax.broadcasted_iota(jnp.int32, (nc, L), 1)

    def prefix_sum(x):
        for s in (1, 2, 4, 8, 16, 32, 64):
            x = x + jnp.where(lane >= s, pltpu.roll(x, s, axis=1), 0.0)
        return x

    def suffix_sum(x):
        for s in (1, 2, 4, 8, 16, 32, 64):
            x = x + jnp.where(lane < L - s, pltpu.roll(x, L - s, axis=1), 0.0)
        return x

    for d in range(2):
        log_i = gate_rows(2 * d)
        log_f = _log_sigmoid(gate_rows(2 * d + 1))
        if d == 0:
            b = prefix_sum(log_f)
            b_last = jnp.broadcast_to(b[:, L - 1:L], (nc, L))
        else:
            b = suffix_sum(log_f)
            b_last = jnp.broadcast_to(b[:, 0:1], (nc, L))
        r = log_i - b
        a = b_last + r
        a_max = jnp.broadcast_to(jnp.max(a, axis=1, keepdims=True), (nc, L))
        st_ref[d, 0] = log_f
        st_ref[d, 1] = r
        st_ref[d, 2] = jnp.exp(a - a_max)
        st_ref[d, 3] = b_last
        st_ref[d, 4] = a_max

    def m_scan(i, carry):
        new = []
        for d, c in ((0, i), (1, nc - 1 - i)):
            row = pl.ds(c, 1)
            st_ref[d, 5, row, :] = carry[d]
            new.append(jnp.maximum(st_ref[d, 3, row, :] + carry[d], st_ref[d, 4, row, :]))
        return tuple(new)

    m0 = jnp.zeros((1, L), F32)
    lax.fori_loop(0, nc, m_scan, (m0, m0))
    for d in range(2):
        b_last, a_max, m_prev = st_ref[d, 3], st_ref[d, 4], st_ref[d, 5]
        m_new = jnp.maximum(b_last + m_prev, a_max)
        st_ref[d, 3] = jnp.exp(b_last + m_prev - m_new)
        st_ref[d, 2] = st_ref[d, 2] * jnp.exp(a_max - m_new)
        st_ref[d, 0] = st_ref[d, 0] * LOG2E
        st_ref[d, 1] = st_ref[d, 1] * LOG2E
        st_ref[d, 5] = m_prev * LOG2E

    c_ref[...] = jnp.zeros_like(c_ref)

    t_idx = lax.broadcasted_iota(jnp.int32, (L, L), 0)
    s_idx = lax.broadcasted_iota(jnp.int32, (L, L), 1)
    masks = (s_idx <= t_idx, s_idx >= t_idx)
    ones = jnp.ones((L, L), BF16)

    def tile3(x):
        return jnp.concatenate([x] * (dv // L + 1), axis=1)

    def body(i, carry, assign):
        items = []
        for u in range(unroll):
            c = i * unroll + u
            items += [(0, c), (1, nc - 1 - c)]

        work = []
        for d, c in items:
            rows = pl.ds(pl.multiple_of(c * L, L), L)
            qc = q_ref[0, rows, :]
            kt = kt_ref[0, c]
            vaug = jnp.concatenate([v_ref[0, rows, :], ones], axis=1)
            w = st_ref[d, 2, pl.ds(c, 1), :]
            scores = _dot(qc, kt)
            kv = _dot(kt * w.astype(BF16), vaug)
            work.append((rows, qc, vaug, scores, kv))

        states = [c_ref[0], c_ref[1]]
        prev_states = []
        for (d, c), (rows, qc, vaug, scores, kv) in zip(items, work):
            prev_states.append(states[d].astype(BF16))
            states[d] = tile3(st_ref[d, 3, pl.ds(c, 1), :]) * states[d] + kv
        c_ref[0] = states[0]
        c_ref[1] = states[1]

        for (d, c), (rows, qc, vaug, scores, kv), prev_state in zip(items, work, prev_states):
            log_f = st_ref[d, 0, pl.ds(c, 1), :]
            r = st_ref[d, 1, pl.ds(c, 1), :]
            m_prev = st_ref[d, 5, pl.ds(c, 1), :]
            mask = masks[d]
            r_masked = jnp.where(mask, r, NEG_INF)
            cm = jnp.max(r_masked, axis=1, keepdims=True)
            b_col = jnp.sum(jnp.where(mask, log_f, 0.0), axis=1, keepdims=True)
            mu = jnp.broadcast_to(jnp.maximum(m_prev[:, 0:1], cm), (L, L))
            decay = jnp.exp2(r_masked - mu)
            inter = jnp.exp2(m_prev - mu)
            lhs = jnp.concatenate([(scores * decay).astype(BF16), inter.astype(BF16) * qc], axis=1)
            both = _dot(lhs, jnp.concatenate([vaug, prev_state], axis=0))
            den = jnp.maximum(jnp.abs(both[:, dv:]), jnp.exp2(-(b_col + mu)))
            out = both[:, :dv] * tile3(1.0 / den)[:, :dv]
            if assign:
                o_ref[0, rows, :] = out
            else:
                o_ref[0, rows, :] += out
        return carry

    steps = nc // 2 // unroll
    lax.fori_loop(0, steps, functools.partial(body, assign=True), 0)
    lax.fori_loop(steps, 2 * steps, functools.partial(body, assign=False), 0)


def _mlstm_core(q, kt, v, rf, ffn_w_in, ffn_w_out, layer, *, unroll=4):
    B, S, hdk = q.shape
    H = MLSTM_HEADS
    dk = hdk // H
    dv = v.shape[-1] // H
    nc = S // LANES
    rep, w_args, w_in_specs, w_out_specs, w_out_shape = _ffn_cast_specs(
        ffn_w_in, ffn_w_out, layer, lambda b, h: b * H + h, B * H)
    return pl.pallas_call(
        functools.partial(_mlstm_core_kernel, nc=nc, dv=dv, unroll=min(unroll, nc // 2), cast_rep=rep),
        grid=(B, H),
        in_specs=[pl.BlockSpec((1, S, dk), lambda b, h: (b, 0, h)),
                  pl.BlockSpec((1, nc, dk, LANES), lambda b, h: (b, 0, h, 0)),
                  pl.BlockSpec((1, S, dv), lambda b, h: (b, 0, h)),
                  pl.BlockSpec((1, rf.shape[1], LANES), lambda b, h: (b, 0, 0))] + w_in_specs,
        out_specs=[pl.BlockSpec((1, S, dv), lambda b, h: (b, 0, h))] + w_out_specs,
        out_shape=[jax.ShapeDtypeStruct((B, S, H * dv), F32)] + w_out_shape,
        scratch_shapes=[pltpu.VMEM((2, 6, nc, LANES), F32),
                        pltpu.VMEM((2, dk, dv + LANES), F32)],
        compiler_params=_params(),
        name="mlstm_core",
    )(q, kt, v, rf, *w_args)


def _attn_proj_kernel(x_ref, nw_ref, w_ref, pos_ref, invf_ref, qt_ref, k_ref, vt_ref, wt_ref, *, tm, nsub):
    hd = ATTN_HEAD_DIM

    @pl.when((pl.program_id(0) == 0) & (pl.program_id(1) == 0))
    def _():
        for c in range(0, wt_ref.shape[0], LANES):
            wt_ref[c:c + LANES, :] = w_ref[:, c:c + LANES].T.astype(BF16)

    half = hd // 2
    nq = ATTN_Q_HEADS * hd
    nk = ATTN_KV_HEADS * hd
    scale = LOG2E * hd ** -0.5
    piece = 4 * hd
    sub = tm // nsub
    for r in range(nsub):
        rs = slice(r * sub, (r + 1) * sub)
        hn = _rmsnorm(x_ref[0, rs, :], nw_ref[...]).astype(BF16)
        ang = invf_ref[...] * pos_ref[0, :, rs].astype(F32)
        cos = jnp.cos(ang)
        sin = jnp.sin(ang)
        chunks = [(r * (sub // LANES) + jj, slice(jj * LANES, (jj + 1) * LANES))
                  for jj in range(sub // LANES)]

        def proj(lo, hi):
            return _dot_nt(wt_ref[lo:hi, :], hn)

        def rope(xh):
            x1, x2 = xh[:half], xh[half:]
            return jnp.concatenate([x1 * cos - x2 * sin, x2 * cos + x1 * sin], axis=0)

        vt = proj(nq + nk, nq + 2 * nk).astype(BF16)
        for j, cols in chunks:
            vt_ref[0, j] = vt[:, cols]
        kp = proj(nq, nq + nk)
        kt = jnp.concatenate([rope(kp[g * hd:(g + 1) * hd]) for g in range(ATTN_KV_HEADS)], axis=0)
        for j, cols in chunks:
            for f in range(nk // LANES):
                k_ref[0, j * LANES:(j + 1) * LANES, f * LANES:(f + 1) * LANES] = (
                    kt[f * LANES:(f + 1) * LANES, cols].T.astype(BF16))
        for lo in range(0, nq, piece):
            qp = proj(lo, lo + piece)
            qt = jnp.concatenate([rope(qp[h * hd:(h + 1) * hd]) * scale for h in range(piece // hd)],
                                 axis=0).astype(BF16)
            for j, cols in chunks:
                qt_ref[0, j, lo:lo + piece, :] = qt[:, cols]


def _attn_proj(x, norm_w, w_in, layer, pos_rows, inv_freq, *, tm=1024, nsub=4):
    B, S, D = x.shape
    tm = min(tm, S)
    nq = ATTN_Q_HEADS * ATTN_HEAD_DIM
    nk = ATTN_KV_HEADS * ATTN_HEAD_DIM
    nj = tm // LANES
    nb = S // LANES
    return pl.pallas_call(
        functools.partial(_attn_proj_kernel, tm=tm, nsub=nsub),
        grid=(B, S // tm),
        in_specs=[pl.BlockSpec((1, tm, D), lambda b, i: (b, i, 0)), _resident((1, D)),
                  pl.BlockSpec((None,) + w_in.shape[1:], lambda b, i: (layer, 0, 0),
                               pipeline_mode=pl.Buffered(1)),
                  pl.BlockSpec((1, 1, tm), lambda b, i: (b, 0, i)),
                  _resident((ATTN_HEAD_DIM // 2, 1))],
        out_specs=[pl.BlockSpec((1, nj, nq, LANES), lambda b, i: (b, i, 0, 0)),
                   pl.BlockSpec((1, tm, nk), lambda b, i: (b, i, 0)),
                   pl.BlockSpec((1, nj, nk, LANES), lambda b, i: (b, i, 0, 0))],
        out_shape=[jax.ShapeDtypeStruct((B, nb, nq, LANES), BF16),
                   jax.ShapeDtypeStruct((B, S, nk), BF16),
                   jax.ShapeDtypeStruct((B, nb, nk, LANES), BF16)],
        scratch_shapes=[pltpu.VMEM((nq + 2 * nk, D), BF16)],
        compiler_params=_params(),
        name="attn_proj",
    )(x, norm_w.reshape(1, D), w_in, pos_rows, inv_freq)


def _attn_core_kernel(qt_ref, kp_ref, kc_ref, kn_ref, vp_ref, vc_ref, vn_ref, sink_ref, wi_ref, wo_ref,
                      o_ref, wi_out, wo_out, k_all, v_all, *, nb, nblk, cast_rep):
    L = LANES
    hd = ATTN_HEAD_DIM
    G = ATTN_GROUP
    c = pl.program_id(1)
    _ffn_cast(pl.program_id(0) * pl.num_programs(1) + c, cast_rep, (wi_ref, wo_ref), (wi_out, wo_out))
    k_all[0:L] = kp_ref[0]
    k_all[L:(nblk + 1) * L] = kc_ref[0]
    k_all[(nblk + 1) * L:] = kn_ref[0]
    v_all[0] = vp_ref[0, 0]
    v_all[1:nblk + 1] = vc_ref[0]
    v_all[nblk + 1] = vn_ref[0, 0]

    key = lax.broadcasted_iota(jnp.int32, (L, G * L), 0)
    qry = lax.broadcasted_iota(jnp.int32, (L, G * L), 1) % L
    prev_mask = jnp.where(key >= qry, 0.0, NEG_INF)
    next_mask = jnp.where(key <= qry, 0.0, NEG_INF)
    zeros = jnp.zeros((hd, G * L), BF16)
    ones = jnp.ones((16, 3 * L), BF16)

    def scores(j, g):
        qg = jnp.concatenate([qt_ref[0, j, (G * g + i) * hd:(G * g + i + 1) * hd, :] for i in range(G)],
                             axis=1)
        qz = jnp.concatenate([qg, zeros] if g % 2 == 0 else [zeros, qg], axis=0)
        return _dot(k_all[j * L:(j + 3) * L, (g // 2) * L:(g // 2 + 1) * L], qz)

    items = [(j, g) for j in range(nblk) for g in range(ATTN_KV_HEADS)]
    ahead = 2
    pending = [scores(*it) for it in items[:ahead]]
    for n, (j, g) in enumerate(items):
        s = pending.pop(0)
        if n + ahead < len(items):
            pending.append(scores(*items[n + ahead]))
        if g == 0:
            blk = c * nblk + j
            prev_bias = prev_mask + jnp.where(blk > 0, 0.0, NEG_INF)
            next_bias = next_mask + jnp.where(blk < nb - 1, 0.0, NEG_INF)
        s = [s[0:L] + prev_bias, s[L:2 * L], s[2 * L:] + next_bias]
        sink = sink_ref[g:g + 1, :]
        m = sink
        for sj in s:
            m = jnp.maximum(m, jnp.max(sj, axis=0, keepdims=True))
        p = jnp.concatenate([jnp.exp2(sj - m).astype(BF16) for sj in s], axis=0)
        vg = jnp.concatenate([v_all[j + i, g * hd:(g + 1) * hd, :] for i in range(3)], axis=1)
        acc = _dot(jnp.concatenate([vg, ones], axis=0), p)
        denom = acc[hd:hd + 1] + jnp.exp2(sink - m)
        out = acc[:hd] * (1.0 / denom)
        for pair in range(G // 2):
            tile = jnp.concatenate([out[:, (2 * pair) * L:(2 * pair + 1) * L],
                                    out[:, (2 * pair + 1) * L:(2 * pair + 2) * L]], axis=0)
            col = (G * g + 2 * pair) * hd
            o_ref[0, j * L:(j + 1) * L, col:col + L] = tile.T.astype(BF16)


def _attn_core(qt, k, vt, sink_rows, ffn_w_in, ffn_w_out, layer, *, nblk=8):
    B, nb, nq, L = qt.shape
    nk = k.shape[-1]
    S = nb * L
    nblk = min(nblk, nb)
    steps = nb // nblk
    prev = lambda b, c: (b, jnp.maximum(c * nblk - 1, 0), 0)
    cur = lambda b, c: (b, c, 0)
    nxt = lambda b, c: (b, jnp.minimum((c + 1) * nblk, nb - 1), 0)
    four = lambda f: (lambda b, c: f(b, c) + (0,))
    rep, w_args, w_in_specs, w_out_specs, w_out_shape = _ffn_cast_specs(
        ffn_w_in, ffn_w_out, layer, lambda b, c: b * steps + c, B * steps)
    return pl.pallas_call(
        functools.partial(_attn_core_kernel, nb=nb, nblk=nblk, cast_rep=rep),
        grid=(B, steps),
        in_specs=[pl.BlockSpec((1, nblk, nq, L), four(cur)),
                  pl.BlockSpec((1, L, nk), prev), pl.BlockSpec((1, nblk * L, nk), cur),
                  pl.BlockSpec((1, L, nk), nxt),
                  pl.BlockSpec((1, 1, nk, L), four(prev)), pl.BlockSpec((1, nblk, nk, L), four(cur)),
                  pl.BlockSpec((1, 1, nk, L), four(nxt)),
                  _resident(sink_rows.shape)] + w_in_specs,
        out_specs=[pl.BlockSpec((1, nblk * L, nq), cur)] + w_out_specs,
        out_shape=[jax.ShapeDtypeStruct((B, S, nq), BF16)] + w_out_shape,
        scratch_shapes=[pltpu.VMEM(((nblk + 2) * L, nk), BF16),
                        pltpu.VMEM((nblk + 2, nk, L), BF16)],
        compiler_params=_params(),
        name="attn_core",
    )(qt, k, k, k, vt, vt, vt, sink_rows, *w_args)


def _mlstm_mixer(x, norm_w, w_t, layer, b_gate, ffn_w_in, ffn_w_out, ffn_layer):
    bias_rows = jnp.broadcast_to(b_gate.astype(F32).reshape(-1, 1), (b_gate.shape[0], LANES))
    q, kt, v, og, rf = _mlstm_proj(x, norm_w, w_t, layer, bias_rows)
    hs, wi, wo = _mlstm_core(q, kt, v, rf, ffn_w_in, ffn_w_out, ffn_layer)
    return (hs, og), wi, wo


def _attn_mixer(x, pos_rows, inv_freq, norm_w, w_in, layer, sink, ffn_w_in, ffn_w_out, ffn_layer):
    G, L = ATTN_GROUP, LANES
    qt, k, vt = _attn_proj(x, norm_w, w_in, layer, pos_rows, inv_freq)
    sink_rows = jnp.repeat(LOG2E * sink.astype(F32).reshape(ATTN_KV_HEADS, G), L, axis=1)
    a, wi, wo = _attn_core(qt, k, vt, sink_rows, ffn_w_in, ffn_w_out, ffn_layer)
    return (a,), wi, wo


def kernel(x, positions, norm_mix_w, norm_ffn_w, norm_final_w, mlstm_w_in, mlstm_b_gate, mlstm_norm_w, mlstm_w_out, attn_w_in, attn_sink, attn_w_out, ffn_w_in, ffn_w_out):
    depth = norm_mix_w.shape[0]
    B, S = positions.shape
    half = ATTN_HEAD_DIM // 2
    inv_freq = (ROPE_THETA ** (-jnp.arange(half, dtype=F32) / half)).reshape(half, 1)
    pos_rows = positions.reshape(B, 1, S)
    mlstm_wo, attn_wo = mlstm_w_out.astype(BF16), attn_w_out.astype(BF16)
    mlstm_wt = jnp.swapaxes(mlstm_w_in, 1, 2)
    for i in range(depth):
        j = i // 2
        final_w = norm_final_w if i == depth - 1 else None
        if i % 2 == 0:
            (hs, og), wi, wo = _mlstm_mixer(x, norm_mix_w[i], mlstm_wt, j, mlstm_b_gate[j],
                                            ffn_w_in, ffn_w_out, i)
            x = _layer_tail("mlstm", (hs, og, mlstm_norm_w[j]), x, mlstm_wo, j, norm_ffn_w[i], wi, wo,
                            final_w)
        else:
            mixed, wi, wo = _attn_mixer(x, pos_rows, inv_freq, norm_mix_w[i], attn_w_in, j, attn_sink[j],
                                        ffn_w_in, ffn_w_out, i)
            x = _layer_tail("attn", mixed, x, attn_wo, j, norm_ffn_w[i], wi, wo, final_w)
    return x
```

```python
import functools

import jax
import jax.numpy as jnp
from jax import lax
from jax.experimental import pallas as pl
from jax.experimental.pallas import tpu as pltpu

F32 = jnp.float32
BF16 = jnp.bfloat16

EPS = 1e-6
LANES = 128

MLSTM_HEADS = 4
MLSTM_CHUNK = 128
ATTN_HEAD_DIM = 64
ATTN_Q_HEADS = 16
ATTN_KV_HEADS = 4
ATTN_GROUP = ATTN_Q_HEADS // ATTN_KV_HEADS
ATTN_BLOCK = 128
ROPE_THETA = 10000.0

NEG_INF = float("-inf")
LOG2E = 1.4426950408889634
VMEM_LIMIT = 56 * 1024 * 1024


def _params():
    return pltpu.CompilerParams(dimension_semantics=("arbitrary", "arbitrary"),
                                vmem_limit_bytes=VMEM_LIMIT)


def _resident(shape):
    return pl.BlockSpec(shape, lambda *_: (0,) * len(shape), pipeline_mode=pl.Buffered(1))


def _rmsnorm(x, w):
    ms = jnp.mean(x * x, axis=-1, keepdims=True)
    return x * lax.rsqrt(ms + EPS) * w


def _sigmoid(x):
    return 1.0 / (1.0 + jnp.exp(-x))


def _dot(a, b):
    return jnp.dot(a, b, preferred_element_type=F32)


def _dot_nt(a, b):
    return lax.dot_general(a, b, (((1,), (1,)), ((), ())), preferred_element_type=F32)


def _layer_tail_kernel(*refs, d_ff, tf, nsub, mixer, final):
    it = iter(refs)
    if mixer == "mlstm":
        hs_ref, og_ref, hw_ref = next(it), next(it), next(it)
    else:
        a_ref = next(it)
    x_ref, wo_ref, nw_ref, win_ref, wout_ref = (next(it) for _ in range(5))
    fw_ref = next(it) if final else None
    o_ref, x1_ref, hn_ref, act_ref = (next(it) for _ in range(4))

    tm = x_ref.shape[1]
    subs = [slice(r * (tm // nsub), (r + 1) * (tm // nsub)) for r in range(nsub)]
    for rs in subs:
        if mixer == "mlstm":
            dv = hs_ref.shape[-1] // MLSTM_HEADS
            parts = []
            for h in range(MLSTM_HEADS):
                hs = hs_ref[0, rs, h * dv:(h + 1) * dv]
                ms = jnp.mean(hs * hs, axis=-1, keepdims=True)
                parts.append(hs * lax.rsqrt(ms + EPS))
            y = (jnp.concatenate(parts, axis=-1) * hw_ref[...] * og_ref[0, rs, :].astype(F32)).astype(BF16)
        else:
            y = a_ref[0, rs, :]
        x1_ref[rs, :] = x_ref[0, rs, :] + _dot(y, wo_ref[...])
    for rs in subs:
        hn_ref[rs, :] = _rmsnorm(x1_ref[rs, :], nw_ref[...]).astype(BF16)
    for rs in subs:
        for j in range(d_ff // tf):
            hn = hn_ref[rs, :]
            g = _dot(hn, win_ref[:, j * tf:(j + 1) * tf])
            u = _dot(hn, win_ref[:, d_ff + j * tf:d_ff + (j + 1) * tf])
            act_ref[rs, j * tf:(j + 1) * tf] = (g * _sigmoid(g) * u).astype(BF16)
    for rs in subs:
        out = x1_ref[rs, :] + _dot(act_ref[rs, :], wout_ref[...])
        if final:
            out = _rmsnorm(out, fw_ref[...])
        o_ref[0, rs, :] = out


def _layer_tail(mixer, mixer_args, x, w_o, layer_o, norm_w, w_in, w_out, final_w=None,
                *, tm=512, tf=256, nsub=2):
    B, S, D = x.shape
    tm = min(tm, S)
    d_ff = w_out.shape[0]
    final = final_w is not None
    row = pl.BlockSpec((1, tm, D), lambda b, i: (b, i, 0))

    def stacked(shape, l):
        return pl.BlockSpec((None,) + shape, lambda b, i: (l,) + (0,) * len(shape),
                            pipeline_mode=pl.Buffered(1))

    if mixer == "mlstm":
        hs, og, head_w = mixer_args
        args = [hs, og, head_w.reshape(1, D)]
        in_specs = [row, row, _resident((1, D))]
    else:
        args = list(mixer_args)
        in_specs = [row]
    args += [x, w_o, norm_w.reshape(1, D), w_in, w_out]
    in_specs += [row, stacked((w_o.shape[1], D), layer_o), _resident((1, D)),
                 _resident((D, 2 * d_ff)), _resident((d_ff, D))]
    if final:
        args.append(final_w.reshape(1, D))
        in_specs.append(_resident((1, D)))
    return pl.pallas_call(
        functools.partial(_layer_tail_kernel, d_ff=d_ff, tf=tf, nsub=nsub, mixer=mixer, final=final),
        grid=(B, S // tm),
        in_specs=in_specs,
        out_specs=row,
        out_shape=jax.ShapeDtypeStruct((B, S, D), F32),
        scratch_shapes=[pltpu.VMEM((tm, D), F32), pltpu.VMEM((tm, D), BF16),
                        pltpu.VMEM((tm, d_ff), BF16)],
        compiler_params=_params(),
        name=mixer + "_tail",
    )(*args)


def _mlstm_proj_kernel(x_ref, nw_ref, wt_ref, bias_ref,
                       q_ref, kt_ref, v_ref, og_ref, rf_ref, w_ref, wkt_ref, *, dk, tm, nsub):
    d = v_ref.shape[-1]
    hdk = q_ref.shape[-1]
    ng = rf_ref.shape[1] // (tm // LANES)
    sub = tm // nsub

    @pl.when((pl.program_id(0) == 0) & (pl.program_id(1) == 0))
    def _():
        for c in range(0, hdk, LANES):
            w_ref[:, c:c + LANES] = wt_ref[c:c + LANES, :].T.astype(BF16)
            wkt_ref[c:c + LANES, :] = wt_ref[hdk + c:hdk + c + LANES, :].astype(BF16)
        for c in range(0, 2 * d, LANES):
            w_ref[:, hdk + c:hdk + c + LANES] = wt_ref[2 * hdk + c:2 * hdk + c + LANES, :].T.astype(BF16)
        wkt_ref[hdk:, :] = wt_ref[2 * hdk + 2 * d:, :].astype(BF16)

    for r in range(nsub):
        rs = slice(r * sub, (r + 1) * sub)
        hn = _rmsnorm(x_ref[0, rs, :], nw_ref[...]).astype(BF16)
        og_ref[0, rs, :] = _sigmoid(_dot(hn, w_ref[:, hdk + d:])).astype(BF16)
        v_ref[0, rs, :] = _dot(hn, w_ref[:, hdk:hdk + d]).astype(BF16)
        q_ref[0, rs, :] = (_dot(hn, w_ref[:, :hdk]) * (dk ** -0.5)).astype(BF16)
        kg = _dot_nt(wkt_ref[...], hn)
        kt = kg[:hdk].astype(BF16)
        for jj in range(sub // LANES):
            j = r * (sub // LANES) + jj
            kt_ref[0, j] = kt[:, jj * LANES:(jj + 1) * LANES]
            rf_ref[0, j * ng:(j + 1) * ng, :] = kg[hdk:, jj * LANES:(jj + 1) * LANES] + bias_ref[...]


def _mlstm_proj(x, norm_w, w_t, layer, bias_rows, *, tm=1024, nsub=4):
    B, S, D = x.shape
    H = MLSTM_HEADS
    hdk = D // 2
    dk = hdk // H
    tm = min(tm, S)
    ng = bias_rows.shape[0]
    nj = tm // LANES
    NC = S // LANES
    row = lambda b, i: (b, i, 0)
    return pl.pallas_call(
        functools.partial(_mlstm_proj_kernel, dk=dk, tm=tm, nsub=nsub),
        grid=(B, S // tm),
        in_specs=[pl.BlockSpec((1, tm, D), row), _resident((1, D)),
                  pl.BlockSpec((None,) + w_t.shape[1:], lambda b, i: (layer, 0, 0),
                               pipeline_mode=pl.Buffered(1)),
                  _resident((ng, LANES))],
        out_specs=[pl.BlockSpec((1, tm, hdk), row),
                   pl.BlockSpec((1, nj, hdk, LANES), lambda b, i: (b, i, 0, 0)),
                   pl.BlockSpec((1, tm, D), row),
                   pl.BlockSpec((1, tm, D), row),
                   pl.BlockSpec((1, nj * ng, LANES), row)],
        out_shape=[jax.ShapeDtypeStruct((B, S, hdk), BF16),
                   jax.ShapeDtypeStruct((B, NC, hdk, LANES), BF16),
                   jax.ShapeDtypeStruct((B, S, D), BF16),
                   jax.ShapeDtypeStruct((B, S, D), BF16),
                   jax.ShapeDtypeStruct((B, NC * ng, LANES), F32)],
        scratch_shapes=[pltpu.VMEM((D, hdk + 2 * D), BF16), pltpu.VMEM((hdk + ng, D), BF16)],
        compiler_params=_params(),
        name="mlstm_proj",
    )(x, norm_w.reshape(1, D), w_t, bias_rows)


FFN_CAST_BLOCKS = 8


def _ffn_cast_specs(ffn_w_in, ffn_w_out, layer, step_of, total_steps):
    nblk = min(FFN_CAST_BLOCKS, total_steps)
    rep = total_steps // nblk
    args, in_specs, out_specs, out_shape = [], [], [], []
    for w in (ffn_w_in, ffn_w_out):
        rows, cols = w.shape[1] // nblk, w.shape[2]
        args.append(w)
        in_specs.append(pl.BlockSpec((None, rows, cols),
                                     lambda *g, l=layer: (l, step_of(*g) // rep, 0)))
        out_specs.append(pl.BlockSpec((rows, cols), lambda *g: (step_of(*g) // rep, 0)))
        out_shape.append(jax.ShapeDtypeStruct(w.shape[1:], BF16))
    return rep, args, in_specs, out_specs, out_shape


def _ffn_cast(step, rep, srcs, dsts):
    @pl.when(step % rep == 0)
    def _():
        for src, dst in zip(srcs, dsts):
            dst[...] = src[...].astype(BF16)


def _log_sigmoid(x):
    return jnp.minimum(x, 0.0) - jnp.log1p(jnp.exp(-jnp.abs(x)))


def _mlstm_core_kernel(q_ref, kt_ref, v_ref, rf_ref, wi_ref, wo_ref, o_ref, wi_out, wo_out, st_ref, c_ref,
                       *, nc, dv, unroll, cast_rep):
    L = LANES
    h = pl.program_id(1)
    ng = 4 * MLSTM_HEADS
    _ffn_cast(pl.program_id(0) * pl.num_programs(1) + h, cast_rep, (wi_ref, wo_ref), (wi_out, wo_out))

    def gate_rows(g):
        return rf_ref[0, pl.ds(MLSTM_HEADS * g + h, nc, stride=ng), :]

    lane = lax.broadcasted_iota(jnp.int32, (nc, L), 1)

    def prefix_sum(x):
        for s in (1, 2, 4, 8, 16, 32, 64):
            x = x + jnp.where(lane >= s, pltpu.roll(x, s, axis=1), 0.0)
        return x

    def suffix_sum(x):
        for s in (1, 2, 4, 8, 16, 32, 64):
            x = x + jnp.where(lane < L - s, pltpu.roll(x, L - s, axis=1), 0.0)
        return x

    for d in range(2):
        log_i = gate_rows(2 * d)
        log_f = _log_sigmoid(gate_rows(2 * d + 1))
        if d == 0:
            b = prefix_sum(log_f)
            b_last = jnp.broadcast_to(b[:, L - 1:L], (nc, L))
        else:
            b = suffix_sum(log_f)
            b_last = jnp.broadcast_to(b[:, 0:1], (nc, L))
        r = log_i - b
        a = b_last + r
        a_max = jnp.broadcast_to(jnp.max(a, axis=1, keepdims=True), (nc, L))
        st_ref[d, 0] = log_f
        st_ref[d, 1] = r
        st_ref[d, 2] = jnp.exp(a - a_max)
        st_ref[d, 3] = b_last
        st_ref[d, 4] = a_max

    def m_scan(i, carry):
        new = []
        for d, c in ((0, i), (1, nc - 1 - i)):
            row = pl.ds(c, 1)
            st_ref[d, 5, row, :] = carry[d]
            new.append(jnp.maximum(st_ref[d, 3, row, :] + carry[d], st_ref[d, 4, row, :]))
        return tuple(new)

    m0 = jnp.zeros((1, L), F32)
    lax.fori_loop(0, nc, m_scan, (m0, m0))
    for d in range(2):
        b_last, a_max, m_prev = st_ref[d, 3], st_ref[d, 4], st_ref[d, 5]
        m_new = jnp.maximum(b_last + m_prev, a_max)
        st_ref[d, 3] = jnp.exp(b_last + m_prev - m_new)
        st_ref[d, 2] = st_ref[d, 2] * jnp.exp(a_max - m_new)
        st_ref[d, 0] = st_ref[d, 0] * LOG2E
        st_ref[d, 1] = st_ref[d, 1] * LOG2E
        st_ref[d, 5] = m_prev * LOG2E

    c_ref[...] = jnp.zeros_like(c_ref)

    t_idx = lax.broadcasted_iota(jnp.int32, (L, L), 0)
    s_idx = lax.broadcasted_iota(jnp.int32, (L, L), 1)
    masks = (s_idx <= t_idx, s_idx >= t_idx)
    ones = jnp.ones((L, L), BF16)

    def tile3(x):
        return jnp.concatenate([x] * (dv // L + 1), axis=1)

    def body(i, carry, assign):
        items = []
        for u in range(unroll):
            c = i * unroll + u
            items += [(0, c), (1, nc - 1 - c)]

        work = []
        for d, c in items:
            rows = pl.ds(pl.multiple_of(c * L, L), L)
            qc = q_ref[0, rows, :]
            kt = kt_ref[0, c]
            vaug = jnp.concatenate([v_ref[0, rows, :], ones], axis=1)
            w = st_ref[d, 2, pl.ds(c, 1), :]
            scores = _dot(qc, kt)
            kv = _dot(kt * w.astype(BF16), vaug)
            work.append((rows, qc, vaug, scores, kv))

        states = [c_ref[0], c_ref[1]]
        prev_states = []
        for (d, c), (rows, qc, vaug, scores, kv) in zip(items, work):
            prev_states.append(states[d].astype(BF16))
            states[d] = tile3(st_ref[d, 3, pl.ds(c, 1), :]) * states[d] + kv
        c_ref[0] = states[0]
        c_ref[1] = states[1]

        for (d, c), (rows, qc, vaug, scores, kv), prev_state in zip(items, work, prev_states):
            log_f = st_ref[d, 0, pl.ds(c, 1), :]
            r = st_ref[d, 1, pl.ds(c, 1), :]
            m_prev = st_ref[d, 5, pl.ds(c, 1), :]
            mask = masks[d]
            r_masked = jnp.where(mask, r, NEG_INF)
            cm = jnp.max(r_masked, axis=1, keepdims=True)
            b_col = jnp.sum(jnp.where(mask, log_f, 0.0), axis=1, keepdims=True)
            mu = jnp.broadcast_to(jnp.maximum(m_prev[:, 0:1], cm), (L, L))
            decay = jnp.exp2(r_masked - mu)
            inter = jnp.exp2(m_prev - mu)
            lhs = jnp.concatenate([(scores * decay).astype(BF16), inter.astype(BF16) * qc], axis=1)
            both = _dot(lhs, jnp.concatenate([vaug, prev_state], axis=0))
            den = jnp.maximum(jnp.abs(both[:, dv:]), jnp.exp2(-(b_col + mu)))
            out = both[:, :dv] * tile3(1.0 / den)[:, :dv]
            if assign:
                o_ref[0, rows, :] = out
            else:
                o_ref[0, rows, :] += out
        return carry

    steps = nc // 2 // unroll
    lax.fori_loop(0, steps, functools.partial(body, assign=True), 0)
    lax.fori_loop(steps, 2 * steps, functools.partial(body, assign=False), 0)


def _mlstm_core(q, kt, v, rf, ffn_w_in, ffn_w_out, layer, *, unroll=4):
    B, S, hdk = q.shape
    H = MLSTM_HEADS
    dk = hdk // H
    dv = v.shape[-1] // H
    nc = S // LANES
    rep, w_args, w_in_specs, w_out_specs, w_out_shape = _ffn_cast_specs(
        ffn_w_in, ffn_w_out, layer, lambda b, h: b * H + h, B * H)
    return pl.pallas_call(
        functools.partial(_mlstm_core_kernel, nc=nc, dv=dv, unroll=min(unroll, nc // 2), cast_rep=rep),
        grid=(B, H),
        in_specs=[pl.BlockSpec((1, S, dk), lambda b, h: (b, 0, h)),
                  pl.BlockSpec((1, nc, dk, LANES), lambda b, h: (b, 0, h, 0)),
                  pl.BlockSpec((1, S, dv), lambda b, h: (b, 0, h)),
                  pl.BlockSpec((1, rf.shape[1], LANES), lambda b, h: (b, 0, 0))] + w_in_specs,
        out_specs=[pl.BlockSpec((1, S, dv), lambda b, h: (b, 0, h))] + w_out_specs,
        out_shape=[jax.ShapeDtypeStruct((B, S, H * dv), F32)] + w_out_shape,
        scratch_shapes=[pltpu.VMEM((2, 6, nc, LANES), F32),
                        pltpu.VMEM((2, dk, dv + LANES), F32)],
        compiler_params=_params(),
        name="mlstm_core",
    )(q, kt, v, rf, *w_args)


def _attn_weight_prep(w_ref, wt_ref):
    @pl.when((pl.program_id(0) == 0) & (pl.program_id(1) == 0))
    def _():
        for c in range(0, wt_ref.shape[0], LANES):
            wt_ref[c:c + LANES, :] = w_ref[:, c:c + LANES].T.astype(BF16)


def _attn_proj_rows(x_rows, pos_row, invf, nw, wt_ref, qt_ref, k_ref, vt_ref, chunk0):
    hd = ATTN_HEAD_DIM
    half = hd // 2
    nq = ATTN_Q_HEADS * hd
    nk = ATTN_KV_HEADS * hd
    scale = LOG2E * hd ** -0.5
    piece = 8 * hd
    sub = x_rows.shape[0]
    hn = _rmsnorm(x_rows, nw).astype(BF16)
    ang = invf * pos_row.astype(F32)
    cos = jnp.cos(ang)
    sin = jnp.sin(ang)
    chunks = [(chunk0 + jj, slice(jj * LANES, (jj + 1) * LANES)) for jj in range(sub // LANES)]

    def proj(lo, hi):
        return _dot_nt(wt_ref[lo:hi, :], hn)

    def rope(xh):
        x1, x2 = xh[:half], xh[half:]
        return jnp.concatenate([x1 * cos - x2 * sin, x2 * cos + x1 * sin], axis=0)

    kv = proj(nq, nq + 2 * nk)
    vt = kv[nk:].astype(BF16)
    for j, cols in chunks:
        vt_ref[0, j] = vt[:, cols]
    kt = jnp.concatenate([rope(kv[g * hd:(g + 1) * hd]) for g in range(ATTN_KV_HEADS)], axis=0)
    for j, cols in chunks:
        for f in range(nk // LANES):
            k_ref[0, j * LANES:(j + 1) * LANES, f * LANES:(f + 1) * LANES] = (
                kt[f * LANES:(f + 1) * LANES, cols].T.astype(BF16))
    for lo in range(0, nq, piece):
        qp = proj(lo, lo + piece)
        qt = jnp.concatenate([rope(qp[h * hd:(h + 1) * hd]) * scale for h in range(piece // hd)],
                             axis=0).astype(BF16)
        for j, cols in chunks:
            qt_ref[0, j, lo:lo + piece, :] = qt[:, cols]


def _attn_proj_kernel(x_ref, nw_ref, w_ref, pos_ref, invf_ref, qt_ref, k_ref, vt_ref, wt_ref, *, tm, nsub):
    _attn_weight_prep(w_ref, wt_ref)
    sub = tm // nsub
    for r in range(nsub):
        rs = slice(r * sub, (r + 1) * sub)
        _attn_proj_rows(x_ref[0, rs, :], pos_ref[0, :, rs], invf_ref[...], nw_ref[...], wt_ref,
                        qt_ref, k_ref, vt_ref, r * (sub // LANES))


def _attn_proj_specs(S, D, tm, w_in, layer):
    nq = ATTN_Q_HEADS * ATTN_HEAD_DIM
    nk = ATTN_KV_HEADS * ATTN_HEAD_DIM
    nj = tm // LANES
    nb = S // LANES
    in_specs = [_resident((1, D)),
                pl.BlockSpec((None,) + w_in.shape[1:], lambda b, i: (layer, 0, 0),
                             pipeline_mode=pl.Buffered(1)),
                pl.BlockSpec((1, 1, tm), lambda b, i: (b, 0, i)),
                _resident((ATTN_HEAD_DIM // 2, 1))]
    out_specs = [pl.BlockSpec((1, nj, nq, LANES), lambda b, i: (b, i, 0, 0)),
                 pl.BlockSpec((1, tm, nk), lambda b, i: (b, i, 0)),
                 pl.BlockSpec((1, nj, nk, LANES), lambda b, i: (b, i, 0, 0))]
    out_shape = lambda B: [jax.ShapeDtypeStruct((B, nb, nq, LANES), BF16),
                           jax.ShapeDtypeStruct((B, S, nk), BF16),
                           jax.ShapeDtypeStruct((B, nb, nk, LANES), BF16)]
    scratch = [pltpu.VMEM((nq + 2 * nk, D), BF16)]
    return in_specs, out_specs, out_shape, scratch


def _attn_proj(x, norm_w, w_in, layer, pos_rows, inv_freq, *, tm=1024, nsub=4):
    B, S, D = x.shape
    tm = min(tm, S)
    p_in, p_out, p_shape, p_scratch = _attn_proj_specs(S, D, tm, w_in, layer)
    return pl.pallas_call(
        functools.partial(_attn_proj_kernel, tm=tm, nsub=nsub),
        grid=(B, S // tm),
        in_specs=[pl.BlockSpec((1, tm, D), lambda b, i: (b, i, 0))] + p_in,
        out_specs=p_out,
        out_shape=p_shape(B),
        scratch_shapes=p_scratch,
        compiler_params=_params(),
        name="attn_proj",
    )(x, norm_w.reshape(1, D), w_in, pos_rows, inv_freq)


def _attn_core_kernel(qt_ref, kp_ref, kc_ref, kn_ref, vp_ref, vc_ref, vn_ref, sink_ref, wi_ref, wo_ref,
                      o_ref, wi_out, wo_out, k_all, v_all, *, nb, nblk, cast_rep):
    L = LANES
    hd = ATTN_HEAD_DIM
    G = ATTN_GROUP
    c = pl.program_id(1)
    _ffn_cast(pl.program_id(0) * pl.num_programs(1) + c, cast_rep, (wi_ref, wo_ref), (wi_out, wo_out))
    k_all[0:L] = kp_ref[0]
    k_all[L:(nblk + 1) * L] = kc_ref[0]
    k_all[(nblk + 1) * L:] = kn_ref[0]
    v_all[0] = vp_ref[0, 0]
    v_all[1:nblk + 1] = vc_ref[0]
    v_all[nblk + 1] = vn_ref[0, 0]

    key = lax.broadcasted_iota(jnp.int32, (L, G * L), 0)
    qry = lax.broadcasted_iota(jnp.int32, (L, G * L), 1) % L
    prev_mask = jnp.where(key >= qry, 0.0, NEG_INF)
    next_mask = jnp.where(key <= qry, 0.0, NEG_INF)
    zeros = jnp.zeros((hd, G * L), BF16)
    ones = jnp.ones((16, 3 * L), BF16)

    def scores(j, g):
        qg = jnp.concatenate([qt_ref[0, j, (G * g + i) * hd:(G * g + i + 1) * hd, :] for i in range(G)],
                             axis=1)
        qz = jnp.concatenate([qg, zeros] if g % 2 == 0 else [zeros, qg], axis=0)
        return _dot(k_all[j * L:(j + 3) * L, (g // 2) * L:(g // 2 + 1) * L], qz)

    items = [(j, g) for j in range(nblk) for g in range(ATTN_KV_HEADS)]
    ahead = 2
    pending = [scores(*it) for it in items[:ahead]]
    for n, (j, g) in enumerate(items):
        s = pending.pop(0)
        if n + ahead < len(items):
            pending.append(scores(*items[n + ahead]))
        if g == 0:
            blk = c * nblk + j
            prev_bias = prev_mask + jnp.where(blk > 0, 0.0, NEG_INF)
            next_bias = next_mask + jnp.where(blk < nb - 1, 0.0, NEG_INF)
        s = [s[0:L] + prev_bias, s[L:2 * L], s[2 * L:] + next_bias]
        sink = sink_ref[g:g + 1, :]
        m = sink
        for sj in s:
            m = jnp.maximum(m, jnp.max(sj, axis=0, keepdims=True))
        p = jnp.concatenate([jnp.exp2(sj - m).astype(BF16) for sj in s], axis=0)
        vg = jnp.concatenate([v_all[j + i, g * hd:(g + 1) * hd, :] for i in range(3)], axis=1)
        acc = _dot(jnp.concatenate([vg, ones], axis=0), p)
        denom = acc[hd:hd + 1] + jnp.exp2(sink - m)
        out = acc[:hd] * (1.0 / denom)
        for pair in range(G // 2):
            tile = jnp.concatenate([out[:, (2 * pair) * L:(2 * pair + 1) * L],
                                    out[:, (2 * pair + 1) * L:(2 * pair + 2) * L]], axis=0)
            col = (G * g + 2 * pair) * hd
            o_ref[0, j * L:(j + 1) * L, col:col + L] = tile.T.astype(BF16)


def _attn_core(qt, k, vt, sink_rows, ffn_w_in, ffn_w_out, layer, *, nblk=8):
    B, nb, nq, L = qt.shape
    nk = k.shape[-1]
    S = nb * L
    nblk = min(nblk, nb)
    steps = nb // nblk
    prev = lambda b, c: (b, jnp.maximum(c * nblk - 1, 0), 0)
    cur = lambda b, c: (b, c, 0)
    nxt = lambda b, c: (b, jnp.minimum((c + 1) * nblk, nb - 1), 0)
    four = lambda f: (lambda b, c: f(b, c) + (0,))
    rep, w_args, w_in_specs, w_out_specs, w_out_shape = _ffn_cast_specs(
        ffn_w_in, ffn_w_out, layer, lambda b, c: b * steps + c, B * steps)
    return pl.pallas_call(
        functools.partial(_attn_core_kernel, nb=nb, nblk=nblk, cast_rep=rep),
        grid=(B, steps),
        in_specs=[pl.BlockSpec((1, nblk, nq, L), four(cur)),
                  pl.BlockSpec((1, L, nk), prev), pl.BlockSpec((1, nblk * L, nk), cur),
                  pl.BlockSpec((1, L, nk), nxt),
                  pl.BlockSpec((1, 1, nk, L), four(prev)), pl.BlockSpec((1, nblk, nk, L), four(cur)),
                  pl.BlockSpec((1, 1, nk, L), four(nxt)),
                  _resident(sink_rows.shape)] + w_in_specs,
        out_specs=[pl.BlockSpec((1, nblk * L, nq), cur)] + w_out_specs,
        out_shape=[jax.ShapeDtypeStruct((B, S, nq), BF16)] + w_out_shape,
        scratch_shapes=[pltpu.VMEM(((nblk + 2) * L, nk), BF16),
                        pltpu.VMEM((nblk + 2, nk, L), BF16)],
        compiler_params=_params(),
        name="attn_core",
    )(qt, k, k, k, vt, vt, vt, sink_rows, *w_args)


def _mlstm_mixer(x, norm_w, w_t, layer, b_gate, ffn_w_in, ffn_w_out, ffn_layer):
    bias_rows = jnp.broadcast_to(b_gate.astype(F32).reshape(-1, 1), (b_gate.shape[0], LANES))
    q, kt, v, og, rf = _mlstm_proj(x, norm_w, w_t, layer, bias_rows)
    hs, wi, wo = _mlstm_core(q, kt, v, rf, ffn_w_in, ffn_w_out, ffn_layer)
    return (hs, og), wi, wo


def _attn_mixer(x, pos_rows, inv_freq, norm_w, w_in, layer, sink, ffn_w_in, ffn_w_out, ffn_layer):
    G, L = ATTN_GROUP, LANES
    qt, k, vt = _attn_proj(x, norm_w, w_in, layer, pos_rows, inv_freq)
    sink_rows = jnp.repeat(LOG2E * sink.astype(F32).reshape(ATTN_KV_HEADS, G), L, axis=1)
    a, wi, wo = _attn_core(qt, k, vt, sink_rows, ffn_w_in, ffn_w_out, ffn_layer)
    return (a,), wi, wo


def kernel(x, positions, norm_mix_w, norm_ffn_w, norm_final_w, mlstm_w_in, mlstm_b_gate, mlstm_norm_w, mlstm_w_out, attn_w_in, attn_sink, attn_w_out, ffn_w_in, ffn_w_out):
    depth = norm_mix_w.shape[0]
    B, S = positions.shape
    half = ATTN_HEAD_DIM // 2
    inv_freq = (ROPE_THETA ** (-jnp.arange(half, dtype=F32) / half)).reshape(half, 1)
    pos_rows = positions.reshape(B, 1, S)
    mlstm_wo, attn_wo = mlstm_w_out.astype(BF16), attn_w_out.astype(BF16)
    mlstm_wt = jnp.swapaxes(mlstm_w_in, 1, 2)
    for i in range(depth):
        j = i // 2
        final_w = norm_final_w if i == depth - 1 else None
        if i % 2 == 0:
            (hs, og), wi, wo = _mlstm_mixer(x, norm_mix_w[i], mlstm_wt, j, mlstm_b_gate[j],
                                            ffn_w_in, ffn_w_out, i)
            x = _layer_tail("mlstm", (hs, og, mlstm_norm_w[j]), x, mlstm_wo, j, norm_ffn_w[i], wi, wo,
                            final_w)
        else:
            mixed, wi, wo = _attn_mixer(x, pos_rows, inv_freq, norm_mix_w[i], attn_w_in, j, attn_sink[j],
                                        ffn_w_in, ffn_w_out, i)
            x = _layer_tail("attn", mixed, x, attn_wo, j, norm_ffn_w[i], wi, wo, final_w)
    return x
```

```python
import functools

import jax
import jax.numpy as jnp
from jax import lax
from jax.experimental import pallas as pl
from jax.experimental.pallas import tpu as pltpu

F32 = jnp.float32
BF16 = jnp.bfloat16

EPS = 1e-6
LANES = 128

MLSTM_HEADS = 4
MLSTM_CHUNK = 128
ATTN_HEAD_DIM = 64
ATTN_Q_HEADS = 16
ATTN_KV_HEADS = 4
ATTN_GROUP = ATTN_Q_HEADS // ATTN_KV_HEADS
ATTN_BLOCK = 128
ROPE_THETA = 10000.0

NEG_INF = float("-inf")
LOG2E = 1.4426950408889634
VMEM_LIMIT = 56 * 1024 * 1024


def _params():
    return pltpu.CompilerParams(dimension_semantics=("arbitrary", "arbitrary"),
                                vmem_limit_bytes=VMEM_LIMIT)


def _resident(shape):
    return pl.BlockSpec(shape, lambda *_: (0,) * len(shape), pipeline_mode=pl.Buffered(1))


def _rmsnorm(x, w):
    ms = jnp.mean(x * x, axis=-1, keepdims=True)
    return x * lax.rsqrt(ms + EPS) * w


def _sigmoid(x):
    return 1.0 / (1.0 + jnp.exp(-x))


def _dot(a, b):
    return jnp.dot(a, b, preferred_element_type=F32)


def _dot_nt(a, b):
    return lax.dot_general(a, b, (((1,), (1,)), ((), ())), preferred_element_type=F32)


def _layer_tail_kernel(*refs, d_ff, tf, nsub, mixer, final):
    it = iter(refs)
    if mixer == "mlstm":
        hs_ref, og_ref, hw_ref = next(it), next(it), next(it)
    else:
        a_ref = next(it)
    x_ref, wo_ref, nw_ref, win_ref, wout_ref = (next(it) for _ in range(5))
    fw_ref = next(it) if final else None
    o_ref, x1_ref, hn_ref, act_ref = (next(it) for _ in range(4))

    tm = x_ref.shape[1]
    subs = [slice(r * (tm // nsub), (r + 1) * (tm // nsub)) for r in range(nsub)]
    for rs in subs:
        if mixer == "mlstm":
            dv = hs_ref.shape[-1] // MLSTM_HEADS
            parts = []
            for h in range(MLSTM_HEADS):
                hs = hs_ref[0, rs, h * dv:(h + 1) * dv]
                ms = jnp.mean(hs * hs, axis=-1, keepdims=True)
                parts.append(hs * lax.rsqrt(ms + EPS))
            y = (jnp.concatenate(parts, axis=-1) * hw_ref[...] * og_ref[0, rs, :].astype(F32)).astype(BF16)
        else:
            y = a_ref[0, rs, :]
        x1_ref[rs, :] = x_ref[0, rs, :] + _dot(y, wo_ref[...])
    for rs in subs:
        hn_ref[rs, :] = _rmsnorm(x1_ref[rs, :], nw_ref[...]).astype(BF16)
    for rs in subs:
        for j in range(d_ff // tf):
            hn = hn_ref[rs, :]
            g = _dot(hn, win_ref[:, j * tf:(j + 1) * tf])
            u = _dot(hn, win_ref[:, d_ff + j * tf:d_ff + (j + 1) * tf])
            act_ref[rs, j * tf:(j + 1) * tf] = (g * _sigmoid(g) * u).astype(BF16)
    for rs in subs:
        out = x1_ref[rs, :] + _dot(act_ref[rs, :], wout_ref[...])
        if final:
            out = _rmsnorm(out, fw_ref[...])
        o_ref[0, rs, :] = out


def _layer_tail(mixer, mixer_args, x, w_o, layer_o, norm_w, w_in, w_out, final_w=None,
                *, tm=512, tf=256, nsub=2):
    B, S, D = x.shape
    tm = min(tm, S)
    d_ff = w_out.shape[0]
    final = final_w is not None
    row = pl.BlockSpec((1, tm, D), lambda b, i: (b, i, 0))

    def stacked(shape, l):
        return pl.BlockSpec((None,) + shape, lambda b, i: (l,) + (0,) * len(shape),
                            pipeline_mode=pl.Buffered(1))

    if mixer == "mlstm":
        hs, og, head_w = mixer_args
        args = [hs, og, head_w.reshape(1, D)]
        in_specs = [row, row, _resident((1, D))]
    else:
        args = list(mixer_args)
        in_specs = [row]
    args += [x, w_o, norm_w.reshape(1, D), w_in, w_out]
    in_specs += [row, stacked((w_o.shape[1], D), layer_o), _resident((1, D)),
                 _resident((D, 2 * d_ff)), _resident((d_ff, D))]
    if final:
        args.append(final_w.reshape(1, D))
        in_specs.append(_resident((1, D)))
    return pl.pallas_call(
        functools.partial(_layer_tail_kernel, d_ff=d_ff, tf=tf, nsub=nsub, mixer=mixer, final=final),
        grid=(B, S // tm),
        in_specs=in_specs,
        out_specs=row,
        out_shape=jax.ShapeDtypeStruct((B, S, D), F32),
        scratch_shapes=[pltpu.VMEM((tm, D), F32), pltpu.VMEM((tm, D), BF16),
                        pltpu.VMEM((tm, d_ff), BF16)],
        compiler_params=_params(),
        name=mixer + "_tail",
    )(*args)


def _mlstm_proj_kernel(x_ref, nw_ref, wt_ref, bias_ref,
                       q_ref, kt_ref, v_ref, og_ref, rf_ref, w_ref, wkt_ref, *, dk, tm, nsub):
    d = v_ref.shape[-1]
    hdk = q_ref.shape[-1]
    ng = rf_ref.shape[1] // (tm // LANES)
    sub = tm // nsub

    @pl.when((pl.program_id(0) == 0) & (pl.program_id(1) == 0))
    def _():
        for c in range(0, hdk, LANES):
            w_ref[:, c:c + LANES] = wt_ref[c:c + LANES, :].T.astype(BF16)
            wkt_ref[c:c + LANES, :] = wt_ref[hdk + c:hdk + c + LANES, :].astype(BF16)
        for c in range(0, 2 * d, LANES):
            w_ref[:, hdk + c:hdk + c + LANES] = wt_ref[2 * hdk + c:2 * hdk + c + LANES, :].T.astype(BF16)
        wkt_ref[hdk:, :] = wt_ref[2 * hdk + 2 * d:, :].astype(BF16)

    for r in range(nsub):
        rs = slice(r * sub, (r + 1) * sub)
        hn = _rmsnorm(x_ref[0, rs, :], nw_ref[...]).astype(BF16)
        og_ref[0, rs, :] = _sigmoid(_dot(hn, w_ref[:, hdk + d:])).astype(BF16)
        v_ref[0, rs, :] = _dot(hn, w_ref[:, hdk:hdk + d]).astype(BF16)
        q_ref[0, rs, :] = (_dot(hn, w_ref[:, :hdk]) * (dk ** -0.5)).astype(BF16)
        kg = _dot_nt(wkt_ref[...], hn)
        kt = kg[:hdk].astype(BF16)
        for jj in range(sub // LANES):
            j = r * (sub // LANES) + jj
            kt_ref[0, j] = kt[:, jj * LANES:(jj + 1) * LANES]
            rf_ref[0, j * ng:(j + 1) * ng, :] = kg[hdk:, jj * LANES:(jj + 1) * LANES] + bias_ref[...]


def _mlstm_proj(x, norm_w, w_t, layer, bias_rows, *, tm=1024, nsub=4):
    B, S, D = x.shape
    H = MLSTM_HEADS
    hdk = D // 2
    dk = hdk // H
    tm = min(tm, S)
    ng = bias_rows.shape[0]
    nj = tm // LANES
    NC = S // LANES
    row = lambda b, i: (b, i, 0)
    return pl.pallas_call(
        functools.partial(_mlstm_proj_kernel, dk=dk, tm=tm, nsub=nsub),
        grid=(B, S // tm),
        in_specs=[pl.BlockSpec((1, tm, D), row), _resident((1, D)),
                  pl.BlockSpec((None,) + w_t.shape[1:], lambda b, i: (layer, 0, 0),
                               pipeline_mode=pl.Buffered(1)),
                  _resident((ng, LANES))],
        out_specs=[pl.BlockSpec((1, tm, hdk), row),
                   pl.BlockSpec((1, nj, hdk, LANES), lambda b, i: (b, i, 0, 0)),
                   pl.BlockSpec((1, tm, D), row),
                   pl.BlockSpec((1, tm, D), row),
                   pl.BlockSpec((1, nj * ng, LANES), row)],
        out_shape=[jax.ShapeDtypeStruct((B, S, hdk), BF16),
                   jax.ShapeDtypeStruct((B, NC, hdk, LANES), BF16),
                   jax.ShapeDtypeStruct((B, S, D), BF16),
                   jax.ShapeDtypeStruct((B, S, D), BF16),
                   jax.ShapeDtypeStruct((B, NC * ng, LANES), F32)],
        scratch_shapes=[pltpu.VMEM((D, hdk + 2 * D), BF16), pltpu.VMEM((hdk + ng, D), BF16)],
        compiler_params=_params(),
        name="mlstm_proj",
    )(x, norm_w.reshape(1, D), w_t, bias_rows)


FFN_CAST_BLOCKS = 8


def _ffn_cast_specs(ffn_w_in, ffn_w_out, layer, step_of, total_steps):
    nblk = min(FFN_CAST_BLOCKS, total_steps)
    rep = total_steps // nblk
    args, in_specs, out_specs, out_shape = [], [], [], []
    for w in (ffn_w_in, ffn_w_out):
        rows, cols = w.shape[1] // nblk, w.shape[2]
        args.append(w)
        in_specs.append(pl.BlockSpec((None, rows, cols),
                                     lambda *g, l=layer: (l, step_of(*g) // rep, 0)))
        out_specs.append(pl.BlockSpec((rows, cols), lambda *g: (step_of(*g) // rep, 0)))
        out_shape.append(jax.ShapeDtypeStruct(w.shape[1:], BF16))
    return rep, args, in_specs, out_specs, out_shape


def _ffn_cast(step, rep, srcs, dsts):
    @pl.when(step % rep == 0)
    def _():
        for src, dst in zip(srcs, dsts):
            dst[...] = src[...].astype(BF16)


def _log_sigmoid(x):
    return jnp.minimum(x, 0.0) - jnp.log1p(jnp.exp(-jnp.abs(x)))


def _mlstm_core_kernel(q_ref, kt_ref, v_ref, rf_ref, wi_ref, wo_ref, o_ref, wi_out, wo_out, st_ref, c_ref,
                       *, nc, dv, unroll, cast_rep):
    L = LANES
    h = pl.program_id(1)
    ng = 4 * MLSTM_HEADS
    _ffn_cast(pl.program_id(0) * pl.num_programs(1) + h, cast_rep, (wi_ref, wo_ref), (wi_out, wo_out))

    def gate_rows(g):
        return rf_ref[0, pl.ds(MLSTM_HEADS * g + h, nc, stride=ng), :]

    lane = lax.broadcasted_iota(jnp.int32, (nc, L), 1)

    def prefix_sum(x):
        for s in (1, 2, 4, 8, 16, 32, 64):
            x = x + jnp.where(lane >= s, pltpu.roll(x, s, axis=1), 0.0)
        return x

    def suffix_sum(x):
        for s in (1, 2, 4, 8, 16, 32, 64):
            x = x + jnp.where(lane < L - s, pltpu.roll(x, L - s, axis=1), 0.0)
        return x

    for d in range(2):
        log_i = gate_rows(2 * d)
        log_f = _log_sigmoid(gate_rows(2 * d + 1))
        if d == 0:
            b = prefix_sum(log_f)
            b_last = jnp.broadcast_to(b[:, L - 1:L], (nc, L))
        else:
            b = suffix_sum(log_f)
            b_last = jnp.broadcast_to(b[:, 0:1], (nc, L))
        r = log_i - b
        a = b_last + r
        a_max = jnp.broadcast_to(jnp.max(a, axis=1, keepdims=True), (nc, L))
        st_ref[d, 0] = log_f
        st_ref[d, 1] = r
        st_ref[d, 2] = jnp.exp(a - a_max)
        st_ref[d, 3] = b_last
        st_ref[d, 4] = a_max

    def m_scan(i, carry):
        new = []
        for d, c in ((0, i), (1, nc - 1 - i)):
            row = pl.ds(c, 1)
            st_ref[d, 5, row, :] = carry[d]
            new.append(jnp.maximum(st_ref[d, 3, row, :] + carry[d], st_ref[d, 4, row, :]))
        return tuple(new)

    m0 = jnp.zeros((1, L), F32)
    lax.fori_loop(0, nc, m_scan, (m0, m0))
    for d in range(2):
        b_last, a_max, m_prev = st_ref[d, 3], st_ref[d, 4], st_ref[d, 5]
        m_new = jnp.maximum(b_last + m_prev, a_max)
        st_ref[d, 3] = jnp.exp(b_last + m_prev - m_new)
        st_ref[d, 2] = st_ref[d, 2] * jnp.exp(a_max - m_new)
        st_ref[d, 0] = st_ref[d, 0] * LOG2E
        st_ref[d, 1] = st_ref[d, 1] * LOG2E
        st_ref[d, 5] = m_prev * LOG2E

    c_ref[...] = jnp.zeros_like(c_ref)

    t_idx = lax.broadcasted_iota(jnp.int32, (L, L), 0)
    s_idx = lax.broadcasted_iota(jnp.int32, (L, L), 1)
    masks = (s_idx <= t_idx, s_idx >= t_idx)
    ones = jnp.ones((L, L), BF16)

    def tile3(x):
        return jnp.concatenate([x] * (dv // L + 1), axis=1)

    def body(i, carry, assign):
        items = []
        for u in range(unroll):
            c = i * unroll + u
            items += [(0, c), (1, nc - 1 - c)]

        work = []
        for d, c in items:
            rows = pl.ds(pl.multiple_of(c * L, L), L)
            qc = q_ref[0, rows, :]
            kt = kt_ref[0, c]
            vaug = jnp.concatenate([v_ref[0, rows, :], ones], axis=1)
            w = st_ref[d, 2, pl.ds(c, 1), :]
            scores = _dot(qc, kt)
            kv = _dot(kt * w.astype(BF16), vaug)
            work.append((rows, qc, vaug, scores, kv))

        states = [c_ref[0], c_ref[1]]
        prev_states = []
        for (d, c), (rows, qc, vaug, scores, kv) in zip(items, work):
            prev_states.append(states[d].astype(BF16))
            states[d] = tile3(st_ref[d, 3, pl.ds(c, 1), :]) * states[d] + kv
        c_ref[0] = states[0]
        c_ref[1] = states[1]

        for (d, c), (rows, qc, vaug, scores, kv), prev_state in zip(items, work, prev_states):
            log_f = st_ref[d, 0, pl.ds(c, 1), :]
            r = st_ref[d, 1, pl.ds(c, 1), :]
            m_prev = st_ref[d, 5, pl.ds(c, 1), :]
            mask = masks[d]
            r_masked = jnp.where(mask, r, NEG_INF)
            cm = jnp.max(r_masked, axis=1, keepdims=True)
            b_col = jnp.sum(jnp.where(mask, log_f, 0.0), axis=1, keepdims=True)
            mu = jnp.broadcast_to(jnp.maximum(m_prev[:, 0:1], cm), (L, L))
            decay = jnp.exp2(r_masked - mu)
            inter = jnp.exp2(m_prev - mu)
            lhs = jnp.concatenate([(scores * decay).astype(BF16), inter.astype(BF16) * qc], axis=1)
            both = _dot(lhs, jnp.concatenate([vaug, prev_state], axis=0))
            den = jnp.maximum(jnp.abs(both[:, dv:]), jnp.exp2(-(b_col + mu)))
            out = both[:, :dv] * tile3(1.0 / den)[:, :dv]
            if assign:
                o_ref[0, rows, :] = out
            else:
                o_ref[0, rows, :] += out
        return carry

    steps = nc // 2 // unroll
    lax.fori_loop(0, steps, functools.partial(body, assign=True), 0)
    lax.fori_loop(steps, 2 * steps, functools.partial(body, assign=False), 0)


def _mlstm_core(q, kt, v, rf, ffn_w_in, ffn_w_out, layer, *, unroll=4):
    B, S, hdk = q.shape
    H = MLSTM_HEADS
    dk = hdk // H
    dv = v.shape[-1] // H
    nc = S // LANES
    rep, w_args, w_in_specs, w_out_specs, w_out_shape = _ffn_cast_specs(
        ffn_w_in, ffn_w_out, layer, lambda b, h: b * H + h, B * H)
    return pl.pallas_call(
        functools.partial(_mlstm_core_kernel, nc=nc, dv=dv, unroll=min(unroll, nc // 2), cast_rep=rep),
        grid=(B, H),
        in_specs=[pl.BlockSpec((1, S, dk), lambda b, h: (b, 0, h)),
                  pl.BlockSpec((1, nc, dk, LANES), lambda b, h: (b, 0, h, 0)),
                  pl.BlockSpec((1, S, dv), lambda b, h: (b, 0, h)),
                  pl.BlockSpec((1, rf.shape[1], LANES), lambda b, h: (b, 0, 0))] + w_in_specs,
        out_specs=[pl.BlockSpec((1, S, dv), lambda b, h: (b, 0, h))] + w_out_specs,
        out_shape=[jax.ShapeDtypeStruct((B, S, H * dv), F32)] + w_out_shape,
        scratch_shapes=[pltpu.VMEM((2, 6, nc, LANES), F32),
                        pltpu.VMEM((2, dk, dv + LANES), F32)],
        compiler_params=_params(),
        name="mlstm_core",
    )(q, kt, v, rf, *w_args)


def _attn_weight_prep(w_ref, wt_ref):
    @pl.when((pl.program_id(0) == 0) & (pl.program_id(1) == 0))
    def _():
        for c in range(0, wt_ref.shape[0], LANES):
            wt_ref[c:c + LANES, :] = w_ref[:, c:c + LANES].T.astype(BF16)


def _attn_proj_rows(x_rows, pos_row, invf, nw, wt_ref, qt_ref, k_ref, vt_ref, chunk0):
    hd = ATTN_HEAD_DIM
    half = hd // 2
    nq = ATTN_Q_HEADS * hd
    nk = ATTN_KV_HEADS * hd
    scale = LOG2E * hd ** -0.5
    piece = 8 * hd
    sub = x_rows.shape[0]
    hn = _rmsnorm(x_rows, nw).astype(BF16)
    ang = invf * pos_row.astype(F32)
    cos = jnp.cos(ang)
    sin = jnp.sin(ang)
    chunks = [(chunk0 + jj, slice(jj * LANES, (jj + 1) * LANES)) for jj in range(sub // LANES)]

    def rope(xh):
        x1, x2 = xh[:half], xh[half:]
        return jnp.concatenate([x1 * cos - x2 * sin, x2 * cos + x1 * sin], axis=0)

    qp = _dot_nt(wt_ref[:nq, :], hn)
    kv = _dot_nt(wt_ref[nq:, :], hn)
    for lo in range(0, nq, piece):
        qt = jnp.concatenate([rope(qp[lo + h * hd:lo + (h + 1) * hd]) * scale for h in range(piece // hd)],
                             axis=0).astype(BF16)
        for j, cols in chunks:
            qt_ref[0, j, lo:lo + piece, :] = qt[:, cols]
    vt = kv[nk:].astype(BF16)
    for j, cols in chunks:
        vt_ref[0, j] = vt[:, cols]
    kt = jnp.concatenate([rope(kv[g * hd:(g + 1) * hd]) for g in range(ATTN_KV_HEADS)], axis=0)
    for j, cols in chunks:
        for f in range(nk // LANES):
            k_ref[0, j * LANES:(j + 1) * LANES, f * LANES:(f + 1) * LANES] = (
                kt[f * LANES:(f + 1) * LANES, cols].T.astype(BF16))


def _attn_proj_kernel(x_ref, nw_ref, w_ref, pos_ref, invf_ref, qt_ref, k_ref, vt_ref, wt_ref, *, tm, nsub):
    _attn_weight_prep(w_ref, wt_ref)
    sub = tm // nsub
    for r in range(nsub):
        rs = slice(r * sub, (r + 1) * sub)
        _attn_proj_rows(x_ref[0, rs, :], pos_ref[0, :, rs], invf_ref[...], nw_ref[...], wt_ref,
                        qt_ref, k_ref, vt_ref, r * (sub // LANES))


def _attn_proj_specs(S, D, tm, w_in, layer):
    nq = ATTN_Q_HEADS * ATTN_HEAD_DIM
    nk = ATTN_KV_HEADS * ATTN_HEAD_DIM
    nj = tm // LANES
    nb = S // LANES
    in_specs = [_resident((1, D)),
                pl.BlockSpec((None,) + w_in.shape[1:], lambda b, i: (layer, 0, 0),
                             pipeline_mode=pl.Buffered(1)),
                pl.BlockSpec((1, 1, tm), lambda b, i: (b, 0, i)),
                _resident((ATTN_HEAD_DIM // 2, 1))]
    out_specs = [pl.BlockSpec((1, nj, nq, LANES), lambda b, i: (b, i, 0, 0)),
                 pl.BlockSpec((1, tm, nk), lambda b, i: (b, i, 0)),
                 pl.BlockSpec((1, nj, nk, LANES), lambda b, i: (b, i, 0, 0))]
    out_shape = lambda B: [jax.ShapeDtypeStruct((B, nb, nq, LANES), BF16),
                           jax.ShapeDtypeStruct((B, S, nk), BF16),
                           jax.ShapeDtypeStruct((B, nb, nk, LANES), BF16)]
    scratch = [pltpu.VMEM((nq + 2 * nk, D), BF16)]
    return in_specs, out_specs, out_shape, scratch


def _attn_proj(x, norm_w, w_in, layer, pos_rows, inv_freq, *, tm=1024, nsub=4):
    B, S, D = x.shape
    tm = min(tm, S)
    p_in, p_out, p_shape, p_scratch = _attn_proj_specs(S, D, tm, w_in, layer)
    return pl.pallas_call(
        functools.partial(_attn_proj_kernel, tm=tm, nsub=nsub),
        grid=(B, S // tm),
        in_specs=[pl.BlockSpec((1, tm, D), lambda b, i: (b, i, 0))] + p_in,
        out_specs=p_out,
        out_shape=p_shape(B),
        scratch_shapes=p_scratch,
        compiler_params=_params(),
        name="attn_proj",
    )(x, norm_w.reshape(1, D), w_in, pos_rows, inv_freq)


def _attn_core_kernel(qt_ref, kp_ref, kc_ref, kn_ref, vp_ref, vc_ref, vn_ref, sink_ref, wi_ref, wo_ref,
                      o_ref, wi_out, wo_out, k_all, v_all, *, nb, nblk, cast_rep):
    L = LANES
    hd = ATTN_HEAD_DIM
    G = ATTN_GROUP
    c = pl.program_id(1)
    _ffn_cast(pl.program_id(0) * pl.num_programs(1) + c, cast_rep, (wi_ref, wo_ref), (wi_out, wo_out))
    k_all[0:L] = kp_ref[0]
    k_all[L:(nblk + 1) * L] = kc_ref[0]
    k_all[(nblk + 1) * L:] = kn_ref[0]
    v_all[0] = vp_ref[0, 0]
    v_all[1:nblk + 1] = vc_ref[0]
    v_all[nblk + 1] = vn_ref[0, 0]

    key = lax.broadcasted_iota(jnp.int32, (L, G * L), 0)
    qry = lax.broadcasted_iota(jnp.int32, (L, G * L), 1) % L
    prev_mask = jnp.where(key >= qry, 0.0, NEG_INF)
    next_mask = jnp.where(key <= qry, 0.0, NEG_INF)
    zeros = jnp.zeros((hd, G * L), BF16)
    ones = jnp.ones((16, 3 * L), BF16)

    def scores(j, g):
        qg = jnp.concatenate([qt_ref[0, j, (G * g + i) * hd:(G * g + i + 1) * hd, :] for i in range(G)],
                             axis=1)
        qz = jnp.concatenate([qg, zeros] if g % 2 == 0 else [zeros, qg], axis=0)
        return _dot(k_all[j * L:(j + 3) * L, (g // 2) * L:(g // 2 + 1) * L], qz)

    items = [(j, g) for j in range(nblk) for g in range(ATTN_KV_HEADS)]
    ahead = 2
    pending = [scores(*it) for it in items[:ahead]]
    for n, (j, g) in enumerate(items):
        s = pending.pop(0)
        if n + ahead < len(items):
            pending.append(scores(*items[n + ahead]))
        if g == 0:
            blk = c * nblk + j
            prev_bias = prev_mask + jnp.where(blk > 0, 0.0, NEG_INF)
            next_bias = next_mask + jnp.where(blk < nb - 1, 0.0, NEG_INF)
        s = [s[0:L] + prev_bias, s[L:2 * L], s[2 * L:] + next_bias]
        sink = sink_ref[g:g + 1, :]
        m = sink
        for sj in s:
            m = jnp.maximum(m, jnp.max(sj, axis=0, keepdims=True))
        p = jnp.concatenate([jnp.exp2(sj - m).astype(BF16) for sj in s], axis=0)
        vg = jnp.concatenate([v_all[j + i, g * hd:(g + 1) * hd, :] for i in range(3)], axis=1)
        acc = _dot(jnp.concatenate([vg, ones], axis=0), p)
        denom = acc[hd:hd + 1] + jnp.exp2(sink - m)
        out = acc[:hd] * (1.0 / denom)
        for pair in range(G // 2):
            tile = jnp.concatenate([out[:, (2 * pair) * L:(2 * pair + 1) * L],
                                    out[:, (2 * pair + 1) * L:(2 * pair + 2) * L]], axis=0)
            col = (G * g + 2 * pair) * hd
            o_ref[0, j * L:(j + 1) * L, col:col + L] = tile.T.astype(BF16)


def _attn_core(qt, k, vt, sink_rows, ffn_w_in, ffn_w_out, layer, *, nblk=8):
    B, nb, nq, L = qt.shape
    nk = k.shape[-1]
    S = nb * L
    nblk = min(nblk, nb)
    steps = nb // nblk
    prev = lambda b, c: (b, jnp.maximum(c * nblk - 1, 0), 0)
    cur = lambda b, c: (b, c, 0)
    nxt = lambda b, c: (b, jnp.minimum((c + 1) * nblk, nb - 1), 0)
    four = lambda f: (lambda b, c: f(b, c) + (0,))
    rep, w_args, w_in_specs, w_out_specs, w_out_shape = _ffn_cast_specs(
        ffn_w_in, ffn_w_out, layer, lambda b, c: b * steps + c, B * steps)
    return pl.pallas_call(
        functools.partial(_attn_core_kernel, nb=nb, nblk=nblk, cast_rep=rep),
        grid=(B, steps),
        in_specs=[pl.BlockSpec((1, nblk, nq, L), four(cur)),
                  pl.BlockSpec((1, L, nk), prev), pl.BlockSpec((1, nblk * L, nk), cur),
                  pl.BlockSpec((1, L, nk), nxt),
                  pl.BlockSpec((1, 1, nk, L), four(prev)), pl.BlockSpec((1, nblk, nk, L), four(cur)),
                  pl.BlockSpec((1, 1, nk, L), four(nxt)),
                  _resident(sink_rows.shape)] + w_in_specs,
        out_specs=[pl.BlockSpec((1, nblk * L, nq), cur)] + w_out_specs,
        out_shape=[jax.ShapeDtypeStruct((B, S, nq), BF16)] + w_out_shape,
        scratch_shapes=[pltpu.VMEM(((nblk + 2) * L, nk), BF16),
                        pltpu.VMEM((nblk + 2, nk, L), BF16)],
        compiler_params=_params(),
        name="attn_core",
    )(qt, k, k, k, vt, vt, vt, sink_rows, *w_args)


def _mlstm_mixer(x, norm_w, w_t, layer, b_gate, ffn_w_in, ffn_w_out, ffn_layer):
    bias_rows = jnp.broadcast_to(b_gate.astype(F32).reshape(-1, 1), (b_gate.shape[0], LANES))
    q, kt, v, og, rf = _mlstm_proj(x, norm_w, w_t, layer, bias_rows)
    hs, wi, wo = _mlstm_core(q, kt, v, rf, ffn_w_in, ffn_w_out, ffn_layer)
    return (hs, og), wi, wo


def _attn_mixer(x, pos_rows, inv_freq, norm_w, w_in, layer, sink, ffn_w_in, ffn_w_out, ffn_layer):
    G, L = ATTN_GROUP, LANES
    qt, k, vt = _attn_proj(x, norm_w, w_in, layer, pos_rows, inv_freq)
    sink_rows = jnp.repeat(LOG2E * sink.astype(F32).reshape(ATTN_KV_HEADS, G), L, axis=1)
    a, wi, wo = _attn_core(qt, k, vt, sink_rows, ffn_w_in, ffn_w_out, ffn_layer)
    return (a,), wi, wo


def kernel(x, positions, norm_mix_w, norm_ffn_w, norm_final_w, mlstm_w_in, mlstm_b_gate, mlstm_norm_w, mlstm_w_out, attn_w_in, attn_sink, attn_w_out, ffn_w_in, ffn_w_out):
    depth = norm_mix_w.shape[0]
    B, S = positions.shape
    half = ATTN_HEAD_DIM // 2
    inv_freq = (ROPE_THETA ** (-jnp.arange(half, dtype=F32) / half)).reshape(half, 1)
    pos_rows = positions.reshape(B, 1, S)
    mlstm_wo, attn_wo = mlstm_w_out.astype(BF16), attn_w_out.astype(BF16)
    mlstm_wt = jnp.swapaxes(mlstm_w_in, 1, 2)
    for i in range(depth):
        j = i // 2
        final_w = norm_final_w if i == depth - 1 else None
        if i % 2 == 0:
            (hs, og), wi, wo = _mlstm_mixer(x, norm_mix_w[i], mlstm_wt, j, mlstm_b_gate[j],
                                            ffn_w_in, ffn_w_out, i)
            x = _layer_tail("mlstm", (hs, og, mlstm_norm_w[j]), x, mlstm_wo, j, norm_ffn_w[i], wi, wo,
                            final_w)
        else:
            mixed, wi, wo = _attn_mixer(x, pos_rows, inv_freq, norm_mix_w[i], attn_w_in, j, attn_sink[j],
                                        ffn_w_in, ffn_w_out, i)
            x = _layer_tail("attn", mixed, x, attn_wo, j, norm_ffn_w[i], wi, wo, final_w)
    return x
```

```python
import functools

import jax
import jax.numpy as jnp
from jax import lax
from jax.experimental import pallas as pl
from jax.experimental.pallas import tpu as pltpu

F32 = jnp.float32
BF16 = jnp.bfloat16

EPS = 1e-6
LANES = 128

MLSTM_HEADS = 4
MLSTM_CHUNK = 128
ATTN_HEAD_DIM = 64
ATTN_Q_HEADS = 16
ATTN_KV_HEADS = 4
ATTN_GROUP = ATTN_Q_HEADS // ATTN_KV_HEADS
ATTN_BLOCK = 128
ROPE_THETA = 10000.0

NEG_INF = float("-inf")
LOG2E = 1.4426950408889634
VMEM_LIMIT = 56 * 1024 * 1024


def _params():
    return pltpu.CompilerParams(dimension_semantics=("arbitrary", "arbitrary"),
                                vmem_limit_bytes=VMEM_LIMIT)


def _resident(shape):
    return pl.BlockSpec(shape, lambda *_: (0,) * len(shape), pipeline_mode=pl.Buffered(1))


def _rmsnorm(x, w):
    ms = jnp.mean(x * x, axis=-1, keepdims=True)
    return x * lax.rsqrt(ms + EPS) * w


def _sigmoid(x):
    return 1.0 / (1.0 + jnp.exp(-x))


def _dot(a, b):
    return jnp.dot(a, b, preferred_element_type=F32)


def _dot_nt(a, b):
    return lax.dot_general(a, b, (((1,), (1,)), ((), ())), preferred_element_type=F32)


def _layer_tail_kernel(*refs, d_ff, tf, nsub, mixer, final):
    it = iter(refs)
    if mixer == "mlstm":
        hs_ref, og_ref, hw_ref = next(it), next(it), next(it)
    else:
        a_ref = next(it)
    x_ref, wo_ref, nw_ref, win_ref, wout_ref = (next(it) for _ in range(5))
    fw_ref = next(it) if final else None
    o_ref, x1_ref, hn_ref, act_ref = (next(it) for _ in range(4))

    tm = x_ref.shape[1]
    subs = [slice(r * (tm // nsub), (r + 1) * (tm // nsub)) for r in range(nsub)]
    for rs in subs:
        if mixer == "mlstm":
            dv = hs_ref.shape[-1] // MLSTM_HEADS
            parts = []
            for h in range(MLSTM_HEADS):
                hs = hs_ref[0, rs, h * dv:(h + 1) * dv]
                ms = jnp.mean(hs * hs, axis=-1, keepdims=True)
                parts.append(hs * lax.rsqrt(ms + EPS))
            y = (jnp.concatenate(parts, axis=-1) * hw_ref[...] * og_ref[0, rs, :].astype(F32)).astype(BF16)
        else:
            y = a_ref[0, rs, :]
        x1_ref[rs, :] = x_ref[0, rs, :] + _dot(y, wo_ref[...])
    for rs in subs:
        hn_ref[rs, :] = _rmsnorm(x1_ref[rs, :], nw_ref[...]).astype(BF16)
    for rs in subs:
        for j in range(d_ff // tf):
            hn = hn_ref[rs, :]
            g = _dot(hn, win_ref[:, j * tf:(j + 1) * tf])
            u = _dot(hn, win_ref[:, d_ff + j * tf:d_ff + (j + 1) * tf])
            act_ref[rs, j * tf:(j + 1) * tf] = (g * _sigmoid(g) * u).astype(BF16)
    for rs in subs:
        out = x1_ref[rs, :] + _dot(act_ref[rs, :], wout_ref[...])
        if final:
            out = _rmsnorm(out, fw_ref[...])
        o_ref[0, rs, :] = out


def _layer_tail(mixer, mixer_args, x, w_o, layer_o, norm_w, w_in, w_out, final_w=None,
                *, tm=512, tf=256, nsub=2):
    B, S, D = x.shape
    tm = min(tm, S)
    d_ff = w_out.shape[0]
    final = final_w is not None
    row = pl.BlockSpec((1, tm, D), lambda b, i: (b, i, 0))

    def stacked(shape, l):
        return pl.BlockSpec((None,) + shape, lambda b, i: (l,) + (0,) * len(shape),
                            pipeline_mode=pl.Buffered(1))

    if mixer == "mlstm":
        hs, og, head_w = mixer_args
        args = [hs, og, head_w.reshape(1, D)]
        in_specs = [row, row, _resident((1, D))]
    else:
        args = list(mixer_args)
        in_specs = [row]
    args += [x, w_o, norm_w.reshape(1, D), w_in, w_out]
    in_specs += [row, stacked((w_o.shape[1], D), layer_o), _resident((1, D)),
                 _resident((D, 2 * d_ff)), _resident((d_ff, D))]
    if final:
        args.append(final_w.reshape(1, D))
        in_specs.append(_resident((1, D)))
    return pl.pallas_call(
        functools.partial(_layer_tail_kernel, d_ff=d_ff, tf=tf, nsub=nsub, mixer=mixer, final=final),
        grid=(B, S // tm),
        in_specs=in_specs,
        out_specs=row,
        out_shape=jax.ShapeDtypeStruct((B, S, D), F32),
        scratch_shapes=[pltpu.VMEM((tm, D), F32), pltpu.VMEM((tm, D), BF16),
                        pltpu.VMEM((tm, d_ff), BF16)],
        compiler_params=_params(),
        name=mixer + "_tail",
    )(*args)


def _mlstm_proj_kernel(x_ref, nw_ref, wt_ref, bias_ref,
                       q_ref, kt_ref, v_ref, og_ref, rf_ref, w_ref, wkt_ref, *, dk, tm, nsub):
    d = v_ref.shape[-1]
    hdk = q_ref.shape[-1]
    ng = rf_ref.shape[1] // (tm // LANES)
    sub = tm // nsub

    @pl.when((pl.program_id(0) == 0) & (pl.program_id(1) == 0))
    def _():
        for c in range(0, hdk, LANES):
            w_ref[:, c:c + LANES] = wt_ref[c:c + LANES, :].T.astype(BF16)
            wkt_ref[c:c + LANES, :] = wt_ref[hdk + c:hdk + c + LANES, :].astype(BF16)
        for c in range(0, 2 * d, LANES):
            w_ref[:, hdk + c:hdk + c + LANES] = wt_ref[2 * hdk + c:2 * hdk + c + LANES, :].T.astype(BF16)
        wkt_ref[hdk:, :] = wt_ref[2 * hdk + 2 * d:, :].astype(BF16)

    for r in range(nsub):
        rs = slice(r * sub, (r + 1) * sub)
        hn = _rmsnorm(x_ref[0, rs, :], nw_ref[...]).astype(BF16)
        og_ref[0, rs, :] = _sigmoid(_dot(hn, w_ref[:, hdk + d:])).astype(BF16)
        v_ref[0, rs, :] = _dot(hn, w_ref[:, hdk:hdk + d]).astype(BF16)
        q_ref[0, rs, :] = (_dot(hn, w_ref[:, :hdk]) * (dk ** -0.5)).astype(BF16)
        kg = _dot_nt(wkt_ref[...], hn)
        kt = kg[:hdk].astype(BF16)
        for jj in range(sub // LANES):
            j = r * (sub // LANES) + jj
            kt_ref[0, j] = kt[:, jj * LANES:(jj + 1) * LANES]
            rf_ref[0, j * ng:(j + 1) * ng, :] = kg[hdk:, jj * LANES:(jj + 1) * LANES] + bias_ref[...]


def _mlstm_proj(x, norm_w, w_t, layer, bias_rows, *, tm=1024, nsub=4):
    B, S, D = x.shape
    H = MLSTM_HEADS
    hdk = D // 2
    dk = hdk // H
    tm = min(tm, S)
    ng = bias_rows.shape[0]
    nj = tm // LANES
    NC = S // LANES
    row = lambda b, i: (b, i, 0)
    return pl.pallas_call(
        functools.partial(_mlstm_proj_kernel, dk=dk, tm=tm, nsub=nsub),
        grid=(B, S // tm),
        in_specs=[pl.BlockSpec((1, tm, D), row), _resident((1, D)),
                  pl.BlockSpec((None,) + w_t.shape[1:], lambda b, i: (layer, 0, 0),
                               pipeline_mode=pl.Buffered(1)),
                  _resident((ng, LANES))],
        out_specs=[pl.BlockSpec((1, tm, hdk), row),
                   pl.BlockSpec((1, nj, hdk, LANES), lambda b, i: (b, i, 0, 0)),
                   pl.BlockSpec((1, tm, D), row),
                   pl.BlockSpec((1, tm, D), row),
                   pl.BlockSpec((1, nj * ng, LANES), row)],
        out_shape=[jax.ShapeDtypeStruct((B, S, hdk), BF16),
                   jax.ShapeDtypeStruct((B, NC, hdk, LANES), BF16),
                   jax.ShapeDtypeStruct((B, S, D), BF16),
                   jax.ShapeDtypeStruct((B, S, D), BF16),
                   jax.ShapeDtypeStruct((B, NC * ng, LANES), F32)],
        scratch_shapes=[pltpu.VMEM((D, hdk + 2 * D), BF16), pltpu.VMEM((hdk + ng, D), BF16)],
        compiler_params=_params(),
        name="mlstm_proj",
    )(x, norm_w.reshape(1, D), w_t, bias_rows)


FFN_CAST_BLOCKS = 8


def _ffn_cast_specs(ffn_w_in, ffn_w_out, layer, step_of, total_steps):
    nblk = min(FFN_CAST_BLOCKS, total_steps)
    rep = total_steps // nblk
    args, in_specs, out_specs, out_shape = [], [], [], []
    for w in (ffn_w_in, ffn_w_out):
        rows, cols = w.shape[1] // nblk, w.shape[2]
        args.append(w)
        in_specs.append(pl.BlockSpec((None, rows, cols),
                                     lambda *g, l=layer: (l, step_of(*g) // rep, 0)))
        out_specs.append(pl.BlockSpec((rows, cols), lambda *g: (step_of(*g) // rep, 0)))
        out_shape.append(jax.ShapeDtypeStruct(w.shape[1:], BF16))
    return rep, args, in_specs, out_specs, out_shape


def _ffn_cast(step, rep, srcs, dsts):
    @pl.when(step % rep == 0)
    def _():
        for src, dst in zip(srcs, dsts):
            dst[...] = src[...].astype(BF16)


def _log_sigmoid(x):
    return jnp.minimum(x, 0.0) - jnp.log1p(jnp.exp(-jnp.abs(x)))


def _mlstm_core_kernel(q_ref, kt_ref, v_ref, rf_ref, wi_ref, wo_ref, o_ref, wi_out, wo_out, st_ref, c_ref,
                       *, nc, dv, unroll, cast_rep):
    L = LANES
    h = pl.program_id(1)
    ng = 4 * MLSTM_HEADS
    _ffn_cast(pl.program_id(0) * pl.num_programs(1) + h, cast_rep, (wi_ref, wo_ref), (wi_out, wo_out))

    def gate_rows(g):
        return rf_ref[0, pl.ds(MLSTM_HEADS * g + h, nc, stride=ng), :]

    lane = lax.broadcasted_iota(jnp.int32, (nc, L), 1)

    def prefix_sum(x):
        for s in (1, 2, 4, 8, 16, 32, 64):
            x = x + jnp.where(lane >= s, pltpu.roll(x, s, axis=1), 0.0)
        return x

    def suffix_sum(x):
        for s in (1, 2, 4, 8, 16, 32, 64):
            x = x + jnp.where(lane < L - s, pltpu.roll(x, L - s, axis=1), 0.0)
        return x

    for d in range(2):
        log_i = gate_rows(2 * d)
        log_f = _log_sigmoid(gate_rows(2 * d + 1))
        if d == 0:
            b = prefix_sum(log_f)
            b_last = jnp.broadcast_to(b[:, L - 1:L], (nc, L))
        else:
            b = suffix_sum(log_f)
            b_last = jnp.broadcast_to(b[:, 0:1], (nc, L))
        r = log_i - b
        a = b_last + r
        a_max = jnp.broadcast_to(jnp.max(a, axis=1, keepdims=True), (nc, L))
        st_ref[d, 0] = log_f
        st_ref[d, 1] = r
        st_ref[d, 2] = jnp.exp(a - a_max)
        st_ref[d, 3] = b_last
        st_ref[d, 4] = a_max

    def m_scan(i, carry):
        new = []
        for d, c in ((0, i), (1, nc - 1 - i)):
            row = pl.ds(c, 1)
            st_ref[d, 5, row, :] = carry[d]
            new.append(jnp.maximum(st_ref[d, 3, row, :] + carry[d], st_ref[d, 4, row, :]))
        return tuple(new)

    m0 = jnp.zeros((1, L), F32)
    lax.fori_loop(0, nc, m_scan, (m0, m0))
    for d in range(2):
        b_last, a_max, m_prev = st_ref[d, 3], st_ref[d, 4], st_ref[d, 5]
        m_new = jnp.maximum(b_last + m_prev, a_max)
        st_ref[d, 3] = jnp.exp(b_last + m_prev - m_new)
        st_ref[d, 2] = st_ref[d, 2] * jnp.exp(a_max - m_new)
        st_ref[d, 0] = st_ref[d, 0] * LOG2E
        st_ref[d, 1] = st_ref[d, 1] * LOG2E
        st_ref[d, 5] = m_prev * LOG2E

    c_ref[...] = jnp.zeros_like(c_ref)

    t_idx = lax.broadcasted_iota(jnp.int32, (L, L), 0)
    s_idx = lax.broadcasted_iota(jnp.int32, (L, L), 1)
    masks = (s_idx <= t_idx, s_idx >= t_idx)
    ones = jnp.ones((L, L), BF16)

    def tile3(x):
        return jnp.concatenate([x] * (dv // L + 1), axis=1)

    def body(i, carry, assign):
        items = []
        for u in range(unroll):
            c = i * unroll + u
            items += [(0, c), (1, nc - 1 - c)]

        work = []
        for d, c in items:
            rows = pl.ds(pl.multiple_of(c * L, L), L)
            qc = q_ref[0, rows, :]
            kt = kt_ref[0, c]
            vaug = jnp.concatenate([v_ref[0, rows, :], ones], axis=1)
            w = st_ref[d, 2, pl.ds(c, 1), :]
            scores = _dot(qc, kt)
            kv = _dot(kt * w.astype(BF16), vaug)
            work.append((rows, qc, vaug, scores, kv))

        states = [c_ref[0], c_ref[1]]
        prev_states = []
        for (d, c), (rows, qc, vaug, scores, kv) in zip(items, work):
            prev_states.append(states[d].astype(BF16))
            states[d] = tile3(st_ref[d, 3, pl.ds(c, 1), :]) * states[d] + kv
        c_ref[0] = states[0]
        c_ref[1] = states[1]

        for (d, c), (rows, qc, vaug, scores, kv), prev_state in zip(items, work, prev_states):
            log_f = st_ref[d, 0, pl.ds(c, 1), :]
            r = st_ref[d, 1, pl.ds(c, 1), :]
            m_prev = st_ref[d, 5, pl.ds(c, 1), :]
            mask = masks[d]
            r_masked = jnp.where(mask, r, NEG_INF)
            cm = jnp.max(r_masked, axis=1, keepdims=True)
            b_col = jnp.sum(jnp.where(mask, log_f, 0.0), axis=1, keepdims=True)
            mu = jnp.broadcast_to(jnp.maximum(m_prev[:, 0:1], cm), (L, L))
            decay = jnp.exp2(r_masked - mu)
            inter = jnp.exp2(m_prev - mu)
            lhs = jnp.concatenate([(scores * decay).astype(BF16), inter.astype(BF16) * qc], axis=1)
            both = _dot(lhs, jnp.concatenate([vaug, prev_state], axis=0))
            den = jnp.maximum(jnp.abs(both[:, dv:]), jnp.exp2(-(b_col + mu)))
            out = both[:, :dv] * tile3(1.0 / den)[:, :dv]
            if assign:
                o_ref[0, rows, :] = out
            else:
                o_ref[0, rows, :] += out
        return carry

    steps = nc // 2 // unroll
    lax.fori_loop(0, steps, functools.partial(body, assign=True), 0)
    lax.fori_loop(steps, 2 * steps, functools.partial(body, assign=False), 0)


def _mlstm_core(q, kt, v, rf, ffn_w_in, ffn_w_out, layer, *, unroll=4):
    B, S, hdk = q.shape
    H = MLSTM_HEADS
    dk = hdk // H
    dv = v.shape[-1] // H
    nc = S // LANES
    rep, w_args, w_in_specs, w_out_specs, w_out_shape = _ffn_cast_specs(
        ffn_w_in, ffn_w_out, layer, lambda b, h: b * H + h, B * H)
    return pl.pallas_call(
        functools.partial(_mlstm_core_kernel, nc=nc, dv=dv, unroll=min(unroll, nc // 2), cast_rep=rep),
        grid=(B, H),
        in_specs=[pl.BlockSpec((1, S, dk), lambda b, h: (b, 0, h)),
                  pl.BlockSpec((1, nc, dk, LANES), lambda b, h: (b, 0, h, 0)),
                  pl.BlockSpec((1, S, dv), lambda b, h: (b, 0, h)),
                  pl.BlockSpec((1, rf.shape[1], LANES), lambda b, h: (b, 0, 0))] + w_in_specs,
        out_specs=[pl.BlockSpec((1, S, dv), lambda b, h: (b, 0, h))] + w_out_specs,
        out_shape=[jax.ShapeDtypeStruct((B, S, H * dv), F32)] + w_out_shape,
        scratch_shapes=[pltpu.VMEM((2, 6, nc, LANES), F32),
                        pltpu.VMEM((2, dk, dv + LANES), F32)],
        compiler_params=_params(),
        name="mlstm_core",
    )(q, kt, v, rf, *w_args)


def _attn_weight_prep(w_ref, wt_ref):
    @pl.when((pl.program_id(0) == 0) & (pl.program_id(1) == 0))
    def _():
        for c in range(0, wt_ref.shape[0], LANES):
            wt_ref[c:c + LANES, :] = w_ref[:, c:c + LANES].T.astype(BF16)


def _attn_proj_rows(x_rows, pos_row, invf, nw, wt_ref, qt_ref, k_ref, vt_ref, chunk0):
    hd = ATTN_HEAD_DIM
    half = hd // 2
    nq = ATTN_Q_HEADS * hd
    nk = ATTN_KV_HEADS * hd
    scale = LOG2E * hd ** -0.5
    piece = 8 * hd
    sub = x_rows.shape[0]
    hn = _rmsnorm(x_rows, nw).astype(BF16)
    ang = invf * pos_row.astype(F32)
    cos = jnp.cos(ang)
    sin = jnp.sin(ang)
    chunks = [(chunk0 + jj, slice(jj * LANES, (jj + 1) * LANES)) for jj in range(sub // LANES)]

    def rope(xh):
        x1, x2 = xh[:half], xh[half:]
        return jnp.concatenate([x1 * cos - x2 * sin, x2 * cos + x1 * sin], axis=0)

    qp = _dot_nt(wt_ref[:nq, :], hn)
    kv = _dot_nt(wt_ref[nq:, :], hn)
    for lo in range(0, nq, piece):
        qt = jnp.concatenate([rope(qp[lo + h * hd:lo + (h + 1) * hd]) * scale for h in range(piece // hd)],
                             axis=0).astype(BF16)
        for j, cols in chunks:
            qt_ref[0, j, lo:lo + piece, :] = qt[:, cols]
    vt = kv[nk:].astype(BF16)
    for j, cols in chunks:
        vt_ref[0, j] = vt[:, cols]
    kt = jnp.concatenate([rope(kv[g * hd:(g + 1) * hd]) for g in range(ATTN_KV_HEADS)], axis=0)
    for j, cols in chunks:
        for f in range(nk // LANES):
            k_ref[0, j * LANES:(j + 1) * LANES, f * LANES:(f + 1) * LANES] = (
                kt[f * LANES:(f + 1) * LANES, cols].T.astype(BF16))


def _attn_proj_kernel(x_ref, nw_ref, w_ref, pos_ref, invf_ref, qt_ref, k_ref, vt_ref, wt_ref, *, tm, nsub):
    _attn_weight_prep(w_ref, wt_ref)
    sub = tm // nsub
    for r in range(nsub):
        rs = slice(r * sub, (r + 1) * sub)
        _attn_proj_rows(x_ref[0, rs, :], pos_ref[0, :, rs], invf_ref[...], nw_ref[...], wt_ref,
                        qt_ref, k_ref, vt_ref, r * (sub // LANES))


def _attn_proj_specs(S, D, tm, w_in, layer):
    nq = ATTN_Q_HEADS * ATTN_HEAD_DIM
    nk = ATTN_KV_HEADS * ATTN_HEAD_DIM
    nj = tm // LANES
    nb = S // LANES
    in_specs = [_resident((1, D)),
                pl.BlockSpec((None,) + w_in.shape[1:], lambda b, i: (layer, 0, 0),
                             pipeline_mode=pl.Buffered(1)),
                pl.BlockSpec((1, 1, tm), lambda b, i: (b, 0, i)),
                _resident((ATTN_HEAD_DIM // 2, 1))]
    out_specs = [pl.BlockSpec((1, nj, nq, LANES), lambda b, i: (b, i, 0, 0)),
                 pl.BlockSpec((1, tm, nk), lambda b, i: (b, i, 0)),
                 pl.BlockSpec((1, nj, nk, LANES), lambda b, i: (b, i, 0, 0))]
    out_shape = lambda B: [jax.ShapeDtypeStruct((B, nb, nq, LANES), BF16),
                           jax.ShapeDtypeStruct((B, S, nk), BF16),
                           jax.ShapeDtypeStruct((B, nb, nk, LANES), BF16)]
    scratch = [pltpu.VMEM((nq + 2 * nk, D), BF16)]
    return in_specs, out_specs, out_shape, scratch


def _attn_proj(x, norm_w, w_in, layer, pos_rows, inv_freq, *, tm=1024, nsub=4):
    B, S, D = x.shape
    tm = min(tm, S)
    p_in, p_out, p_shape, p_scratch = _attn_proj_specs(S, D, tm, w_in, layer)
    return pl.pallas_call(
        functools.partial(_attn_proj_kernel, tm=tm, nsub=nsub),
        grid=(B, S // tm),
        in_specs=[pl.BlockSpec((1, tm, D), lambda b, i: (b, i, 0))] + p_in,
        out_specs=p_out,
        out_shape=p_shape(B),
        scratch_shapes=p_scratch,
        compiler_params=_params(),
        name="attn_proj",
    )(x, norm_w.reshape(1, D), w_in, pos_rows, inv_freq)


def _attn_core_kernel(qt_ref, kp_ref, kc_ref, kn_ref, vp_ref, vc_ref, vn_ref, sink_ref, wi_ref, wo_ref,
                      o_ref, wi_out, wo_out, k_all, v_all, *, nb, nblk, cast_rep):
    L = LANES
    hd = ATTN_HEAD_DIM
    G = ATTN_GROUP
    c = pl.program_id(1)
    _ffn_cast(pl.program_id(0) * pl.num_programs(1) + c, cast_rep, (wi_ref, wo_ref), (wi_out, wo_out))
    k_all[0:L] = kp_ref[0]
    k_all[L:(nblk + 1) * L] = kc_ref[0]
    k_all[(nblk + 1) * L:] = kn_ref[0]
    v_all[0] = vp_ref[0, 0]
    v_all[1:nblk + 1] = vc_ref[0]
    v_all[nblk + 1] = vn_ref[0, 0]

    key = lax.broadcasted_iota(jnp.int32, (L, G * L), 0)
    qry = lax.broadcasted_iota(jnp.int32, (L, G * L), 1) % L
    prev_mask = jnp.where(key >= qry, 0.0, NEG_INF)
    next_mask = jnp.where(key <= qry, 0.0, NEG_INF)
    zeros = jnp.zeros((hd, G * L), BF16)
    ones = jnp.ones((16, 3 * L), BF16)

    def scores(j, g):
        qg = jnp.concatenate([qt_ref[0, j, (G * g + i) * hd:(G * g + i + 1) * hd, :] for i in range(G)],
                             axis=1)
        qz = jnp.concatenate([qg, zeros] if g % 2 == 0 else [zeros, qg], axis=0)
        return _dot(k_all[j * L:(j + 3) * L, (g // 2) * L:(g // 2 + 1) * L], qz)

    items = [(j, g) for j in range(nblk) for g in range(ATTN_KV_HEADS)]
    ahead = 2
    pending = [scores(*it) for it in items[:ahead]]
    for n, (j, g) in enumerate(items):
        s = pending.pop(0)
        if n + ahead < len(items):
            pending.append(scores(*items[n + ahead]))
        if g == 0:
            blk = c * nblk + j
            prev_bias = prev_mask + jnp.where(blk > 0, 0.0, NEG_INF)
            next_bias = next_mask + jnp.where(blk < nb - 1, 0.0, NEG_INF)
        s = [s[0:L] + prev_bias, s[L:2 * L], s[2 * L:] + next_bias]
        sink = sink_ref[g:g + 1, :]
        m8 = functools.reduce(jnp.maximum, [sj.reshape(L // 8, 8, G * L).max(axis=0) for sj in s])
        m = jnp.maximum(sink, jnp.max(m8, axis=0, keepdims=True))
        p = jnp.concatenate([jnp.exp2(sj - m).astype(BF16) for sj in s], axis=0)
        vg = jnp.concatenate([v_all[j + i, g * hd:(g + 1) * hd, :] for i in range(3)], axis=1)
        acc = _dot(jnp.concatenate([vg, ones], axis=0), p)
        denom = acc[hd:hd + 1] + jnp.exp2(sink - m)
        out = acc[:hd] * (1.0 / denom)
        for pair in range(G // 2):
            tile = jnp.concatenate([out[:, (2 * pair) * L:(2 * pair + 1) * L],
                                    out[:, (2 * pair + 1) * L:(2 * pair + 2) * L]], axis=0)
            col = (G * g + 2 * pair) * hd
            o_ref[0, j * L:(j + 1) * L, col:col + L] = tile.T.astype(BF16)


def _attn_core(qt, k, vt, sink_rows, ffn_w_in, ffn_w_out, layer, *, nblk=8):
    B, nb, nq, L = qt.shape
    nk = k.shape[-1]
    S = nb * L
    nblk = min(nblk, nb)
    steps = nb // nblk
    prev = lambda b, c: (b, jnp.maximum(c * nblk - 1, 0), 0)
    cur = lambda b, c: (b, c, 0)
    nxt = lambda b, c: (b, jnp.minimum((c + 1) * nblk, nb - 1), 0)
    four = lambda f: (lambda b, c: f(b, c) + (0,))
    rep, w_args, w_in_specs, w_out_specs, w_out_shape = _ffn_cast_specs(
        ffn_w_in, ffn_w_out, layer, lambda b, c: b * steps + c, B * steps)
    return pl.pallas_call(
        functools.partial(_attn_core_kernel, nb=nb, nblk=nblk, cast_rep=rep),
        grid=(B, steps),
        in_specs=[pl.BlockSpec((1, nblk, nq, L), four(cur)),
                  pl.BlockSpec((1, L, nk), prev), pl.BlockSpec((1, nblk * L, nk), cur),
                  pl.BlockSpec((1, L, nk), nxt),
                  pl.BlockSpec((1, 1, nk, L), four(prev)), pl.BlockSpec((1, nblk, nk, L), four(cur)),
                  pl.BlockSpec((1, 1, nk, L), four(nxt)),
                  _resident(sink_rows.shape)] + w_in_specs,
        out_specs=[pl.BlockSpec((1, nblk * L, nq), cur)] + w_out_specs,
        out_shape=[jax.ShapeDtypeStruct((B, S, nq), BF16)] + w_out_shape,
        scratch_shapes=[pltpu.VMEM(((nblk + 2) * L, nk), BF16),
                        pltpu.VMEM((nblk + 2, nk, L), BF16)],
        compiler_params=_params(),
        name="attn_core",
    )(qt, k, k, k, vt, vt, vt, sink_rows, *w_args)


def _mlstm_mixer(x, norm_w, w_t, layer, b_gate, ffn_w_in, ffn_w_out, ffn_layer):
    bias_rows = jnp.broadcast_to(b_gate.astype(F32).reshape(-1, 1), (b_gate.shape[0], LANES))
    q, kt, v, og, rf = _mlstm_proj(x, norm_w, w_t, layer, bias_rows)
    hs, wi, wo = _mlstm_core(q, kt, v, rf, ffn_w_in, ffn_w_out, ffn_layer)
    return (hs, og), wi, wo


def _attn_mixer(x, pos_rows, inv_freq, norm_w, w_in, layer, sink, ffn_w_in, ffn_w_out, ffn_layer):
    G, L = ATTN_GROUP, LANES
    qt, k, vt = _attn_proj(x, norm_w, w_in, layer, pos_rows, inv_freq)
    sink_rows = jnp.repeat(LOG2E * sink.astype(F32).reshape(ATTN_KV_HEADS, G), L, axis=1)
    a, wi, wo = _attn_core(qt, k, vt, sink_rows, ffn_w_in, ffn_w_out, ffn_layer)
    return (a,), wi, wo


def kernel(x, positions, norm_mix_w, norm_ffn_w, norm_final_w, mlstm_w_in, mlstm_b_gate, mlstm_norm_w, mlstm_w_out, attn_w_in, attn_sink, attn_w_out, ffn_w_in, ffn_w_out):
    depth = norm_mix_w.shape[0]
    B, S = positions.shape
    half = ATTN_HEAD_DIM // 2
    inv_freq = (ROPE_THETA ** (-jnp.arange(half, dtype=F32) / half)).reshape(half, 1)
    pos_rows = positions.reshape(B, 1, S)
    mlstm_wo, attn_wo = mlstm_w_out.astype(BF16), attn_w_out.astype(BF16)
    mlstm_wt = jnp.swapaxes(mlstm_w_in, 1, 2)
    for i in range(depth):
        j = i // 2
        final_w = norm_final_w if i == depth - 1 else None
        if i % 2 == 0:
            (hs, og), wi, wo = _mlstm_mixer(x, norm_mix_w[i], mlstm_wt, j, mlstm_b_gate[j],
                                            ffn_w_in, ffn_w_out, i)
            x = _layer_tail("mlstm", (hs, og, mlstm_norm_w[j]), x, mlstm_wo, j, norm_ffn_w[i], wi, wo,
                            final_w)
        else:
            mixed, wi, wo = _attn_mixer(x, pos_rows, inv_freq, norm_mix_w[i], attn_w_in, j, attn_sink[j],
                                        ffn_w_in, ffn_w_out, i)
            x = _layer_tail("attn", mixed, x, attn_wo, j, norm_ffn_w[i], wi, wo, final_w)
    return x
```

```python
import functools

import jax
import jax.numpy as jnp
from jax import lax
from jax.experimental import pallas as pl
from jax.experimental.pallas import tpu as pltpu

F32 = jnp.float32
BF16 = jnp.bfloat16

EPS = 1e-6
LANES = 128

MLSTM_HEADS = 4
MLSTM_CHUNK = 128
ATTN_HEAD_DIM = 64
ATTN_Q_HEADS = 16
ATTN_KV_HEADS = 4
ATTN_GROUP = ATTN_Q_HEADS // ATTN_KV_HEADS
ATTN_BLOCK = 128
ROPE_THETA = 10000.0

NEG_INF = float("-inf")
LOG2E = 1.4426950408889634
VMEM_LIMIT = 56 * 1024 * 1024


def _params():
    return pltpu.CompilerParams(dimension_semantics=("arbitrary", "arbitrary"),
                                vmem_limit_bytes=VMEM_LIMIT)


def _resident(shape):
    return pl.BlockSpec(shape, lambda *_: (0,) * len(shape), pipeline_mode=pl.Buffered(1))


def _rmsnorm(x, w):
    ms = jnp.mean(x * x, axis=-1, keepdims=True)
    return x * lax.rsqrt(ms + EPS) * w


def _sigmoid(x):
    return 1.0 / (1.0 + jnp.exp(-x))


def _dot(a, b):
    return jnp.dot(a, b, preferred_element_type=F32)


def _dot_nt(a, b):
    return lax.dot_general(a, b, (((1,), (1,)), ((), ())), preferred_element_type=F32)


def _layer_tail_kernel(*refs, d_ff, tf, nsub, mixer, final):
    it = iter(refs)
    if mixer == "mlstm":
        hf_ref, hb_ref, og_ref, hw_ref = next(it), next(it), next(it), next(it)
    else:
        a_ref = next(it)
    x_ref, wo_ref, nw_ref, win_ref, wout_ref = (next(it) for _ in range(5))
    fw_ref = next(it) if final else None
    o_ref, x1_ref, hn_ref, act_ref = (next(it) for _ in range(4))

    tm = x_ref.shape[1]
    subs = [slice(r * (tm // nsub), (r + 1) * (tm // nsub)) for r in range(nsub)]
    for rs in subs:
        if mixer == "mlstm":
            dv = hf_ref.shape[-1] // MLSTM_HEADS
            parts = []
            for h in range(MLSTM_HEADS):
                cols = slice(h * dv, (h + 1) * dv)
                hs = hf_ref[0, rs, cols].astype(F32) + hb_ref[0, rs, cols].astype(F32)
                ms = jnp.mean(hs * hs, axis=-1, keepdims=True)
                parts.append(hs * lax.rsqrt(ms + EPS))
            y = (jnp.concatenate(parts, axis=-1) * hw_ref[...] * og_ref[0, rs, :].astype(F32)).astype(BF16)
        else:
            y = a_ref[0, rs, :]
        x1_ref[rs, :] = x_ref[0, rs, :] + _dot(y, wo_ref[...])
    for rs in subs:
        hn_ref[rs, :] = _rmsnorm(x1_ref[rs, :], nw_ref[...]).astype(BF16)
    for rs in subs:
        for j in range(d_ff // tf):
            hn = hn_ref[rs, :]
            g = _dot(hn, win_ref[:, j * tf:(j + 1) * tf])
            u = _dot(hn, win_ref[:, d_ff + j * tf:d_ff + (j + 1) * tf])
            act_ref[rs, j * tf:(j + 1) * tf] = (g * _sigmoid(g) * u).astype(BF16)
    for rs in subs:
        out = x1_ref[rs, :] + _dot(act_ref[rs, :], wout_ref[...])
        if final:
            out = _rmsnorm(out, fw_ref[...])
        o_ref[0, rs, :] = out


def _layer_tail(mixer, mixer_args, x, w_o, layer_o, norm_w, w_in, w_out, final_w=None,
                *, tm=512, tf=256, nsub=2):
    B, S, D = x.shape
    tm = min(tm, S)
    d_ff = w_out.shape[0]
    final = final_w is not None
    row = pl.BlockSpec((1, tm, D), lambda b, i: (b, i, 0))

    def stacked(shape, l):
        return pl.BlockSpec((None,) + shape, lambda b, i: (l,) + (0,) * len(shape),
                            pipeline_mode=pl.Buffered(1))

    if mixer == "mlstm":
        h_fwd, h_bwd, og, head_w = mixer_args
        args = [h_fwd, h_bwd, og, head_w.reshape(1, D)]
        in_specs = [row, row, row, _resident((1, D))]
    else:
        args = list(mixer_args)
        in_specs = [row]
    args += [x, w_o, norm_w.reshape(1, D), w_in, w_out]
    in_specs += [row, stacked((w_o.shape[1], D), layer_o), _resident((1, D)),
                 _resident((D, 2 * d_ff)), _resident((d_ff, D))]
    if final:
        args.append(final_w.reshape(1, D))
        in_specs.append(_resident((1, D)))
    return pl.pallas_call(
        functools.partial(_layer_tail_kernel, d_ff=d_ff, tf=tf, nsub=nsub, mixer=mixer, final=final),
        grid=(B, S // tm),
        in_specs=in_specs,
        out_specs=row,
        out_shape=jax.ShapeDtypeStruct((B, S, D), F32),
        scratch_shapes=[pltpu.VMEM((tm, D), F32), pltpu.VMEM((tm, D), BF16),
                        pltpu.VMEM((tm, d_ff), BF16)],
        compiler_params=_params(),
        name=mixer + "_tail",
    )(*args)


def _mlstm_proj_kernel(x_ref, nw_ref, wt_ref, bias_ref,
                       q_ref, kt_ref, v_ref, og_ref, rf_ref, w_ref, wkt_ref, *, dk, tm, nsub):
    d = v_ref.shape[-1]
    hdk = q_ref.shape[-1]
    ng = rf_ref.shape[1] // (tm // LANES)
    sub = tm // nsub

    @pl.when((pl.program_id(0) == 0) & (pl.program_id(1) == 0))
    def _():
        for c in range(0, hdk, LANES):
            w_ref[:, c:c + LANES] = wt_ref[c:c + LANES, :].T.astype(BF16)
            wkt_ref[c:c + LANES, :] = wt_ref[hdk + c:hdk + c + LANES, :].astype(BF16)
        for c in range(0, 2 * d, LANES):
            w_ref[:, hdk + c:hdk + c + LANES] = wt_ref[2 * hdk + c:2 * hdk + c + LANES, :].T.astype(BF16)
        wkt_ref[hdk:, :] = wt_ref[2 * hdk + 2 * d:, :].astype(BF16)

    for r in range(nsub):
        rs = slice(r * sub, (r + 1) * sub)
        hn = _rmsnorm(x_ref[0, rs, :], nw_ref[...]).astype(BF16)
        og_ref[0, rs, :] = _sigmoid(_dot(hn, w_ref[:, hdk + d:])).astype(BF16)
        v_ref[0, rs, :] = _dot(hn, w_ref[:, hdk:hdk + d]).astype(BF16)
        q_ref[0, rs, :] = (_dot(hn, w_ref[:, :hdk]) * (dk ** -0.5)).astype(BF16)
        kg = _dot_nt(wkt_ref[...], hn)
        kt = kg[:hdk].astype(BF16)
        for jj in range(sub // LANES):
            j = r * (sub // LANES) + jj
            kt_ref[0, j] = kt[:, jj * LANES:(jj + 1) * LANES]
            rf_ref[0, j * ng:(j + 1) * ng, :] = kg[hdk:, jj * LANES:(jj + 1) * LANES] + bias_ref[...]


def _mlstm_proj(x, norm_w, w_t, layer, bias_rows, *, tm=1024, nsub=4):
    B, S, D = x.shape
    H = MLSTM_HEADS
    hdk = D // 2
    dk = hdk // H
    tm = min(tm, S)
    ng = bias_rows.shape[0]
    nj = tm // LANES
    NC = S // LANES
    row = lambda b, i: (b, i, 0)
    return pl.pallas_call(
        functools.partial(_mlstm_proj_kernel, dk=dk, tm=tm, nsub=nsub),
        grid=(B, S // tm),
        in_specs=[pl.BlockSpec((1, tm, D), row), _resident((1, D)),
                  pl.BlockSpec((None,) + w_t.shape[1:], lambda b, i: (layer, 0, 0),
                               pipeline_mode=pl.Buffered(1)),
                  _resident((ng, LANES))],
        out_specs=[pl.BlockSpec((1, tm, hdk), row),
                   pl.BlockSpec((1, nj, hdk, LANES), lambda b, i: (b, i, 0, 0)),
                   pl.BlockSpec((1, tm, D), row),
                   pl.BlockSpec((1, tm, D), row),
                   pl.BlockSpec((1, nj * ng, LANES), row)],
        out_shape=[jax.ShapeDtypeStruct((B, S, hdk), BF16),
                   jax.ShapeDtypeStruct((B, NC, hdk, LANES), BF16),
                   jax.ShapeDtypeStruct((B, S, D), BF16),
                   jax.ShapeDtypeStruct((B, S, D), BF16),
                   jax.ShapeDtypeStruct((B, NC * ng, LANES), F32)],
        scratch_shapes=[pltpu.VMEM((D, hdk + 2 * D), BF16), pltpu.VMEM((hdk + ng, D), BF16)],
        compiler_params=_params(),
        name="mlstm_proj",
    )(x, norm_w.reshape(1, D), w_t, bias_rows)


FFN_CAST_BLOCKS = 8


def _ffn_cast_specs(ffn_w_in, ffn_w_out, layer, step_of, total_steps):
    nblk = min(FFN_CAST_BLOCKS, total_steps)
    rep = total_steps // nblk
    args, in_specs, out_specs, out_shape = [], [], [], []
    for w in (ffn_w_in, ffn_w_out):
        rows, cols = w.shape[1] // nblk, w.shape[2]
        args.append(w)
        in_specs.append(pl.BlockSpec((None, rows, cols),
                                     lambda *g, l=layer: (l, step_of(*g) // rep, 0)))
        out_specs.append(pl.BlockSpec((rows, cols), lambda *g: (step_of(*g) // rep, 0)))
        out_shape.append(jax.ShapeDtypeStruct(w.shape[1:], BF16))
    return rep, args, in_specs, out_specs, out_shape


def _ffn_cast(step, rep, srcs, dsts):
    @pl.when(step % rep == 0)
    def _():
        for src, dst in zip(srcs, dsts):
            dst[...] = src[...].astype(BF16)


def _log_sigmoid(x):
    return jnp.minimum(x, 0.0) - jnp.log1p(jnp.exp(-jnp.abs(x)))


def _mlstm_core_kernel(q_ref, kt_ref, v_ref, rf_ref, wi_ref, wo_ref, of_ref, ob_ref, wi_out, wo_out, st_ref, c_ref,
                       *, nc, dv, unroll, cast_rep):
    L = LANES
    h = pl.program_id(1)
    ng = 4 * MLSTM_HEADS
    _ffn_cast(pl.program_id(0) * pl.num_programs(1) + h, cast_rep, (wi_ref, wo_ref), (wi_out, wo_out))

    def gate_rows(g):
        return rf_ref[0, pl.ds(MLSTM_HEADS * g + h, nc, stride=ng), :]

    lane = lax.broadcasted_iota(jnp.int32, (nc, L), 1)

    def prefix_sum(x):
        for s in (1, 2, 4, 8, 16, 32, 64):
            x = x + jnp.where(lane >= s, pltpu.roll(x, s, axis=1), 0.0)
        return x

    def suffix_sum(x):
        for s in (1, 2, 4, 8, 16, 32, 64):
            x = x + jnp.where(lane < L - s, pltpu.roll(x, L - s, axis=1), 0.0)
        return x

    for d in range(2):
        log_i = gate_rows(2 * d)
        log_f = _log_sigmoid(gate_rows(2 * d + 1))
        if d == 0:
            b = prefix_sum(log_f)
            b_last = jnp.broadcast_to(b[:, L - 1:L], (nc, L))
        else:
            b = suffix_sum(log_f)
            b_last = jnp.broadcast_to(b[:, 0:1], (nc, L))
        r = log_i - b
        a = b_last + r
        a_max = jnp.broadcast_to(jnp.max(a, axis=1, keepdims=True), (nc, L))
        st_ref[d, 0] = log_f
        st_ref[d, 1] = r
        st_ref[d, 2] = jnp.exp(a - a_max)
        st_ref[d, 3] = b_last
        st_ref[d, 4] = a_max

    def m_scan(i, carry):
        new = []
        for d, c in ((0, i), (1, nc - 1 - i)):
            row = pl.ds(c, 1)
            st_ref[d, 5, row, :] = carry[d]
            new.append(jnp.maximum(st_ref[d, 3, row, :] + carry[d], st_ref[d, 4, row, :]))
        return tuple(new)

    m0 = jnp.zeros((1, L), F32)
    lax.fori_loop(0, nc, m_scan, (m0, m0))
    for d in range(2):
        b_last, a_max, m_prev = st_ref[d, 3], st_ref[d, 4], st_ref[d, 5]
        m_new = jnp.maximum(b_last + m_prev, a_max)
        st_ref[d, 3] = jnp.exp(b_last + m_prev - m_new)
        st_ref[d, 2] = st_ref[d, 2] * jnp.exp(a_max - m_new)
        st_ref[d, 0] = st_ref[d, 0] * LOG2E
        st_ref[d, 1] = st_ref[d, 1] * LOG2E
        st_ref[d, 5] = m_prev * LOG2E

    c_ref[...] = jnp.zeros_like(c_ref)

    t_idx = lax.broadcasted_iota(jnp.int32, (L, L), 0)
    s_idx = lax.broadcasted_iota(jnp.int32, (L, L), 1)
    masks = (s_idx <= t_idx, s_idx >= t_idx)
    ones = jnp.ones((L, L), BF16)

    def tile3(x):
        return jnp.concatenate([x] * (dv // L + 1), axis=1)

    o_refs = (of_ref, ob_ref)

    def body(i, carry):
        items = []
        for u in range(unroll):
            c = i * unroll + u
            items += [(0, c), (1, nc - 1 - c)]

        work = []
        for d, c in items:
            rows = pl.ds(pl.multiple_of(c * L, L), L)
            qc = q_ref[0, rows, :]
            kt = kt_ref[0, c]
            vaug = jnp.concatenate([v_ref[0, rows, :], ones], axis=1)
            w = st_ref[d, 2, pl.ds(c, 1), :]
            scores = _dot(qc, kt)
            kv = _dot(kt * w.astype(BF16), vaug)
            work.append((rows, qc, vaug, scores, kv))

        states = [c_ref[0], c_ref[1]]
        prev_states = []
        for (d, c), (rows, qc, vaug, scores, kv) in zip(items, work):
            prev_states.append(states[d].astype(BF16))
            states[d] = tile3(st_ref[d, 3, pl.ds(c, 1), :]) * states[d] + kv
        c_ref[0] = states[0]
        c_ref[1] = states[1]

        for (d, c), (rows, qc, vaug, scores, kv), prev_state in zip(items, work, prev_states):
            log_f = st_ref[d, 0, pl.ds(c, 1), :]
            r = st_ref[d, 1, pl.ds(c, 1), :]
            m_prev = st_ref[d, 5, pl.ds(c, 1), :]
            mask = masks[d]
            r_masked = jnp.where(mask, r, NEG_INF)
            cm = jnp.max(r_masked, axis=1, keepdims=True)
            b_col = jnp.sum(jnp.where(mask, log_f, 0.0), axis=1, keepdims=True)
            mu = jnp.broadcast_to(jnp.maximum(m_prev[:, 0:1], cm), (L, L))
            decay = jnp.exp2(r_masked - mu)
            inter = jnp.exp2(m_prev - mu)
            lhs = jnp.concatenate([(scores * decay).astype(BF16), inter.astype(BF16) * qc], axis=1)
            both = _dot(lhs, jnp.concatenate([vaug, prev_state], axis=0))
            den = jnp.maximum(jnp.abs(both[:, dv:]), jnp.exp2(-(b_col + mu)))
            o_refs[d][0, rows, :] = (both[:, :dv] * tile3(1.0 / den)[:, :dv]).astype(BF16)
        return carry

    lax.fori_loop(0, nc // unroll, body, 0)


def _mlstm_core(q, kt, v, rf, ffn_w_in, ffn_w_out, layer, *, unroll=4):
    B, S, hdk = q.shape
    H = MLSTM_HEADS
    dk = hdk // H
    dv = v.shape[-1] // H
    nc = S // LANES
    rep, w_args, w_in_specs, w_out_specs, w_out_shape = _ffn_cast_specs(
        ffn_w_in, ffn_w_out, layer, lambda b, h: b * H + h, B * H)
    return pl.pallas_call(
        functools.partial(_mlstm_core_kernel, nc=nc, dv=dv, unroll=min(unroll, nc), cast_rep=rep),
        grid=(B, H),
        in_specs=[pl.BlockSpec((1, S, dk), lambda b, h: (b, 0, h)),
                  pl.BlockSpec((1, nc, dk, LANES), lambda b, h: (b, 0, h, 0)),
                  pl.BlockSpec((1, S, dv), lambda b, h: (b, 0, h)),
                  pl.BlockSpec((1, rf.shape[1], LANES), lambda b, h: (b, 0, 0))] + w_in_specs,
        out_specs=[pl.BlockSpec((1, S, dv), lambda b, h: (b, 0, h))] * 2 + w_out_specs,
        out_shape=[jax.ShapeDtypeStruct((B, S, H * dv), BF16)] * 2 + w_out_shape,
        scratch_shapes=[pltpu.VMEM((2, 6, nc, LANES), F32),
                        pltpu.VMEM((2, dk, dv + LANES), F32)],
        compiler_params=_params(),
        name="mlstm_core",
    )(q, kt, v, rf, *w_args)


def _attn_weight_prep(w_ref, wt_ref):
    @pl.when((pl.program_id(0) == 0) & (pl.program_id(1) == 0))
    def _():
        for c in range(0, wt_ref.shape[0], LANES):
            wt_ref[c:c + LANES, :] = w_ref[:, c:c + LANES].T.astype(BF16)


def _attn_proj_rows(x_rows, pos_row, invf, nw, wt_ref, qt_ref, k_ref, vt_ref, chunk0):
    hd = ATTN_HEAD_DIM
    half = hd // 2
    nq = ATTN_Q_HEADS * hd
    nk = ATTN_KV_HEADS * hd
    scale = LOG2E * hd ** -0.5
    piece = 8 * hd
    sub = x_rows.shape[0]
    hn = _rmsnorm(x_rows, nw).astype(BF16)
    ang = invf * pos_row.astype(F32)
    cos = jnp.cos(ang)
    sin = jnp.sin(ang)
    chunks = [(chunk0 + jj, slice(jj * LANES, (jj + 1) * LANES)) for jj in range(sub // LANES)]

    def rope(xh):
        x1, x2 = xh[:half], xh[half:]
        return jnp.concatenate([x1 * cos - x2 * sin, x2 * cos + x1 * sin], axis=0)

    qp = _dot_nt(wt_ref[:nq, :], hn)
    kv = _dot_nt(wt_ref[nq:, :], hn)
    for lo in range(0, nq, piece):
        qt = jnp.concatenate([rope(qp[lo + h * hd:lo + (h + 1) * hd]) * scale for h in range(piece // hd)],
                             axis=0).astype(BF16)
        for j, cols in chunks:
            qt_ref[0, j, lo:lo + piece, :] = qt[:, cols]
    vt = kv[nk:].astype(BF16)
    for j, cols in chunks:
        vt_ref[0, j] = vt[:, cols]
    kt = jnp.concatenate([rope(kv[g * hd:(g + 1) * hd]) for g in range(ATTN_KV_HEADS)], axis=0)
    for j, cols in chunks:
        for f in range(nk // LANES):
            k_ref[0, j * LANES:(j + 1) * LANES, f * LANES:(f + 1) * LANES] = (
                kt[f * LANES:(f + 1) * LANES, cols].T.astype(BF16))


def _attn_proj_kernel(x_ref, nw_ref, w_ref, pos_ref, invf_ref, qt_ref, k_ref, vt_ref, wt_ref, *, tm, nsub):
    _attn_weight_prep(w_ref, wt_ref)
    sub = tm // nsub
    for r in range(nsub):
        rs = slice(r * sub, (r + 1) * sub)
        _attn_proj_rows(x_ref[0, rs, :], pos_ref[0, :, rs], invf_ref[...], nw_ref[...], wt_ref,
                        qt_ref, k_ref, vt_ref, r * (sub // LANES))


def _attn_proj_specs(S, D, tm, w_in, layer):
    nq = ATTN_Q_HEADS * ATTN_HEAD_DIM
    nk = ATTN_KV_HEADS * ATTN_HEAD_DIM
    nj = tm // LANES
    nb = S // LANES
    in_specs = [_resident((1, D)),
                pl.BlockSpec((None,) + w_in.shape[1:], lambda b, i: (layer, 0, 0),
                             pipeline_mode=pl.Buffered(1)),
                pl.BlockSpec((1, 1, tm), lambda b, i: (b, 0, i)),
                _resident((ATTN_HEAD_DIM // 2, 1))]
    out_specs = [pl.BlockSpec((1, nj, nq, LANES), lambda b, i: (b, i, 0, 0)),
                 pl.BlockSpec((1, tm, nk), lambda b, i: (b, i, 0)),
                 pl.BlockSpec((1, nj, nk, LANES), lambda b, i: (b, i, 0, 0))]
    out_shape = lambda B: [jax.ShapeDtypeStruct((B, nb, nq, LANES), BF16),
                           jax.ShapeDtypeStruct((B, S, nk), BF16),
                           jax.ShapeDtypeStruct((B, nb, nk, LANES), BF16)]
    scratch = [pltpu.VMEM((nq + 2 * nk, D), BF16)]
    return in_specs, out_specs, out_shape, scratch


def _attn_proj(x, norm_w, w_in, layer, pos_rows, inv_freq, *, tm=1024, nsub=4):
    B, S, D = x.shape
    tm = min(tm, S)
    p_in, p_out, p_shape, p_scratch = _attn_proj_specs(S, D, tm, w_in, layer)
    return pl.pallas_call(
        functools.partial(_attn_proj_kernel, tm=tm, nsub=nsub),
        grid=(B, S // tm),
        in_specs=[pl.BlockSpec((1, tm, D), lambda b, i: (b, i, 0))] + p_in,
        out_specs=p_out,
        out_shape=p_shape(B),
        scratch_shapes=p_scratch,
        compiler_params=_params(),
        name="attn_proj",
    )(x, norm_w.reshape(1, D), w_in, pos_rows, inv_freq)


def _attn_core_kernel(qt_ref, kp_ref, kc_ref, kn_ref, vp_ref, vc_ref, vn_ref, sink_ref, wi_ref, wo_ref,
                      o_ref, wi_out, wo_out, k_all, v_all, *, nb, nblk, cast_rep):
    L = LANES
    hd = ATTN_HEAD_DIM
    G = ATTN_GROUP
    c = pl.program_id(1)
    _ffn_cast(pl.program_id(0) * pl.num_programs(1) + c, cast_rep, (wi_ref, wo_ref), (wi_out, wo_out))
    k_all[0:L] = kp_ref[0]
    k_all[L:(nblk + 1) * L] = kc_ref[0]
    k_all[(nblk + 1) * L:] = kn_ref[0]
    v_all[0] = vp_ref[0, 0]
    v_all[1:nblk + 1] = vc_ref[0]
    v_all[nblk + 1] = vn_ref[0, 0]

    key = lax.broadcasted_iota(jnp.int32, (L, G * L), 0)
    qry = lax.broadcasted_iota(jnp.int32, (L, G * L), 1) % L
    prev_mask = jnp.where(key >= qry, 0.0, NEG_INF)
    next_mask = jnp.where(key <= qry, 0.0, NEG_INF)
    zeros = jnp.zeros((hd, G * L), BF16)
    ones = jnp.ones((16, 3 * L), BF16)

    def scores(j, g):
        qg = jnp.concatenate([qt_ref[0, j, (G * g + i) * hd:(G * g + i + 1) * hd, :] for i in range(G)],
                             axis=1)
        qz = jnp.concatenate([qg, zeros] if g % 2 == 0 else [zeros, qg], axis=0)
        return _dot(k_all[j * L:(j + 3) * L, (g // 2) * L:(g // 2 + 1) * L], qz)

    items = [(j, g) for j in range(nblk) for g in range(ATTN_KV_HEADS)]
    ahead = 2
    pending = [scores(*it) for it in items[:ahead]]
    for n, (j, g) in enumerate(items):
        s = pending.pop(0)
        if n + ahead < len(items):
            pending.append(scores(*items[n + ahead]))
        if g == 0:
            blk = c * nblk + j
            prev_bias = prev_mask + jnp.where(blk > 0, 0.0, NEG_INF)
            next_bias = next_mask + jnp.where(blk < nb - 1, 0.0, NEG_INF)
        s = [s[0:L] + prev_bias, s[L:2 * L], s[2 * L:] + next_bias]
        sink = sink_ref[g:g + 1, :]
        m8 = functools.reduce(jnp.maximum, [sj.reshape(L // 8, 8, G * L).max(axis=0) for sj in s])
        m = jnp.maximum(sink, jnp.max(m8, axis=0, keepdims=True))
        p = jnp.concatenate([jnp.exp2(sj - m).astype(BF16) for sj in s], axis=0)
        vg = jnp.concatenate([v_all[j + i, g * hd:(g + 1) * hd, :] for i in range(3)], axis=1)
        acc = _dot(jnp.concatenate([vg, ones], axis=0), p)
        denom = acc[hd:hd + 1] + jnp.exp2(sink - m)
        out = acc[:hd] * (1.0 / denom)
        for pair in range(G // 2):
            tile = jnp.concatenate([out[:, (2 * pair) * L:(2 * pair + 1) * L],
                                    out[:, (2 * pair + 1) * L:(2 * pair + 2) * L]], axis=0)
            col = (G * g + 2 * pair) * hd
            o_ref[0, j * L:(j + 1) * L, col:col + L] = tile.T.astype(BF16)


def _attn_core(qt, k, vt, sink_rows, ffn_w_in, ffn_w_out, layer, *, nblk=8):
    B, nb, nq, L = qt.shape
    nk = k.shape[-1]
    S = nb * L
    nblk = min(nblk, nb)
    steps = nb // nblk
    prev = lambda b, c: (b, jnp.maximum(c * nblk - 1, 0), 0)
    cur = lambda b, c: (b, c, 0)
    nxt = lambda b, c: (b, jnp.minimum((c + 1) * nblk, nb - 1), 0)
    four = lambda f: (lambda b, c: f(b, c) + (0,))
    rep, w_args, w_in_specs, w_out_specs, w_out_shape = _ffn_cast_specs(
        ffn_w_in, ffn_w_out, layer, lambda b, c: b * steps + c, B * steps)
    return pl.pallas_call(
        functools.partial(_attn_core_kernel, nb=nb, nblk=nblk, cast_rep=rep),
        grid=(B, steps),
        in_specs=[pl.BlockSpec((1, nblk, nq, L), four(cur)),
                  pl.BlockSpec((1, L, nk), prev), pl.BlockSpec((1, nblk * L, nk), cur),
                  pl.BlockSpec((1, L, nk), nxt),
                  pl.BlockSpec((1, 1, nk, L), four(prev)), pl.BlockSpec((1, nblk, nk, L), four(cur)),
                  pl.BlockSpec((1, 1, nk, L), four(nxt)),
                  _resident(sink_rows.shape)] + w_in_specs,
        out_specs=[pl.BlockSpec((1, nblk * L, nq), cur)] + w_out_specs,
        out_shape=[jax.ShapeDtypeStruct((B, S, nq), BF16)] + w_out_shape,
        scratch_shapes=[pltpu.VMEM(((nblk + 2) * L, nk), BF16),
                        pltpu.VMEM((nblk + 2, nk, L), BF16)],
        compiler_params=_params(),
        name="attn_core",
    )(qt, k, k, k, vt, vt, vt, sink_rows, *w_args)


def _mlstm_mixer(x, norm_w, w_t, layer, b_gate, ffn_w_in, ffn_w_out, ffn_layer):
    bias_rows = jnp.broadcast_to(b_gate.astype(F32).reshape(-1, 1), (b_gate.shape[0], LANES))
    q, kt, v, og, rf = _mlstm_proj(x, norm_w, w_t, layer, bias_rows)
    h_fwd, h_bwd, wi, wo = _mlstm_core(q, kt, v, rf, ffn_w_in, ffn_w_out, ffn_layer)
    return (h_fwd, h_bwd, og), wi, wo


def _attn_mixer(x, pos_rows, inv_freq, norm_w, w_in, layer, sink, ffn_w_in, ffn_w_out, ffn_layer):
    G, L = ATTN_GROUP, LANES
    qt, k, vt = _attn_proj(x, norm_w, w_in, layer, pos_rows, inv_freq)
    sink_rows = jnp.repeat(LOG2E * sink.astype(F32).reshape(ATTN_KV_HEADS, G), L, axis=1)
    a, wi, wo = _attn_core(qt, k, vt, sink_rows, ffn_w_in, ffn_w_out, ffn_layer)
    return (a,), wi, wo


def kernel(x, positions, norm_mix_w, norm_ffn_w, norm_final_w, mlstm_w_in, mlstm_b_gate, mlstm_norm_w, mlstm_w_out, attn_w_in, attn_sink, attn_w_out, ffn_w_in, ffn_w_out):
    depth = norm_mix_w.shape[0]
    B, S = positions.shape
    half = ATTN_HEAD_DIM // 2
    inv_freq = (ROPE_THETA ** (-jnp.arange(half, dtype=F32) / half)).reshape(half, 1)
    pos_rows = positions.reshape(B, 1, S)
    mlstm_wo, attn_wo = mlstm_w_out.astype(BF16), attn_w_out.astype(BF16)
    mlstm_wt = jnp.swapaxes(mlstm_w_in, 1, 2)
    for i in range(depth):
        j = i // 2
        final_w = norm_final_w if i == depth - 1 else None
        if i % 2 == 0:
            mixed, wi, wo = _mlstm_mixer(x, norm_mix_w[i], mlstm_wt, j, mlstm_b_gate[j],
                                         ffn_w_in, ffn_w_out, i)
            x = _layer_tail("mlstm", mixed + (mlstm_norm_w[j],), x, mlstm_wo, j, norm_ffn_w[i], wi, wo,
                            final_w)
        else:
            mixed, wi, wo = _attn_mixer(x, pos_rows, inv_freq, norm_mix_w[i], attn_w_in, j, attn_sink[j],
                                        ffn_w_in, ffn_w_out, i)
            x = _layer_tail("attn", mixed, x, attn_wo, j, norm_ffn_w[i], wi, wo, final_w)
    return x
```

```python
import functools

import jax
import jax.numpy as jnp
from jax import lax
from jax.experimental import pallas as pl
from jax.experimental.pallas import tpu as pltpu

F32 = jnp.float32
BF16 = jnp.bfloat16

EPS = 1e-6
LANES = 128

MLSTM_HEADS = 4
MLSTM_CHUNK = 128
ATTN_HEAD_DIM = 64
ATTN_Q_HEADS = 16
ATTN_KV_HEADS = 4
ATTN_GROUP = ATTN_Q_HEADS // ATTN_KV_HEADS
ATTN_BLOCK = 128
ROPE_THETA = 10000.0

NEG_INF = float("-inf")
LOG2E = 1.4426950408889634
VMEM_LIMIT = 56 * 1024 * 1024


def _params():
    return pltpu.CompilerParams(dimension_semantics=("arbitrary", "arbitrary"),
                                vmem_limit_bytes=VMEM_LIMIT)


def _resident(shape):
    return pl.BlockSpec(shape, lambda *_: (0,) * len(shape), pipeline_mode=pl.Buffered(1))


def _rmsnorm(x, w):
    ms = jnp.mean(x * x, axis=-1, keepdims=True)
    return x * lax.rsqrt(ms + EPS) * w


def _sigmoid(x):
    return 1.0 / (1.0 + jnp.exp(-x))


def _dot(a, b):
    return jnp.dot(a, b, preferred_element_type=F32)


def _dot_nt(a, b):
    return lax.dot_general(a, b, (((1,), (1,)), ((), ())), preferred_element_type=F32)


def _layer_tail_kernel(*refs, d_ff, tf, nsub, mixer, final):
    it = iter(refs)
    if mixer == "mlstm":
        hs_ref, og_ref, hw_ref = next(it), next(it), next(it)
    else:
        a_ref = next(it)
    x_ref, wo_ref, nw_ref, win_ref, wout_ref = (next(it) for _ in range(5))
    fw_ref = next(it) if final else None
    o_ref, x1_ref, hn_ref, act_ref = (next(it) for _ in range(4))

    tm = x_ref.shape[1]
    subs = [slice(r * (tm // nsub), (r + 1) * (tm // nsub)) for r in range(nsub)]
    for rs in subs:
        if mixer == "mlstm":
            dv = hs_ref.shape[-1] // MLSTM_HEADS
            parts = []
            for h in range(MLSTM_HEADS):
                hs = hs_ref[0, rs, h * dv:(h + 1) * dv]
                ms = jnp.mean(hs * hs, axis=-1, keepdims=True)
                parts.append(hs * lax.rsqrt(ms + EPS))
            y = (jnp.concatenate(parts, axis=-1) * hw_ref[...] * og_ref[0, rs, :].astype(F32)).astype(BF16)
        else:
            y = a_ref[0, rs, :]
        x1_ref[rs, :] = x_ref[0, rs, :] + _dot(y, wo_ref[...])
    for rs in subs:
        hn_ref[rs, :] = _rmsnorm(x1_ref[rs, :], nw_ref[...]).astype(BF16)
    for rs in subs:
        for j in range(d_ff // tf):
            hn = hn_ref[rs, :]
            g = _dot(hn, win_ref[:, j * tf:(j + 1) * tf])
            u = _dot(hn, win_ref[:, d_ff + j * tf:d_ff + (j + 1) * tf])
            act_ref[rs, j * tf:(j + 1) * tf] = (g * _sigmoid(g) * u).astype(BF16)
    for rs in subs:
        out = x1_ref[rs, :] + _dot(act_ref[rs, :], wout_ref[...])
        if final:
            out = _rmsnorm(out, fw_ref[...])
        o_ref[0, rs, :] = out


def _layer_tail(mixer, mixer_args, x, w_o, layer_o, norm_w, w_in, w_out, final_w=None,
                *, tm=512, tf=256, nsub=2):
    B, S, D = x.shape
    tm = min(tm, S)
    d_ff = w_out.shape[0]
    final = final_w is not None
    row = pl.BlockSpec((1, tm, D), lambda b, i: (b, i, 0))

    def stacked(shape, l):
        return pl.BlockSpec((None,) + shape, lambda b, i: (l,) + (0,) * len(shape),
                            pipeline_mode=pl.Buffered(1))

    if mixer == "mlstm":
        hs, og, head_w = mixer_args
        args = [hs, og, head_w.reshape(1, D)]
        in_specs = [row, row, _resident((1, D))]
    else:
        args = list(mixer_args)
        in_specs = [row]
    args += [x, w_o, norm_w.reshape(1, D), w_in, w_out]
    in_specs += [row, stacked((w_o.shape[1], D), layer_o), _resident((1, D)),
                 _resident((D, 2 * d_ff)), _resident((d_ff, D))]
    if final:
        args.append(final_w.reshape(1, D))
        in_specs.append(_resident((1, D)))
    return pl.pallas_call(
        functools.partial(_layer_tail_kernel, d_ff=d_ff, tf=tf, nsub=nsub, mixer=mixer, final=final),
        grid=(B, S // tm),
        in_specs=in_specs,
        out_specs=row,
        out_shape=jax.ShapeDtypeStruct((B, S, D), F32),
        scratch_shapes=[pltpu.VMEM((tm, D), F32), pltpu.VMEM((tm, D), BF16),
                        pltpu.VMEM((tm, d_ff), BF16)],
        compiler_params=_params(),
        name=mixer + "_tail",
    )(*args)


def _mlstm_proj_kernel(x_ref, nw_ref, wt_ref, bias_ref,
                       q_ref, kt_ref, v_ref, og_ref, rf_ref, w_ref, wkt_ref, *, dk, tm, nsub):
    d = v_ref.shape[-1]
    hdk = q_ref.shape[-1]
    ng = rf_ref.shape[1] // (tm // LANES)
    sub = tm // nsub

    @pl.when((pl.program_id(0) == 0) & (pl.program_id(1) == 0))
    def _():
        for c in range(0, hdk, LANES):
            w_ref[:, c:c + LANES] = wt_ref[c:c + LANES, :].T.astype(BF16)
            wkt_ref[c:c + LANES, :] = wt_ref[hdk + c:hdk + c + LANES, :].astype(BF16)
        for c in range(0, 2 * d, LANES):
            w_ref[:, hdk + c:hdk + c + LANES] = wt_ref[2 * hdk + c:2 * hdk + c + LANES, :].T.astype(BF16)
        wkt_ref[hdk:, :] = wt_ref[2 * hdk + 2 * d:, :].astype(BF16)

    for r in range(nsub):
        rs = slice(r * sub, (r + 1) * sub)
        hn = _rmsnorm(x_ref[0, rs, :], nw_ref[...]).astype(BF16)
        og_ref[0, rs, :] = _sigmoid(_dot(hn, w_ref[:, hdk + d:])).astype(BF16)
        v_ref[0, rs, :] = _dot(hn, w_ref[:, hdk:hdk + d]).astype(BF16)
        q_ref[0, rs, :] = (_dot(hn, w_ref[:, :hdk]) * (dk ** -0.5)).astype(BF16)
        kg = _dot_nt(wkt_ref[...], hn)
        kt = kg[:hdk].astype(BF16)
        for jj in range(sub // LANES):
            j = r * (sub // LANES) + jj
            kt_ref[0, j] = kt[:, jj * LANES:(jj + 1) * LANES]
            rf_ref[0, j * ng:(j + 1) * ng, :] = kg[hdk:, jj * LANES:(jj + 1) * LANES] + bias_ref[...]


def _mlstm_proj(x, norm_w, w_t, layer, bias_rows, *, tm=1024, nsub=4):
    B, S, D = x.shape
    H = MLSTM_HEADS
    hdk = D // 2
    dk = hdk // H
    tm = min(tm, S)
    ng = bias_rows.shape[0]
    nj = tm // LANES
    NC = S // LANES
    row = lambda b, i: (b, i, 0)
    return pl.pallas_call(
        functools.partial(_mlstm_proj_kernel, dk=dk, tm=tm, nsub=nsub),
        grid=(B, S // tm),
        in_specs=[pl.BlockSpec((1, tm, D), row), _resident((1, D)),
                  pl.BlockSpec((None,) + w_t.shape[1:], lambda b, i: (layer, 0, 0),
                               pipeline_mode=pl.Buffered(1)),
                  _resident((ng, LANES))],
        out_specs=[pl.BlockSpec((1, tm, hdk), row),
                   pl.BlockSpec((1, nj, hdk, LANES), lambda b, i: (b, i, 0, 0)),
                   pl.BlockSpec((1, tm, D), row),
                   pl.BlockSpec((1, tm, D), row),
                   pl.BlockSpec((1, nj * ng, LANES), row)],
        out_shape=[jax.ShapeDtypeStruct((B, S, hdk), BF16),
                   jax.ShapeDtypeStruct((B, NC, hdk, LANES), BF16),
                   jax.ShapeDtypeStruct((B, S, D), BF16),
                   jax.ShapeDtypeStruct((B, S, D), BF16),
                   jax.ShapeDtypeStruct((B, NC * ng, LANES), F32)],
        scratch_shapes=[pltpu.VMEM((D, hdk + 2 * D), BF16), pltpu.VMEM((hdk + ng, D), BF16)],
        compiler_params=_params(),
        name="mlstm_proj",
    )(x, norm_w.reshape(1, D), w_t, bias_rows)


FFN_CAST_BLOCKS = 8


def _ffn_cast_specs(ffn_w_in, ffn_w_out, layer, step_of, total_steps):
    nblk = min(FFN_CAST_BLOCKS, total_steps)
    rep = total_steps // nblk
    args, in_specs, out_specs, out_shape = [], [], [], []
    for w in (ffn_w_in, ffn_w_out):
        rows, cols = w.shape[1] // nblk, w.shape[2]
        args.append(w)
        in_specs.append(pl.BlockSpec((None, rows, cols),
                                     lambda *g, l=layer: (l, step_of(*g) // rep, 0)))
        out_specs.append(pl.BlockSpec((rows, cols), lambda *g: (step_of(*g) // rep, 0)))
        out_shape.append(jax.ShapeDtypeStruct(w.shape[1:], BF16))
    return rep, args, in_specs, out_specs, out_shape


def _ffn_cast(step, rep, srcs, dsts):
    @pl.when(step % rep == 0)
    def _():
        for src, dst in zip(srcs, dsts):
            dst[...] = src[...].astype(BF16)


def _log_sigmoid(x):
    return jnp.minimum(x, 0.0) - jnp.log1p(jnp.exp(-jnp.abs(x)))


def _mlstm_core_kernel(q_ref, kt_ref, v_ref, rf_ref, wi_ref, wo_ref, o_ref, wi_out, wo_out, st_ref, c_ref,
                       *, nc, dv, unroll, cast_rep):
    L = LANES
    h = pl.program_id(1)
    ng = 4 * MLSTM_HEADS
    _ffn_cast(pl.program_id(0) * pl.num_programs(1) + h, cast_rep, (wi_ref, wo_ref), (wi_out, wo_out))

    def gate_rows(g):
        return rf_ref[0, pl.ds(MLSTM_HEADS * g + h, nc, stride=ng), :]

    lane = lax.broadcasted_iota(jnp.int32, (nc, L), 1)

    def prefix_sum(x):
        for s in (1, 2, 4, 8, 16, 32, 64):
            x = x + jnp.where(lane >= s, pltpu.roll(x, s, axis=1), 0.0)
        return x

    def suffix_sum(x):
        for s in (1, 2, 4, 8, 16, 32, 64):
            x = x + jnp.where(lane < L - s, pltpu.roll(x, L - s, axis=1), 0.0)
        return x

    for d in range(2):
        log_i = gate_rows(2 * d)
        log_f = _log_sigmoid(gate_rows(2 * d + 1))
        if d == 0:
            b = prefix_sum(log_f)
            b_last = jnp.broadcast_to(b[:, L - 1:L], (nc, L))
        else:
            b = suffix_sum(log_f)
            b_last = jnp.broadcast_to(b[:, 0:1], (nc, L))
        r = log_i - b
        a = b_last + r
        a_max = jnp.broadcast_to(jnp.max(a, axis=1, keepdims=True), (nc, L))
        st_ref[d, 0] = log_f
        st_ref[d, 1] = r
        st_ref[d, 2] = jnp.exp(a - a_max)
        st_ref[d, 3] = b_last
        st_ref[d, 4] = a_max

    def m_scan(i, carry):
        new = []
        for d, c in ((0, i), (1, nc - 1 - i)):
            row = pl.ds(c, 1)
            st_ref[d, 5, row, :] = carry[d]
            new.append(jnp.maximum(st_ref[d, 3, row, :] + carry[d], st_ref[d, 4, row, :]))
        return tuple(new)

    m0 = jnp.zeros((1, L), F32)
    lax.fori_loop(0, nc, m_scan, (m0, m0))
    for d in range(2):
        b_last, a_max, m_prev = st_ref[d, 3], st_ref[d, 4], st_ref[d, 5]
        m_new = jnp.maximum(b_last + m_prev, a_max)
        st_ref[d, 3] = jnp.exp(b_last + m_prev - m_new)
        st_ref[d, 2] = st_ref[d, 2] * jnp.exp(a_max - m_new)
        st_ref[d, 0] = st_ref[d, 0] * LOG2E
        st_ref[d, 1] = st_ref[d, 1] * LOG2E
        st_ref[d, 5] = m_prev * LOG2E

    c_ref[...] = jnp.zeros_like(c_ref)

    t_idx = lax.broadcasted_iota(jnp.int32, (L, L), 0)
    s_idx = lax.broadcasted_iota(jnp.int32, (L, L), 1)
    masks = (s_idx <= t_idx, s_idx >= t_idx)
    ones = jnp.ones((L, L), BF16)

    def tile3(x):
        return jnp.concatenate([x] * (dv // L + 1), axis=1)

    def body(i, carry, assign):
        items = []
        for u in range(unroll):
            c = i * unroll + u
            items += [(0, c), (1, nc - 1 - c)]

        work = []
        for d, c in items:
            rows = pl.ds(pl.multiple_of(c * L, L), L)
            qc = q_ref[0, rows, :]
            kt = kt_ref[0, c]
            vaug = jnp.concatenate([v_ref[0, rows, :], ones], axis=1)
            w = st_ref[d, 2, pl.ds(c, 1), :]
            scores = _dot(qc, kt)
            kv = _dot(kt * w.astype(BF16), vaug)
            work.append((rows, qc, vaug, scores, kv))

        states = [c_ref[0], c_ref[1]]
        prev_states = []
        for (d, c), (rows, qc, vaug, scores, kv) in zip(items, work):
            prev_states.append(states[d].astype(BF16))
            states[d] = tile3(st_ref[d, 3, pl.ds(c, 1), :]) * states[d] + kv
        c_ref[0] = states[0]
        c_ref[1] = states[1]

        for (d, c), (rows, qc, vaug, scores, kv), prev_state in zip(items, work, prev_states):
            log_f = st_ref[d, 0, pl.ds(c, 1), :]
            r = st_ref[d, 1, pl.ds(c, 1), :]
            m_prev = st_ref[d, 5, pl.ds(c, 1), :]
            mask = masks[d]
            r_masked = jnp.where(mask, r, NEG_INF)
            cm = jnp.max(r_masked, axis=1, keepdims=True)
            b_col = jnp.sum(jnp.where(mask, log_f, 0.0), axis=1, keepdims=True)
            mu = jnp.broadcast_to(jnp.maximum(m_prev[:, 0:1], cm), (L, L))
            decay = jnp.exp2(r_masked - mu)
            inter = jnp.exp2(m_prev - mu)
            lhs = jnp.concatenate([(scores * decay).astype(BF16), inter.astype(BF16) * qc], axis=1)
            both = _dot(lhs, jnp.concatenate([vaug, prev_state], axis=0))
            den = jnp.maximum(jnp.abs(both[:, dv:]), jnp.exp2(-(b_col + mu)))
            out = both[:, :dv] * tile3(1.0 / den)[:, :dv]
            if assign:
                o_ref[0, rows, :] = out
            else:
                o_ref[0, rows, :] += out
        return carry

    steps = nc // 2 // unroll
    lax.fori_loop(0, steps, functools.partial(body, assign=True), 0)
    lax.fori_loop(steps, 2 * steps, functools.partial(body, assign=False), 0)


def _mlstm_core(q, kt, v, rf, ffn_w_in, ffn_w_out, layer, *, unroll=8):
    B, S, hdk = q.shape
    H = MLSTM_HEADS
    dk = hdk // H
    dv = v.shape[-1] // H
    nc = S // LANES
    rep, w_args, w_in_specs, w_out_specs, w_out_shape = _ffn_cast_specs(
        ffn_w_in, ffn_w_out, layer, lambda b, h: b * H + h, B * H)
    return pl.pallas_call(
        functools.partial(_mlstm_core_kernel, nc=nc, dv=dv, unroll=min(unroll, nc // 2), cast_rep=rep),
        grid=(B, H),
        in_specs=[pl.BlockSpec((1, S, dk), lambda b, h: (b, 0, h)),
                  pl.BlockSpec((1, nc, dk, LANES), lambda b, h: (b, 0, h, 0)),
                  pl.BlockSpec((1, S, dv), lambda b, h: (b, 0, h)),
                  pl.BlockSpec((1, rf.shape[1], LANES), lambda b, h: (b, 0, 0))] + w_in_specs,
        out_specs=[pl.BlockSpec((1, S, dv), lambda b, h: (b, 0, h))] + w_out_specs,
        out_shape=[jax.ShapeDtypeStruct((B, S, H * dv), F32)] + w_out_shape,
        scratch_shapes=[pltpu.VMEM((2, 6, nc, LANES), F32),
                        pltpu.VMEM((2, dk, dv + LANES), F32)],
        compiler_params=_params(),
        name="mlstm_core",
    )(q, kt, v, rf, *w_args)


def _attn_weight_prep(w_ref, wt_ref):
    @pl.when((pl.program_id(0) == 0) & (pl.program_id(1) == 0))
    def _():
        for c in range(0, wt_ref.shape[0], LANES):
            wt_ref[c:c + LANES, :] = w_ref[:, c:c + LANES].T.astype(BF16)


def _attn_proj_rows(x_rows, pos_row, invf, nw, wt_ref, qt_ref, k_ref, vt_ref, chunk0):
    hd = ATTN_HEAD_DIM
    half = hd // 2
    nq = ATTN_Q_HEADS * hd
    nk = ATTN_KV_HEADS * hd
    scale = LOG2E * hd ** -0.5
    piece = 8 * hd
    sub = x_rows.shape[0]
    hn = _rmsnorm(x_rows, nw).astype(BF16)
    ang = invf * pos_row.astype(F32)
    cos = jnp.cos(ang)
    sin = jnp.sin(ang)
    chunks = [(chunk0 + jj, slice(jj * LANES, (jj + 1) * LANES)) for jj in range(sub // LANES)]

    def rope(xh):
        x1, x2 = xh[:half], xh[half:]
        return jnp.concatenate([x1 * cos - x2 * sin, x2 * cos + x1 * sin], axis=0)

    qp = _dot_nt(wt_ref[:nq, :], hn)
    kv = _dot_nt(wt_ref[nq:, :], hn)
    for lo in range(0, nq, piece):
        qt = jnp.concatenate([rope(qp[lo + h * hd:lo + (h + 1) * hd]) * scale for h in range(piece // hd)],
                             axis=0).astype(BF16)
        for j, cols in chunks:
            qt_ref[0, j, lo:lo + piece, :] = qt[:, cols]
    vt = kv[nk:].astype(BF16)
    for j, cols in chunks:
        vt_ref[0, j] = vt[:, cols]
    kt = jnp.concatenate([rope(kv[g * hd:(g + 1) * hd]) for g in range(ATTN_KV_HEADS)], axis=0)
    for j, cols in chunks:
        for f in range(nk // LANES):
            k_ref[0, j * LANES:(j + 1) * LANES, f * LANES:(f + 1) * LANES] = (
                kt[f * LANES:(f + 1) * LANES, cols].T.astype(BF16))


def _attn_proj_kernel(x_ref, nw_ref, w_ref, pos_ref, invf_ref, qt_ref, k_ref, vt_ref, wt_ref, *, tm, nsub):
    _attn_weight_prep(w_ref, wt_ref)
    sub = tm // nsub
    for r in range(nsub):
        rs = slice(r * sub, (r + 1) * sub)
        _attn_proj_rows(x_ref[0, rs, :], pos_ref[0, :, rs], invf_ref[...], nw_ref[...], wt_ref,
                        qt_ref, k_ref, vt_ref, r * (sub // LANES))


def _attn_proj_specs(S, D, tm, w_in, layer):
    nq = ATTN_Q_HEADS * ATTN_HEAD_DIM
    nk = ATTN_KV_HEADS * ATTN_HEAD_DIM
    nj = tm // LANES
    nb = S // LANES
    in_specs = [_resident((1, D)),
                pl.BlockSpec((None,) + w_in.shape[1:], lambda b, i: (layer, 0, 0),
                             pipeline_mode=pl.Buffered(1)),
                pl.BlockSpec((1, 1, tm), lambda b, i: (b, 0, i)),
                _resident((ATTN_HEAD_DIM // 2, 1))]
    out_specs = [pl.BlockSpec((1, nj, nq, LANES), lambda b, i: (b, i, 0, 0)),
                 pl.BlockSpec((1, tm, nk), lambda b, i: (b, i, 0)),
                 pl.BlockSpec((1, nj, nk, LANES), lambda b, i: (b, i, 0, 0))]
    out_shape = lambda B: [jax.ShapeDtypeStruct((B, nb, nq, LANES), BF16),
                           jax.ShapeDtypeStruct((B, S, nk), BF16),
                           jax.ShapeDtypeStruct((B, nb, nk, LANES), BF16)]
    scratch = [pltpu.VMEM((nq + 2 * nk, D), BF16)]
    return in_specs, out_specs, out_shape, scratch


def _attn_proj(x, norm_w, w_in, layer, pos_rows, inv_freq, *, tm=2048, nsub=8):
    B, S, D = x.shape
    tm = min(tm, S)
    p_in, p_out, p_shape, p_scratch = _attn_proj_specs(S, D, tm, w_in, layer)
    return pl.pallas_call(
        functools.partial(_attn_proj_kernel, tm=tm, nsub=nsub),
        grid=(B, S // tm),
        in_specs=[pl.BlockSpec((1, tm, D), lambda b, i: (b, i, 0))] + p_in,
        out_specs=p_out,
        out_shape=p_shape(B),
        scratch_shapes=p_scratch,
        compiler_params=_params(),
        name="attn_proj",
    )(x, norm_w.reshape(1, D), w_in, pos_rows, inv_freq)


def _attn_core_kernel(qt_ref, kp_ref, kc_ref, kn_ref, vp_ref, vc_ref, vn_ref, sink_ref, wi_ref, wo_ref,
                      o_ref, wi_out, wo_out, k_all, v_all, *, nb, nblk, cast_rep):
    L = LANES
    hd = ATTN_HEAD_DIM
    G = ATTN_GROUP
    c = pl.program_id(1)
    _ffn_cast(pl.program_id(0) * pl.num_programs(1) + c, cast_rep, (wi_ref, wo_ref), (wi_out, wo_out))
    k_all[0:L] = kp_ref[0]
    k_all[L:(nblk + 1) * L] = kc_ref[0]
    k_all[(nblk + 1) * L:] = kn_ref[0]
    v_all[0] = vp_ref[0, 0]
    v_all[1:nblk + 1] = vc_ref[0]
    v_all[nblk + 1] = vn_ref[0, 0]

    key = lax.broadcasted_iota(jnp.int32, (L, G * L), 0)
    qry = lax.broadcasted_iota(jnp.int32, (L, G * L), 1) % L
    prev_mask = jnp.where(key >= qry, 0.0, NEG_INF)
    next_mask = jnp.where(key <= qry, 0.0, NEG_INF)
    zeros = jnp.zeros((hd, G * L), BF16)
    ones = jnp.ones((16, 3 * L), BF16)

    def scores(j, g):
        qg = jnp.concatenate([qt_ref[0, j, (G * g + i) * hd:(G * g + i + 1) * hd, :] for i in range(G)],
                             axis=1)
        qz = jnp.concatenate([qg, zeros] if g % 2 == 0 else [zeros, qg], axis=0)
        return _dot(k_all[j * L:(j + 3) * L, (g // 2) * L:(g // 2 + 1) * L], qz)

    items = [(j, g) for j in range(nblk) for g in range(ATTN_KV_HEADS)]
    ahead = 2
    pending = [scores(*it) for it in items[:ahead]]
    for n, (j, g) in enumerate(items):
        s = pending.pop(0)
        if n + ahead < len(items):
            pending.append(scores(*items[n + ahead]))
        if g == 0:
            blk = c * nblk + j
            prev_bias = prev_mask + jnp.where(blk > 0, 0.0, NEG_INF)
            next_bias = next_mask + jnp.where(blk < nb - 1, 0.0, NEG_INF)
        s = [s[0:L] + prev_bias, s[L:2 * L], s[2 * L:] + next_bias]
        sink = sink_ref[g:g + 1, :]
        m8 = functools.reduce(jnp.maximum, [sj.reshape(L // 8, 8, G * L).max(axis=0) for sj in s])
        m = jnp.maximum(sink, jnp.max(m8, axis=0, keepdims=True))
        p = jnp.concatenate([jnp.exp2(sj - m).astype(BF16) for sj in s], axis=0)
        vg = jnp.concatenate([v_all[j + i, g * hd:(g + 1) * hd, :] for i in range(3)], axis=1)
        acc = _dot(jnp.concatenate([vg, ones], axis=0), p)
        denom = acc[hd:hd + 1] + jnp.exp2(sink - m)
        out = acc[:hd] * (1.0 / denom)
        for pair in range(G // 2):
            tile = jnp.concatenate([out[:, (2 * pair) * L:(2 * pair + 1) * L],
                                    out[:, (2 * pair + 1) * L:(2 * pair + 2) * L]], axis=0)
            col = (G * g + 2 * pair) * hd
            o_ref[0, j * L:(j + 1) * L, col:col + L] = tile.T.astype(BF16)


def _attn_core(qt, k, vt, sink_rows, ffn_w_in, ffn_w_out, layer, *, nblk=16):
    B, nb, nq, L = qt.shape
    nk = k.shape[-1]
    S = nb * L
    nblk = min(nblk, nb)
    steps = nb // nblk
    prev = lambda b, c: (b, jnp.maximum(c * nblk - 1, 0), 0)
    cur = lambda b, c: (b, c, 0)
    nxt = lambda b, c: (b, jnp.minimum((c + 1) * nblk, nb - 1), 0)
    four = lambda f: (lambda b, c: f(b, c) + (0,))
    rep, w_args, w_in_specs, w_out_specs, w_out_shape = _ffn_cast_specs(
        ffn_w_in, ffn_w_out, layer, lambda b, c: b * steps + c, B * steps)
    return pl.pallas_call(
        functools.partial(_attn_core_kernel, nb=nb, nblk=nblk, cast_rep=rep),
        grid=(B, steps),
        in_specs=[pl.BlockSpec((1, nblk, nq, L), four(cur)),
                  pl.BlockSpec((1, L, nk), prev), pl.BlockSpec((1, nblk * L, nk), cur),
                  pl.BlockSpec((1, L, nk), nxt),
                  pl.BlockSpec((1, 1, nk, L), four(prev)), pl.BlockSpec((1, nblk, nk, L), four(cur)),
                  pl.BlockSpec((1, 1, nk, L), four(nxt)),
                  _resident(sink_rows.shape)] + w_in_specs,
        out_specs=[pl.BlockSpec((1, nblk * L, nq), cur)] + w_out_specs,
        out_shape=[jax.ShapeDtypeStruct((B, S, nq), BF16)] + w_out_shape,
        scratch_shapes=[pltpu.VMEM(((nblk + 2) * L, nk), BF16),
                        pltpu.VMEM((nblk + 2, nk, L), BF16)],
        compiler_params=_params(),
        name="attn_core",
    )(qt, k, k, k, vt, vt, vt, sink_rows, *w_args)


def _mlstm_mixer(x, norm_w, w_t, layer, b_gate, ffn_w_in, ffn_w_out, ffn_layer):
    bias_rows = jnp.broadcast_to(b_gate.astype(F32).reshape(-1, 1), (b_gate.shape[0], LANES))
    q, kt, v, og, rf = _mlstm_proj(x, norm_w, w_t, layer, bias_rows)
    hs, wi, wo = _mlstm_core(q, kt, v, rf, ffn_w_in, ffn_w_out, ffn_layer)
    return (hs, og), wi, wo


def _attn_mixer(x, pos_rows, inv_freq, norm_w, w_in, layer, sink, ffn_w_in, ffn_w_out, ffn_layer):
    G, L = ATTN_GROUP, LANES
    qt, k, vt = _attn_proj(x, norm_w, w_in, layer, pos_rows, inv_freq)
    sink_rows = jnp.repeat(LOG2E * sink.astype(F32).reshape(ATTN_KV_HEADS, G), L, axis=1)
    a, wi, wo = _attn_core(qt, k, vt, sink_rows, ffn_w_in, ffn_w_out, ffn_layer)
    return (a,), wi, wo


def kernel(x, positions, norm_mix_w, norm_ffn_w, norm_final_w, mlstm_w_in, mlstm_b_gate, mlstm_norm_w, mlstm_w_out, attn_w_in, attn_sink, attn_w_out, ffn_w_in, ffn_w_out):
    depth = norm_mix_w.shape[0]
    B, S = positions.shape
    half = ATTN_HEAD_DIM // 2
    inv_freq = (ROPE_THETA ** (-jnp.arange(half, dtype=F32) / half)).reshape(half, 1)
    pos_rows = positions.reshape(B, 1, S)
    mlstm_wo, attn_wo = mlstm_w_out.astype(BF16), attn_w_out.astype(BF16)
    mlstm_wt = jnp.swapaxes(mlstm_w_in, 1, 2)
    for i in range(depth):
        j = i // 2
        final_w = norm_final_w if i == depth - 1 else None
        if i % 2 == 0:
            (hs, og), wi, wo = _mlstm_mixer(x, norm_mix_w[i], mlstm_wt, j, mlstm_b_gate[j],
                                            ffn_w_in, ffn_w_out, i)
            x = _layer_tail("mlstm", (hs, og, mlstm_norm_w[j]), x, mlstm_wo, j, norm_ffn_w[i], wi, wo,
                            final_w)
        else:
            mixed, wi, wo = _attn_mixer(x, pos_rows, inv_freq, norm_mix_w[i], attn_w_in, j, attn_sink[j],
                                        ffn_w_in, ffn_w_out, i)
            x = _layer_tail("attn", mixed, x, attn_wo, j, norm_ffn_w[i], wi, wo, final_w)
    return x
```

```python
import functools

import jax
import jax.numpy as jnp
from jax import lax
from jax.experimental import pallas as pl
from jax.experimental.pallas import tpu as pltpu

F32 = jnp.float32
BF16 = jnp.bfloat16

EPS = 1e-6
LANES = 128

MLSTM_HEADS = 4
MLSTM_CHUNK = 128
ATTN_HEAD_DIM = 64
ATTN_Q_HEADS = 16
ATTN_KV_HEADS = 4
ATTN_GROUP = ATTN_Q_HEADS // ATTN_KV_HEADS
ATTN_BLOCK = 128
ROPE_THETA = 10000.0

NEG_INF = float("-inf")
LOG2E = 1.4426950408889634
VMEM_LIMIT = 56 * 1024 * 1024


def _params():
    return pltpu.CompilerParams(dimension_semantics=("arbitrary", "arbitrary"),
                                vmem_limit_bytes=VMEM_LIMIT)


def _resident(shape):
    return pl.BlockSpec(shape, lambda *_: (0,) * len(shape), pipeline_mode=pl.Buffered(1))


def _rmsnorm(x, w):
    ms = jnp.mean(x * x, axis=-1, keepdims=True)
    return x * lax.rsqrt(ms + EPS) * w


def _sigmoid(x):
    return 1.0 / (1.0 + jnp.exp(-x))


def _dot(a, b):
    return jnp.dot(a, b, preferred_element_type=F32)


def _dot_nt(a, b):
    return lax.dot_general(a, b, (((1,), (1,)), ((), ())), preferred_element_type=F32)


def _layer_tail_kernel(*refs, d_ff, tf, nsub, mixer, final):
    it = iter(refs)
    if mixer == "mlstm":
        hs_ref, og_ref, hw_ref = next(it), next(it), next(it)
    else:
        a_ref = next(it)
    x_ref, wo_ref, nw_ref, win_ref, wout_ref = (next(it) for _ in range(5))
    fw_ref = next(it) if final else None
    o_ref, hn_ref, act_ref = (next(it) for _ in range(3))

    tm = x_ref.shape[1]
    subs = [slice(r * (tm // nsub), (r + 1) * (tm // nsub)) for r in range(nsub)]
    for rs in subs:
        if mixer == "mlstm":
            dv = hs_ref.shape[-1] // MLSTM_HEADS
            parts = []
            for h in range(MLSTM_HEADS):
                hs = hs_ref[0, rs, h * dv:(h + 1) * dv]
                ms = jnp.mean(hs * hs, axis=-1, keepdims=True)
                parts.append(hs * lax.rsqrt(ms + EPS))
            y = (jnp.concatenate(parts, axis=-1) * hw_ref[...] * og_ref[0, rs, :].astype(F32)).astype(BF16)
        else:
            y = a_ref[0, rs, :]
        o_ref[0, rs, :] = x_ref[0, rs, :] + _dot(y, wo_ref[...])
    for rs in subs:
        hn_ref[rs, :] = _rmsnorm(o_ref[0, rs, :], nw_ref[...]).astype(BF16)
    for rs in subs:
        for j in range(d_ff // tf):
            hn = hn_ref[rs, :]
            g = _dot(hn, win_ref[:, j * tf:(j + 1) * tf])
            u = _dot(hn, win_ref[:, d_ff + j * tf:d_ff + (j + 1) * tf])
            act_ref[rs, j * tf:(j + 1) * tf] = (g * _sigmoid(g) * u).astype(BF16)
    for rs in subs:
        out = o_ref[0, rs, :] + _dot(act_ref[rs, :], wout_ref[...])
        if final:
            out = _rmsnorm(out, fw_ref[...])
        o_ref[0, rs, :] = out


def _layer_tail(mixer, mixer_args, x, w_o, layer_o, norm_w, w_in, w_out, final_w=None,
                *, tm=1024, tf=256, nsub=4):
    B, S, D = x.shape
    tm = min(tm, S)
    d_ff = w_out.shape[0]
    final = final_w is not None
    row = pl.BlockSpec((1, tm, D), lambda b, i: (b, i, 0))

    def stacked(shape, l):
        return pl.BlockSpec((None,) + shape, lambda b, i: (l,) + (0,) * len(shape),
                            pipeline_mode=pl.Buffered(1))

    if mixer == "mlstm":
        hs, og, head_w = mixer_args
        args = [hs, og, head_w.reshape(1, D)]
        in_specs = [row, row, _resident((1, D))]
    else:
        args = list(mixer_args)
        in_specs = [row]
    args += [x, w_o, norm_w.reshape(1, D), w_in, w_out]
    in_specs += [row, stacked((w_o.shape[1], D), layer_o), _resident((1, D)),
                 _resident((D, 2 * d_ff)), _resident((d_ff, D))]
    if final:
        args.append(final_w.reshape(1, D))
        in_specs.append(_resident((1, D)))
    return pl.pallas_call(
        functools.partial(_layer_tail_kernel, d_ff=d_ff, tf=tf, nsub=nsub, mixer=mixer, final=final),
        grid=(B, S // tm),
        in_specs=in_specs,
        out_specs=row,
        out_shape=jax.ShapeDtypeStruct((B, S, D), F32),
        scratch_shapes=[pltpu.VMEM((tm, D), BF16), pltpu.VMEM((tm, d_ff), BF16)],
        compiler_params=_params(),
        name=mixer + "_tail",
    )(*args)


def _mlstm_proj_kernel(x_ref, nw_ref, wt_ref, bias_ref,
                       q_ref, kt_ref, v_ref, og_ref, rf_ref, w_ref, wkt_ref, *, dk, tm, nsub):
    d = v_ref.shape[-1]
    hdk = q_ref.shape[-1]
    ng = rf_ref.shape[1] // (tm // LANES)
    sub = tm // nsub

    @pl.when((pl.program_id(0) == 0) & (pl.program_id(1) == 0))
    def _():
        for c in range(0, hdk, LANES):
            w_ref[:, c:c + LANES] = wt_ref[c:c + LANES, :].T.astype(BF16)
            wkt_ref[c:c + LANES, :] = wt_ref[hdk + c:hdk + c + LANES, :].astype(BF16)
        for c in range(0, 2 * d, LANES):
            w_ref[:, hdk + c:hdk + c + LANES] = wt_ref[2 * hdk + c:2 * hdk + c + LANES, :].T.astype(BF16)
        wkt_ref[hdk:, :] = wt_ref[2 * hdk + 2 * d:, :].astype(BF16)

    for r in range(nsub):
        rs = slice(r * sub, (r + 1) * sub)
        hn = _rmsnorm(x_ref[0, rs, :], nw_ref[...]).astype(BF16)
        og_ref[0, rs, :] = _sigmoid(_dot(hn, w_ref[:, hdk + d:])).astype(BF16)
        v_ref[0, rs, :] = _dot(hn, w_ref[:, hdk:hdk + d]).astype(BF16)
        q_ref[0, rs, :] = (_dot(hn, w_ref[:, :hdk]) * (dk ** -0.5)).astype(BF16)
        kg = _dot_nt(wkt_ref[...], hn)
        kt = kg[:hdk].astype(BF16)
        for jj in range(sub // LANES):
            j = r * (sub // LANES) + jj
            kt_ref[0, j] = kt[:, jj * LANES:(jj + 1) * LANES]
            rf_ref[0, j * ng:(j + 1) * ng, :] = kg[hdk:, jj * LANES:(jj + 1) * LANES] + bias_ref[...]


def _mlstm_proj(x, norm_w, w_t, layer, bias_rows, *, tm=1024, nsub=4):
    B, S, D = x.shape
    H = MLSTM_HEADS
    hdk = D // 2
    dk = hdk // H
    tm = min(tm, S)
    ng = bias_rows.shape[0]
    nj = tm // LANES
    NC = S // LANES
    row = lambda b, i: (b, i, 0)
    return pl.pallas_call(
        functools.partial(_mlstm_proj_kernel, dk=dk, tm=tm, nsub=nsub),
        grid=(B, S // tm),
        in_specs=[pl.BlockSpec((1, tm, D), row), _resident((1, D)),
                  pl.BlockSpec((None,) + w_t.shape[1:], lambda b, i: (layer, 0, 0),
                               pipeline_mode=pl.Buffered(1)),
                  _resident((ng, LANES))],
        out_specs=[pl.BlockSpec((1, tm, hdk), row),
                   pl.BlockSpec((1, nj, hdk, LANES), lambda b, i: (b, i, 0, 0)),
                   pl.BlockSpec((1, tm, D), row),
                   pl.BlockSpec((1, tm, D), row),
                   pl.BlockSpec((1, nj * ng, LANES), row)],
        out_shape=[jax.ShapeDtypeStruct((B, S, hdk), BF16),
                   jax.ShapeDtypeStruct((B, NC, hdk, LANES), BF16),
                   jax.ShapeDtypeStruct((B, S, D), BF16),
                   jax.ShapeDtypeStruct((B, S, D), BF16),
                   jax.ShapeDtypeStruct((B, NC * ng, LANES), F32)],
        scratch_shapes=[pltpu.VMEM((D, hdk + 2 * D), BF16), pltpu.VMEM((hdk + ng, D), BF16)],
        compiler_params=_params(),
        name="mlstm_proj",
    )(x, norm_w.reshape(1, D), w_t, bias_rows)


FFN_CAST_BLOCKS = 8


def _ffn_cast_specs(ffn_w_in, ffn_w_out, layer, step_of, total_steps):
    nblk = min(FFN_CAST_BLOCKS, total_steps)
    rep = total_steps // nblk
    args, in_specs, out_specs, out_shape = [], [], [], []
    for w in (ffn_w_in, ffn_w_out):
        rows, cols = w.shape[1] // nblk, w.shape[2]
        args.append(w)
        in_specs.append(pl.BlockSpec((None, rows, cols),
                                     lambda *g, l=layer: (l, step_of(*g) // rep, 0)))
        out_specs.append(pl.BlockSpec((rows, cols), lambda *g: (step_of(*g) // rep, 0)))
        out_shape.append(jax.ShapeDtypeStruct(w.shape[1:], BF16))
    return rep, args, in_specs, out_specs, out_shape


def _ffn_cast(step, rep, srcs, dsts):
    @pl.when(step % rep == 0)
    def _():
        for src, dst in zip(srcs, dsts):
            dst[...] = src[...].astype(BF16)


def _log_sigmoid(x):
    return jnp.minimum(x, 0.0) - jnp.log1p(jnp.exp(-jnp.abs(x)))


def _mlstm_core_kernel(q_ref, kt_ref, v_ref, rf_ref, wi_ref, wo_ref, o_ref, wi_out, wo_out, st_ref, c_ref,
                       *, nc, dv, unroll, cast_rep):
    L = LANES
    h = pl.program_id(1)
    ng = 4 * MLSTM_HEADS
    _ffn_cast(pl.program_id(0) * pl.num_programs(1) + h, cast_rep, (wi_ref, wo_ref), (wi_out, wo_out))

    def gate_rows(g):
        return rf_ref[0, pl.ds(MLSTM_HEADS * g + h, nc, stride=ng), :]

    lane = lax.broadcasted_iota(jnp.int32, (nc, L), 1)

    def prefix_sum(x):
        for s in (1, 2, 4, 8, 16, 32, 64):
            x = x + jnp.where(lane >= s, pltpu.roll(x, s, axis=1), 0.0)
        return x

    def suffix_sum(x):
        for s in (1, 2, 4, 8, 16, 32, 64):
            x = x + jnp.where(lane < L - s, pltpu.roll(x, L - s, axis=1), 0.0)
        return x

    for d in range(2):
        log_i = gate_rows(2 * d)
        log_f = _log_sigmoid(gate_rows(2 * d + 1))
        if d == 0:
            b = prefix_sum(log_f)
            b_last = jnp.broadcast_to(b[:, L - 1:L], (nc, L))
        else:
            b = suffix_sum(log_f)
            b_last = jnp.broadcast_to(b[:, 0:1], (nc, L))
        r = log_i - b
        a = b_last + r
        a_max = jnp.broadcast_to(jnp.max(a, axis=1, keepdims=True), (nc, L))
        st_ref[d, 0] = log_f
        st_ref[d, 1] = r
        st_ref[d, 2] = jnp.exp(a - a_max)
        st_ref[d, 3] = b_last
        st_ref[d, 4] = a_max

    def m_scan(i, carry):
        new = []
        for d, c in ((0, i), (1, nc - 1 - i)):
            row = pl.ds(c, 1)
            st_ref[d, 5, row, :] = carry[d]
            new.append(jnp.maximum(st_ref[d, 3, row, :] + carry[d], st_ref[d, 4, row, :]))
        return tuple(new)

    m0 = jnp.zeros((1, L), F32)
    lax.fori_loop(0, nc, m_scan, (m0, m0))
    for d in range(2):
        b_last, a_max, m_prev = st_ref[d, 3], st_ref[d, 4], st_ref[d, 5]
        m_new = jnp.maximum(b_last + m_prev, a_max)
        st_ref[d, 3] = jnp.exp(b_last + m_prev - m_new)
        st_ref[d, 2] = st_ref[d, 2] * jnp.exp(a_max - m_new)
        st_ref[d, 0] = st_ref[d, 0] * LOG2E
        st_ref[d, 1] = st_ref[d, 1] * LOG2E
        st_ref[d, 5] = m_prev * LOG2E

    c_ref[...] = jnp.zeros_like(c_ref)

    t_idx = lax.broadcasted_iota(jnp.int32, (L, L), 0)
    s_idx = lax.broadcasted_iota(jnp.int32, (L, L), 1)
    masks = (s_idx <= t_idx, s_idx >= t_idx)
    ones = jnp.ones((L, L), BF16)

    def tile3(x):
        return jnp.concatenate([x] * (dv // L + 1), axis=1)

    def body(i, carry, assign):
        items = []
        for u in range(unroll):
            c = i * unroll + u
            items += [(0, c), (1, nc - 1 - c)]

        work = []
        for d, c in items:
            rows = pl.ds(pl.multiple_of(c * L, L), L)
            qc = q_ref[0, rows, :]
            kt = kt_ref[0, c]
            vaug = jnp.concatenate([v_ref[0, rows, :], ones], axis=1)
            w = st_ref[d, 2, pl.ds(c, 1), :]
            scores = _dot(qc, kt)
            kv = _dot(kt * w.astype(BF16), vaug)
            work.append((rows, qc, vaug, scores, kv))

        states = [c_ref[0], c_ref[1]]
        prev_states = []
        for (d, c), (rows, qc, vaug, scores, kv) in zip(items, work):
            prev_states.append(states[d].astype(BF16))
            states[d] = tile3(st_ref[d, 3, pl.ds(c, 1), :]) * states[d] + kv
        c_ref[0] = states[0]
        c_ref[1] = states[1]

        for (d, c), (rows, qc, vaug, scores, kv), prev_state in zip(items, work, prev_states):
            log_f = st_ref[d, 0, pl.ds(c, 1), :]
            r = st_ref[d, 1, pl.ds(c, 1), :]
            m_prev = st_ref[d, 5, pl.ds(c, 1), :]
            mask = masks[d]
            r_masked = jnp.where(mask, r, NEG_INF)
            cm = jnp.max(r_masked, axis=1, keepdims=True)
            b_col = jnp.sum(jnp.where(mask, log_f, 0.0), axis=1, keepdims=True)
            mu = jnp.broadcast_to(jnp.maximum(m_prev[:, 0:1], cm), (L, L))
            decay = jnp.exp2(r_masked - mu)
            inter = jnp.exp2(m_prev - mu)
            lhs = jnp.concatenate([(scores * decay).astype(BF16), inter.astype(BF16) * qc], axis=1)
            both = _dot(lhs, jnp.concatenate([vaug, prev_state], axis=0))
            den = jnp.maximum(jnp.abs(both[:, dv:]), jnp.exp2(-(b_col + mu)))
            out = both[:, :dv] * tile3(1.0 / den)[:, :dv]
            if assign:
                o_ref[0, rows, :] = out
            else:
                o_ref[0, rows, :] += out
        return carry

    steps = nc // 2 // unroll
    lax.fori_loop(0, steps, functools.partial(body, assign=True), 0)
    lax.fori_loop(steps, 2 * steps, functools.partial(body, assign=False), 0)


def _mlstm_core(q, kt, v, rf, ffn_w_in, ffn_w_out, layer, *, unroll=4):
    B, S, hdk = q.shape
    H = MLSTM_HEADS
    dk = hdk // H
    dv = v.shape[-1] // H
    nc = S // LANES
    rep, w_args, w_in_specs, w_out_specs, w_out_shape = _ffn_cast_specs(
        ffn_w_in, ffn_w_out, layer, lambda b, h: b * H + h, B * H)
    return pl.pallas_call(
        functools.partial(_mlstm_core_kernel, nc=nc, dv=dv, unroll=min(unroll, nc // 2), cast_rep=rep),
        grid=(B, H),
        in_specs=[pl.BlockSpec((1, S, dk), lambda b, h: (b, 0, h)),
                  pl.BlockSpec((1, nc, dk, LANES), lambda b, h: (b, 0, h, 0)),
                  pl.BlockSpec((1, S, dv), lambda b, h: (b, 0, h)),
                  pl.BlockSpec((1, rf.shape[1], LANES), lambda b, h: (b, 0, 0))] + w_in_specs,
        out_specs=[pl.BlockSpec((1, S, dv), lambda b, h: (b, 0, h))] + w_out_specs,
        out_shape=[jax.ShapeDtypeStruct((B, S, H * dv), F32)] + w_out_shape,
        scratch_shapes=[pltpu.VMEM((2, 6, nc, LANES), F32),
                        pltpu.VMEM((2, dk, dv + LANES), F32)],
        compiler_params=_params(),
        name="mlstm_core",
    )(q, kt, v, rf, *w_args)


def _attn_weight_prep(w_ref, wt_ref):
    @pl.when((pl.program_id(0) == 0) & (pl.program_id(1) == 0))
    def _():
        for c in range(0, wt_ref.shape[0], LANES):
            wt_ref[c:c + LANES, :] = w_ref[:, c:c + LANES].T.astype(BF16)


def _attn_proj_rows(x_rows, pos_row, invf, nw, wt_ref, qt_ref, k_ref, vt_ref, chunk0):
    hd = ATTN_HEAD_DIM
    half = hd // 2
    nq = ATTN_Q_HEADS * hd
    nk = ATTN_KV_HEADS * hd
    scale = LOG2E * hd ** -0.5
    piece = 8 * hd
    sub = x_rows.shape[0]
    hn = _rmsnorm(x_rows, nw).astype(BF16)
    ang = invf * pos_row.astype(F32)
    cos = jnp.cos(ang)
    sin = jnp.sin(ang)
    chunks = [(chunk0 + jj, slice(jj * LANES, (jj + 1) * LANES)) for jj in range(sub // LANES)]

    def rope(xh):
        x1, x2 = xh[:half], xh[half:]
        return jnp.concatenate([x1 * cos - x2 * sin, x2 * cos + x1 * sin], axis=0)

    qp = _dot_nt(wt_ref[:nq, :], hn)
    kv = _dot_nt(wt_ref[nq:, :], hn)
    for lo in range(0, nq, piece):
        qt = jnp.concatenate([rope(qp[lo + h * hd:lo + (h + 1) * hd]) * scale for h in range(piece // hd)],
                             axis=0).astype(BF16)
        for j, cols in chunks:
            qt_ref[0, j, lo:lo + piece, :] = qt[:, cols]
    vt = kv[nk:].astype(BF16)
    for j, cols in chunks:
        vt_ref[0, j] = vt[:, cols]
    kt = jnp.concatenate([rope(kv[g * hd:(g + 1) * hd]) for g in range(ATTN_KV_HEADS)], axis=0)
    for j, cols in chunks:
        for f in range(nk // LANES):
            k_ref[0, j * LANES:(j + 1) * LANES, f * LANES:(f + 1) * LANES] = (
                kt[f * LANES:(f + 1) * LANES, cols].T.astype(BF16))


def _attn_proj_kernel(x_ref, nw_ref, w_ref, pos_ref, invf_ref, qt_ref, k_ref, vt_ref, wt_ref, *, tm, nsub):
    _attn_weight_prep(w_ref, wt_ref)
    sub = tm // nsub
    for r in range(nsub):
        rs = slice(r * sub, (r + 1) * sub)
        _attn_proj_rows(x_ref[0, rs, :], pos_ref[0, :, rs], invf_ref[...], nw_ref[...], wt_ref,
                        qt_ref, k_ref, vt_ref, r * (sub // LANES))


def _attn_proj_specs(S, D, tm, w_in, layer):
    nq = ATTN_Q_HEADS * ATTN_HEAD_DIM
    nk = ATTN_KV_HEADS * ATTN_HEAD_DIM
    nj = tm // LANES
    nb = S // LANES
    in_specs = [_resident((1, D)),
                pl.BlockSpec((None,) + w_in.shape[1:], lambda b, i: (layer, 0, 0),
                             pipeline_mode=pl.Buffered(1)),
                pl.BlockSpec((1, 1, tm), lambda b, i: (b, 0, i)),
                _resident((ATTN_HEAD_DIM // 2, 1))]
    out_specs = [pl.BlockSpec((1, nj, nq, LANES), lambda b, i: (b, i, 0, 0)),
                 pl.BlockSpec((1, tm, nk), lambda b, i: (b, i, 0)),
                 pl.BlockSpec((1, nj, nk, LANES), lambda b, i: (b, i, 0, 0))]
    out_shape = lambda B: [jax.ShapeDtypeStruct((B, nb, nq, LANES), BF16),
                           jax.ShapeDtypeStruct((B, S, nk), BF16),
                           jax.ShapeDtypeStruct((B, nb, nk, LANES), BF16)]
    scratch = [pltpu.VMEM((nq + 2 * nk, D), BF16)]
    return in_specs, out_specs, out_shape, scratch


def _attn_proj(x, norm_w, w_in, layer, pos_rows, inv_freq, *, tm=1024, nsub=4):
    B, S, D = x.shape
    tm = min(tm, S)
    p_in, p_out, p_shape, p_scratch = _attn_proj_specs(S, D, tm, w_in, layer)
    return pl.pallas_call(
        functools.partial(_attn_proj_kernel, tm=tm, nsub=nsub),
        grid=(B, S // tm),
        in_specs=[pl.BlockSpec((1, tm, D), lambda b, i: (b, i, 0))] + p_in,
        out_specs=p_out,
        out_shape=p_shape(B),
        scratch_shapes=p_scratch,
        compiler_params=_params(),
        name="attn_proj",
    )(x, norm_w.reshape(1, D), w_in, pos_rows, inv_freq)


def _attn_core_kernel(qt_ref, kp_ref, kc_ref, kn_ref, vp_ref, vc_ref, vn_ref, sink_ref, wi_ref, wo_ref,
                      o_ref, wi_out, wo_out, k_all, v_all, *, nb, nblk, cast_rep):
    L = LANES
    hd = ATTN_HEAD_DIM
    G = ATTN_GROUP
    c = pl.program_id(1)
    _ffn_cast(pl.program_id(0) * pl.num_programs(1) + c, cast_rep, (wi_ref, wo_ref), (wi_out, wo_out))
    k_all[0:L] = kp_ref[0]
    k_all[L:(nblk + 1) * L] = kc_ref[0]
    k_all[(nblk + 1) * L:] = kn_ref[0]
    v_all[0] = vp_ref[0, 0]
    v_all[1:nblk + 1] = vc_ref[0]
    v_all[nblk + 1] = vn_ref[0, 0]

    key = lax.broadcasted_iota(jnp.int32, (L, G * L), 0)
    qry = lax.broadcasted_iota(jnp.int32, (L, G * L), 1) % L
    prev_mask = jnp.where(key >= qry, 0.0, NEG_INF)
    next_mask = jnp.where(key <= qry, 0.0, NEG_INF)
    zeros = jnp.zeros((hd, G * L), BF16)
    ones = jnp.ones((16, 3 * L), BF16)

    def scores(j, g):
        qg = jnp.concatenate([qt_ref[0, j, (G * g + i) * hd:(G * g + i + 1) * hd, :] for i in range(G)],
                             axis=1)
        qz = jnp.concatenate([qg, zeros] if g % 2 == 0 else [zeros, qg], axis=0)
        return _dot(k_all[j * L:(j + 3) * L, (g // 2) * L:(g // 2 + 1) * L], qz)

    items = [(j, g) for j in range(nblk) for g in range(ATTN_KV_HEADS)]
    ahead = 2
    pending = [scores(*it) for it in items[:ahead]]
    for n, (j, g) in enumerate(items):
        s = pending.pop(0)
        if n + ahead < len(items):
            pending.append(scores(*items[n + ahead]))
        if g == 0:
            blk = c * nblk + j
            prev_bias = prev_mask + jnp.where(blk > 0, 0.0, NEG_INF)
            next_bias = next_mask + jnp.where(blk < nb - 1, 0.0, NEG_INF)
        s = [s[0:L] + prev_bias, s[L:2 * L], s[2 * L:] + next_bias]
        sink = sink_ref[g:g + 1, :]
        m8 = functools.reduce(jnp.maximum, [sj.reshape(L // 8, 8, G * L).max(axis=0) for sj in s])
        m = jnp.maximum(sink, jnp.max(m8, axis=0, keepdims=True))
        p = jnp.concatenate([jnp.exp2(sj - m).astype(BF16) for sj in s], axis=0)
        vg = jnp.concatenate([v_all[j + i, g * hd:(g + 1) * hd, :] for i in range(3)], axis=1)
        acc = _dot(jnp.concatenate([vg, ones], axis=0), p)
        denom = acc[hd:hd + 1] + jnp.exp2(sink - m)
        out = acc[:hd] * (1.0 / denom)
        for pair in range(G // 2):
            tile = jnp.concatenate([out[:, (2 * pair) * L:(2 * pair + 1) * L],
                                    out[:, (2 * pair + 1) * L:(2 * pair + 2) * L]], axis=0)
            col = (G * g + 2 * pair) * hd
            o_ref[0, j * L:(j + 1) * L, col:col + L] = tile.T.astype(BF16)


def _attn_core(qt, k, vt, sink_rows, ffn_w_in, ffn_w_out, layer, *, nblk=8):
    B, nb, nq, L = qt.shape
    nk = k.shape[-1]
    S = nb * L
    nblk = min(nblk, nb)
    steps = nb // nblk
    prev = lambda b, c: (b, jnp.maximum(c * nblk - 1, 0), 0)
    cur = lambda b, c: (b, c, 0)
    nxt = lambda b, c: (b, jnp.minimum((c + 1) * nblk, nb - 1), 0)
    four = lambda f: (lambda b, c: f(b, c) + (0,))
    rep, w_args, w_in_specs, w_out_specs, w_out_shape = _ffn_cast_specs(
        ffn_w_in, ffn_w_out, layer, lambda b, c: b * steps + c, B * steps)
    return pl.pallas_call(
        functools.partial(_attn_core_kernel, nb=nb, nblk=nblk, cast_rep=rep),
        grid=(B, steps),
        in_specs=[pl.BlockSpec((1, nblk, nq, L), four(cur)),
                  pl.BlockSpec((1, L, nk), prev), pl.BlockSpec((1, nblk * L, nk), cur),
                  pl.BlockSpec((1, L, nk), nxt),
                  pl.BlockSpec((1, 1, nk, L), four(prev)), pl.BlockSpec((1, nblk, nk, L), four(cur)),
                  pl.BlockSpec((1, 1, nk, L), four(nxt)),
                  _resident(sink_rows.shape)] + w_in_specs,
        out_specs=[pl.BlockSpec((1, nblk * L, nq), cur)] + w_out_specs,
        out_shape=[jax.ShapeDtypeStruct((B, S, nq), BF16)] + w_out_shape,
        scratch_shapes=[pltpu.VMEM(((nblk + 2) * L, nk), BF16),
                        pltpu.VMEM((nblk + 2, nk, L), BF16)],
        compiler_params=_params(),
        name="attn_core",
    )(qt, k, k, k, vt, vt, vt, sink_rows, *w_args)


def _mlstm_mixer(x, norm_w, w_t, layer, b_gate, ffn_w_in, ffn_w_out, ffn_layer):
    bias_rows = jnp.broadcast_to(b_gate.astype(F32).reshape(-1, 1), (b_gate.shape[0], LANES))
    q, kt, v, og, rf = _mlstm_proj(x, norm_w, w_t, layer, bias_rows)
    hs, wi, wo = _mlstm_core(q, kt, v, rf, ffn_w_in, ffn_w_out, ffn_layer)
    return (hs, og), wi, wo


def _attn_mixer(x, pos_rows, inv_freq, norm_w, w_in, layer, sink, ffn_w_in, ffn_w_out, ffn_layer):
    G, L = ATTN_GROUP, LANES
    qt, k, vt = _attn_proj(x, norm_w, w_in, layer, pos_rows, inv_freq)
    sink_rows = jnp.repeat(LOG2E * sink.astype(F32).reshape(ATTN_KV_HEADS, G), L, axis=1)
    a, wi, wo = _attn_core(qt, k, vt, sink_rows, ffn_w_in, ffn_w_out, ffn_layer)
    return (a,), wi, wo


def kernel(x, positions, norm_mix_w, norm_ffn_w, norm_final_w, mlstm_w_in, mlstm_b_gate, mlstm_norm_w, mlstm_w_out, attn_w_in, attn_sink, attn_w_out, ffn_w_in, ffn_w_out):
    depth = norm_mix_w.shape[0]
    B, S = positions.shape
    half = ATTN_HEAD_DIM // 2
    inv_freq = (ROPE_THETA ** (-jnp.arange(half, dtype=F32) / half)).reshape(half, 1)
    pos_rows = positions.reshape(B, 1, S)
    mlstm_wo, attn_wo = mlstm_w_out.astype(BF16), attn_w_out.astype(BF16)
    mlstm_wt = jnp.swapaxes(mlstm_w_in, 1, 2)
    for i in range(depth):
        j = i // 2
        final_w = norm_final_w if i == depth - 1 else None
        if i % 2 == 0:
            (hs, og), wi, wo = _mlstm_mixer(x, norm_mix_w[i], mlstm_wt, j, mlstm_b_gate[j],
                                            ffn_w_in, ffn_w_out, i)
            x = _layer_tail("mlstm", (hs, og, mlstm_norm_w[j]), x, mlstm_wo, j, norm_ffn_w[i], wi, wo,
                            final_w)
        else:
            mixed, wi, wo = _attn_mixer(x, pos_rows, inv_freq, norm_mix_w[i], attn_w_in, j, attn_sink[j],
                                        ffn_w_in, ffn_w_out, i)
            x = _layer_tail("attn", mixed, x, attn_wo, j, norm_ffn_w[i], wi, wo, final_w)
    return x
```

```python
import functools

import jax
import jax.numpy as jnp
from jax import lax
from jax.experimental import pallas as pl
from jax.experimental.pallas import tpu as pltpu

F32 = jnp.float32
BF16 = jnp.bfloat16

EPS = 1e-6
LANES = 128

MLSTM_HEADS = 4
MLSTM_CHUNK = 128
ATTN_HEAD_DIM = 64
ATTN_Q_HEADS = 16
ATTN_KV_HEADS = 4
ATTN_GROUP = ATTN_Q_HEADS // ATTN_KV_HEADS
ATTN_BLOCK = 128
ROPE_THETA = 10000.0

NEG_INF = float("-inf")
LOG2E = 1.4426950408889634
VMEM_LIMIT = 56 * 1024 * 1024


def _params():
    return pltpu.CompilerParams(dimension_semantics=("arbitrary", "arbitrary"),
                                vmem_limit_bytes=VMEM_LIMIT)


def _resident(shape):
    return pl.BlockSpec(shape, lambda *_: (0,) * len(shape), pipeline_mode=pl.Buffered(1))


def _rmsnorm(x, w):
    ms = jnp.mean(x * x, axis=-1, keepdims=True)
    return x * lax.rsqrt(ms + EPS) * w


def _sigmoid(x):
    return 1.0 / (1.0 + jnp.exp(-x))


def _dot(a, b):
    return jnp.dot(a, b, preferred_element_type=F32)


def _dot_nt(a, b):
    return lax.dot_general(a, b, (((1,), (1,)), ((), ())), preferred_element_type=F32)


def _layer_tail_kernel(*refs, d_ff, tf, nsub, mixer, final):
    it = iter(refs)
    if mixer == "mlstm":
        hs_ref, og_ref, hw_ref = next(it), next(it), next(it)
    else:
        a_ref = next(it)
    x_ref, wo_ref, nw_ref, win_ref, wout_ref = (next(it) for _ in range(5))
    fw_ref = next(it) if final else None
    o_ref, hn_ref, act_ref = (next(it) for _ in range(3))

    tm = x_ref.shape[1]
    subs = [slice(r * (tm // nsub), (r + 1) * (tm // nsub)) for r in range(nsub)]
    for rs in subs:
        if mixer == "mlstm":
            dv = hs_ref.shape[-1] // MLSTM_HEADS
            parts = []
            for h in range(MLSTM_HEADS):
                hs = hs_ref[0, rs, h * dv:(h + 1) * dv]
                ms = jnp.mean(hs * hs, axis=-1, keepdims=True)
                parts.append(hs * lax.rsqrt(ms + EPS))
            y = (jnp.concatenate(parts, axis=-1) * hw_ref[...] * og_ref[0, rs, :].astype(F32)).astype(BF16)
        else:
            y = a_ref[0, rs, :]
        o_ref[0, rs, :] = x_ref[0, rs, :] + _dot(y, wo_ref[...])
    for rs in subs:
        hn_ref[rs, :] = _rmsnorm(o_ref[0, rs, :], nw_ref[...]).astype(BF16)
    for rs in subs:
        for j in range(d_ff // tf):
            hn = hn_ref[rs, :]
            g = _dot(hn, win_ref[:, j * tf:(j + 1) * tf])
            u = _dot(hn, win_ref[:, d_ff + j * tf:d_ff + (j + 1) * tf])
            act_ref[rs, j * tf:(j + 1) * tf] = (g * _sigmoid(g) * u).astype(BF16)
    for rs in subs:
        out = o_ref[0, rs, :] + _dot(act_ref[rs, :], wout_ref[...])
        if final:
            out = _rmsnorm(out, fw_ref[...])
        o_ref[0, rs, :] = out


def _layer_tail(mixer, mixer_args, x, w_o, layer_o, norm_w, w_in, w_out, final_w=None,
                *, tf=256):
    B, S, D = x.shape
    tm, nsub = (1024, 4) if mixer == "attn" else (512, 2)
    tm = min(tm, S)
    d_ff = w_out.shape[0]
    final = final_w is not None
    row = pl.BlockSpec((1, tm, D), lambda b, i: (b, i, 0))

    def stacked(shape, l):
        return pl.BlockSpec((None,) + shape, lambda b, i: (l,) + (0,) * len(shape),
                            pipeline_mode=pl.Buffered(1))

    if mixer == "mlstm":
        hs, og, head_w = mixer_args
        args = [hs, og, head_w.reshape(1, D)]
        in_specs = [row, row, _resident((1, D))]
    else:
        args = list(mixer_args)
        in_specs = [row]
    args += [x, w_o, norm_w.reshape(1, D), w_in, w_out]
    in_specs += [row, stacked((w_o.shape[1], D), layer_o), _resident((1, D)),
                 _resident((D, 2 * d_ff)), _resident((d_ff, D))]
    if final:
        args.append(final_w.reshape(1, D))
        in_specs.append(_resident((1, D)))
    return pl.pallas_call(
        functools.partial(_layer_tail_kernel, d_ff=d_ff, tf=tf, nsub=nsub, mixer=mixer, final=final),
        grid=(B, S // tm),
        in_specs=in_specs,
        out_specs=row,
        out_shape=jax.ShapeDtypeStruct((B, S, D), F32),
        scratch_shapes=[pltpu.VMEM((tm, D), BF16), pltpu.VMEM((tm, d_ff), BF16)],
        compiler_params=_params(),
        name=mixer + "_tail",
    )(*args)


def _mlstm_proj_kernel(x_ref, nw_ref, wt_ref, bias_ref,
                       q_ref, kt_ref, v_ref, og_ref, rf_ref, w_ref, wkt_ref, *, dk, tm, nsub):
    d = v_ref.shape[-1]
    hdk = q_ref.shape[-1]
    ng = rf_ref.shape[1] // (tm // LANES)
    sub = tm // nsub

    @pl.when((pl.program_id(0) == 0) & (pl.program_id(1) == 0))
    def _():
        for c in range(0, hdk, LANES):
            w_ref[:, c:c + LANES] = wt_ref[c:c + LANES, :].T.astype(BF16)
            wkt_ref[c:c + LANES, :] = wt_ref[hdk + c:hdk + c + LANES, :].astype(BF16)
        for c in range(0, 2 * d, LANES):
            w_ref[:, hdk + c:hdk + c + LANES] = wt_ref[2 * hdk + c:2 * hdk + c + LANES, :].T.astype(BF16)
        wkt_ref[hdk:, :] = wt_ref[2 * hdk + 2 * d:, :].astype(BF16)

    for r in range(nsub):
        rs = slice(r * sub, (r + 1) * sub)
        hn = _rmsnorm(x_ref[0, rs, :], nw_ref[...]).astype(BF16)
        og_ref[0, rs, :] = _sigmoid(_dot(hn, w_ref[:, hdk + d:])).astype(BF16)
        v_ref[0, rs, :] = _dot(hn, w_ref[:, hdk:hdk + d]).astype(BF16)
        q_ref[0, rs, :] = (_dot(hn, w_ref[:, :hdk]) * (dk ** -0.5)).astype(BF16)
        kg = _dot_nt(wkt_ref[...], hn)
        kt = kg[:hdk].astype(BF16)
        for jj in range(sub // LANES):
            j = r * (sub // LANES) + jj
            kt_ref[0, j] = kt[:, jj * LANES:(jj + 1) * LANES]
            rf_ref[0, j * ng:(j + 1) * ng, :] = kg[hdk:, jj * LANES:(jj + 1) * LANES] + bias_ref[...]


def _mlstm_proj(x, norm_w, w_t, layer, bias_rows, *, tm=1024, nsub=4):
    B, S, D = x.shape
    H = MLSTM_HEADS
    hdk = D // 2
    dk = hdk // H
    tm = min(tm, S)
    ng = bias_rows.shape[0]
    nj = tm // LANES
    NC = S // LANES
    row = lambda b, i: (b, i, 0)
    return pl.pallas_call(
        functools.partial(_mlstm_proj_kernel, dk=dk, tm=tm, nsub=nsub),
        grid=(B, S // tm),
        in_specs=[pl.BlockSpec((1, tm, D), row), _resident((1, D)),
                  pl.BlockSpec((None,) + w_t.shape[1:], lambda b, i: (layer, 0, 0),
                               pipeline_mode=pl.Buffered(1)),
                  _resident((ng, LANES))],
        out_specs=[pl.BlockSpec((1, tm, hdk), row),
                   pl.BlockSpec((1, nj, hdk, LANES), lambda b, i: (b, i, 0, 0)),
                   pl.BlockSpec((1, tm, D), row),
                   pl.BlockSpec((1, tm, D), row),
                   pl.BlockSpec((1, nj * ng, LANES), row)],
        out_shape=[jax.ShapeDtypeStruct((B, S, hdk), BF16),
                   jax.ShapeDtypeStruct((B, NC, hdk, LANES), BF16),
                   jax.ShapeDtypeStruct((B, S, D), BF16),
                   jax.ShapeDtypeStruct((B, S, D), BF16),
                   jax.ShapeDtypeStruct((B, NC * ng, LANES), F32)],
        scratch_shapes=[pltpu.VMEM((D, hdk + 2 * D), BF16), pltpu.VMEM((hdk + ng, D), BF16)],
        compiler_params=_params(),
        name="mlstm_proj",
    )(x, norm_w.reshape(1, D), w_t, bias_rows)


FFN_CAST_BLOCKS = 8


def _ffn_cast_specs(ffn_w_in, ffn_w_out, layer, step_of, total_steps):
    nblk = min(FFN_CAST_BLOCKS, total_steps)
    rep = total_steps // nblk
    args, in_specs, out_specs, out_shape = [], [], [], []
    for w in (ffn_w_in, ffn_w_out):
        rows, cols = w.shape[1] // nblk, w.shape[2]
        args.append(w)
        in_specs.append(pl.BlockSpec((None, rows, cols),
                                     lambda *g, l=layer: (l, step_of(*g) // rep, 0)))
        out_specs.append(pl.BlockSpec((rows, cols), lambda *g: (step_of(*g) // rep, 0)))
        out_shape.append(jax.ShapeDtypeStruct(w.shape[1:], BF16))
    return rep, args, in_specs, out_specs, out_shape


def _ffn_cast(step, rep, srcs, dsts):
    @pl.when(step % rep == 0)
    def _():
        for src, dst in zip(srcs, dsts):
            dst[...] = src[...].astype(BF16)


def _log_sigmoid(x):
    return jnp.minimum(x, 0.0) - jnp.log1p(jnp.exp(-jnp.abs(x)))


def _mlstm_core_kernel(q_ref, kt_ref, v_ref, rf_ref, wi_ref, wo_ref, o_ref, wi_out, wo_out, st_ref, c_ref,
                       *, nc, dv, unroll, cast_rep):
    L = LANES
    h = pl.program_id(1)
    ng = 4 * MLSTM_HEADS
    _ffn_cast(pl.program_id(0) * pl.num_programs(1) + h, cast_rep, (wi_ref, wo_ref), (wi_out, wo_out))

    def gate_rows(g):
        return rf_ref[0, pl.ds(MLSTM_HEADS * g + h, nc, stride=ng), :]

    lane = lax.broadcasted_iota(jnp.int32, (nc, L), 1)

    def prefix_sum(x):
        for s in (1, 2, 4, 8, 16, 32, 64):
            x = x + jnp.where(lane >= s, pltpu.roll(x, s, axis=1), 0.0)
        return x

    def suffix_sum(x):
        for s in (1, 2, 4, 8, 16, 32, 64):
            x = x + jnp.where(lane < L - s, pltpu.roll(x, L - s, axis=1), 0.0)
        return x

    for d in range(2):
        log_i = gate_rows(2 * d)
        log_f = _log_sigmoid(gate_rows(2 * d + 1))
        if d == 0:
            b = prefix_sum(log_f)
            b_last = jnp.broadcast_to(b[:, L - 1:L], (nc, L))
        else:
            b = suffix_sum(log_f)
            b_last = jnp.broadcast_to(b[:, 0:1], (nc, L))
        r = log_i - b
        a = b_last + r
        a_max = jnp.broadcast_to(jnp.max(a, axis=1, keepdims=True), (nc, L))
        st_ref[d, 0] = log_f
        st_ref[d, 1] = r
        st_ref[d, 2] = jnp.exp(a - a_max)
        st_ref[d, 3] = b_last
        st_ref[d, 4] = a_max

    def m_scan(i, carry):
        new = []
        for d, c in ((0, i), (1, nc - 1 - i)):
            row = pl.ds(c, 1)
            st_ref[d, 5, row, :] = carry[d]
            new.append(jnp.maximum(st_ref[d, 3, row, :] + carry[d], st_ref[d, 4, row, :]))
        return tuple(new)

    m0 = jnp.zeros((1, L), F32)
    lax.fori_loop(0, nc, m_scan, (m0, m0))
    for d in range(2):
        b_last, a_max, m_prev = st_ref[d, 3], st_ref[d, 4], st_ref[d, 5]
        m_new = jnp.maximum(b_last + m_prev, a_max)
        st_ref[d, 3] = jnp.exp(b_last + m_prev - m_new)
        st_ref[d, 2] = st_ref[d, 2] * jnp.exp(a_max - m_new)
        st_ref[d, 0] = st_ref[d, 0] * LOG2E
        st_ref[d, 1] = st_ref[d, 1] * LOG2E
        st_ref[d, 5] = m_prev * LOG2E

    c_ref[...] = jnp.zeros_like(c_ref)

    t_idx = lax.broadcasted_iota(jnp.int32, (L, L), 0)
    s_idx = lax.broadcasted_iota(jnp.int32, (L, L), 1)
    masks = (s_idx <= t_idx, s_idx >= t_idx)
    ones = jnp.ones((L, L), BF16)

    def tile3(x):
        return jnp.concatenate([x] * (dv // L + 1), axis=1)

    def body(i, carry, assign):
        items = []
        for u in range(unroll):
            c = i * unroll + u
            items += [(0, c), (1, nc - 1 - c)]

        work = []
        for d, c in items:
            rows = pl.ds(pl.multiple_of(c * L, L), L)
            qc = q_ref[0, rows, :]
            kt = kt_ref[0, c]
            vaug = jnp.concatenate([v_ref[0, rows, :], ones], axis=1)
            w = st_ref[d, 2, pl.ds(c, 1), :]
            scores = _dot(qc, kt)
            kv = _dot(kt * w.astype(BF16), vaug)
            work.append((rows, qc, vaug, scores, kv))

        states = [c_ref[0], c_ref[1]]
        prev_states = []
        for (d, c), (rows, qc, vaug, scores, kv) in zip(items, work):
            prev_states.append(states[d].astype(BF16))
            states[d] = tile3(st_ref[d, 3, pl.ds(c, 1), :]) * states[d] + kv
        c_ref[0] = states[0]
        c_ref[1] = states[1]

        for (d, c), (rows, qc, vaug, scores, kv), prev_state in zip(items, work, prev_states):
            log_f = st_ref[d, 0, pl.ds(c, 1), :]
            r = st_ref[d, 1, pl.ds(c, 1), :]
            m_prev = st_ref[d, 5, pl.ds(c, 1), :]
            mask = masks[d]
            r_masked = jnp.where(mask, r, NEG_INF)
            cm = jnp.max(r_masked, axis=1, keepdims=True)
            b_col = jnp.sum(jnp.where(mask, log_f, 0.0), axis=1, keepdims=True)
            mu = jnp.broadcast_to(jnp.maximum(m_prev[:, 0:1], cm), (L, L))
            decay = jnp.exp2(r_masked - mu)
            inter = jnp.exp2(m_prev - mu)
            lhs = jnp.concatenate([(scores * decay).astype(BF16), inter.astype(BF16) * qc], axis=1)
            both = _dot(lhs, jnp.concatenate([vaug, prev_state], axis=0))
            den = jnp.maximum(jnp.abs(both[:, dv:]), jnp.exp2(-(b_col + mu)))
            out = both[:, :dv] * tile3(1.0 / den)[:, :dv]
            if assign:
                o_ref[0, rows, :] = out
            else:
                o_ref[0, rows, :] += out
        return carry

    steps = nc // 2 // unroll
    lax.fori_loop(0, steps, functools.partial(body, assign=True), 0)
    lax.fori_loop(steps, 2 * steps, functools.partial(body, assign=False), 0)


def _mlstm_core(q, kt, v, rf, ffn_w_in, ffn_w_out, layer, *, unroll=4):
    B, S, hdk = q.shape
    H = MLSTM_HEADS
    dk = hdk // H
    dv = v.shape[-1] // H
    nc = S // LANES
    rep, w_args, w_in_specs, w_out_specs, w_out_shape = _ffn_cast_specs(
        ffn_w_in, ffn_w_out, layer, lambda b, h: b * H + h, B * H)
    return pl.pallas_call(
        functools.partial(_mlstm_core_kernel, nc=nc, dv=dv, unroll=min(unroll, nc // 2), cast_rep=rep),
        grid=(B, H),
        in_specs=[pl.BlockSpec((1, S, dk), lambda b, h: (b, 0, h)),
                  pl.BlockSpec((1, nc, dk, LANES), lambda b, h: (b, 0, h, 0)),
                  pl.BlockSpec((1, S, dv), lambda b, h: (b, 0, h)),
                  pl.BlockSpec((1, rf.shape[1], LANES), lambda b, h: (b, 0, 0))] + w_in_specs,
        out_specs=[pl.BlockSpec((1, S, dv), lambda b, h: (b, 0, h))] + w_out_specs,
        out_shape=[jax.ShapeDtypeStruct((B, S, H * dv), F32)] + w_out_shape,
        scratch_shapes=[pltpu.VMEM((2, 6, nc, LANES), F32),
                        pltpu.VMEM((2, dk, dv + LANES), F32)],
        compiler_params=_params(),
        name="mlstm_core",
    )(q, kt, v, rf, *w_args)


def _attn_weight_prep(w_ref, wt_ref):
    @pl.when((pl.program_id(0) == 0) & (pl.program_id(1) == 0))
    def _():
        for c in range(0, wt_ref.shape[0], LANES):
            wt_ref[c:c + LANES, :] = w_ref[:, c:c + LANES].T.astype(BF16)


def _attn_proj_rows(x_rows, pos_row, invf, nw, wt_ref, qt_ref, k_ref, vt_ref, chunk0):
    hd = ATTN_HEAD_DIM
    half = hd // 2
    nq = ATTN_Q_HEADS * hd
    nk = ATTN_KV_HEADS * hd
    scale = LOG2E * hd ** -0.5
    piece = 8 * hd
    sub = x_rows.shape[0]
    hn = _rmsnorm(x_rows, nw).astype(BF16)
    ang = invf * pos_row.astype(F32)
    cos = jnp.cos(ang)
    sin = jnp.sin(ang)
    chunks = [(chunk0 + jj, slice(jj * LANES, (jj + 1) * LANES)) for jj in range(sub // LANES)]

    def rope(xh):
        x1, x2 = xh[:half], xh[half:]
        return jnp.concatenate([x1 * cos - x2 * sin, x2 * cos + x1 * sin], axis=0)

    qp = _dot_nt(wt_ref[:nq, :], hn)
    kv = _dot_nt(wt_ref[nq:, :], hn)
    for lo in range(0, nq, piece):
        qt = jnp.concatenate([rope(qp[lo + h * hd:lo + (h + 1) * hd]) * scale for h in range(piece // hd)],
                             axis=0).astype(BF16)
        for j, cols in chunks:
            qt_ref[0, j, lo:lo + piece, :] = qt[:, cols]
    vt = kv[nk:].astype(BF16)
    for j, cols in chunks:
        vt_ref[0, j] = vt[:, cols]
    kt = jnp.concatenate([rope(kv[g * hd:(g + 1) * hd]) for g in range(ATTN_KV_HEADS)], axis=0)
    for j, cols in chunks:
        for f in range(nk // LANES):
            k_ref[0, j * LANES:(j + 1) * LANES, f * LANES:(f + 1) * LANES] = (
                kt[f * LANES:(f + 1) * LANES, cols].T.astype(BF16))


def _attn_proj_kernel(x_ref, nw_ref, w_ref, pos_ref, invf_ref, qt_ref, k_ref, vt_ref, wt_ref, *, tm, nsub):
    _attn_weight_prep(w_ref, wt_ref)
    sub = tm // nsub
    for r in range(nsub):
        rs = slice(r * sub, (r + 1) * sub)
        _attn_proj_rows(x_ref[0, rs, :], pos_ref[0, :, rs], invf_ref[...], nw_ref[...], wt_ref,
                        qt_ref, k_ref, vt_ref, r * (sub // LANES))


def _attn_proj_specs(S, D, tm, w_in, layer):
    nq = ATTN_Q_HEADS * ATTN_HEAD_DIM
    nk = ATTN_KV_HEADS * ATTN_HEAD_DIM
    nj = tm // LANES
    nb = S // LANES
    in_specs = [_resident((1, D)),
                pl.BlockSpec((None,) + w_in.shape[1:], lambda b, i: (layer, 0, 0),
                             pipeline_mode=pl.Buffered(1)),
                pl.BlockSpec((1, 1, tm), lambda b, i: (b, 0, i)),
                _resident((ATTN_HEAD_DIM // 2, 1))]
    out_specs = [pl.BlockSpec((1, nj, nq, LANES), lambda b, i: (b, i, 0, 0)),
                 pl.BlockSpec((1, tm, nk), lambda b, i: (b, i, 0)),
                 pl.BlockSpec((1, nj, nk, LANES), lambda b, i: (b, i, 0, 0))]
    out_shape = lambda B: [jax.ShapeDtypeStruct((B, nb, nq, LANES), BF16),
                           jax.ShapeDtypeStruct((B, S, nk), BF16),
                           jax.ShapeDtypeStruct((B, nb, nk, LANES), BF16)]
    scratch = [pltpu.VMEM((nq + 2 * nk, D), BF16)]
    return in_specs, out_specs, out_shape, scratch


def _attn_proj(x, norm_w, w_in, layer, pos_rows, inv_freq, *, tm=1024, nsub=4):
    B, S, D = x.shape
    tm = min(tm, S)
    p_in, p_out, p_shape, p_scratch = _attn_proj_specs(S, D, tm, w_in, layer)
    return pl.pallas_call(
        functools.partial(_attn_proj_kernel, tm=tm, nsub=nsub),
        grid=(B, S // tm),
        in_specs=[pl.BlockSpec((1, tm, D), lambda b, i: (b, i, 0))] + p_in,
        out_specs=p_out,
        out_shape=p_shape(B),
        scratch_shapes=p_scratch,
        compiler_params=_params(),
        name="attn_proj",
    )(x, norm_w.reshape(1, D), w_in, pos_rows, inv_freq)


def _attn_core_kernel(qt_ref, kp_ref, kc_ref, kn_ref, vp_ref, vc_ref, vn_ref, sink_ref, wi_ref, wo_ref,
                      o_ref, wi_out, wo_out, k_all, v_all, *, nb, nblk, cast_rep):
    L = LANES
    hd = ATTN_HEAD_DIM
    G = ATTN_GROUP
    c = pl.program_id(1)
    _ffn_cast(pl.program_id(0) * pl.num_programs(1) + c, cast_rep, (wi_ref, wo_ref), (wi_out, wo_out))
    k_all[0:L] = kp_ref[0]
    k_all[L:(nblk + 1) * L] = kc_ref[0]
    k_all[(nblk + 1) * L:] = kn_ref[0]
    v_all[0] = vp_ref[0, 0]
    v_all[1:nblk + 1] = vc_ref[0]
    v_all[nblk + 1] = vn_ref[0, 0]

    key = lax.broadcasted_iota(jnp.int32, (L, G * L), 0)
    qry = lax.broadcasted_iota(jnp.int32, (L, G * L), 1) % L
    prev_mask = jnp.where(key >= qry, 0.0, NEG_INF)
    next_mask = jnp.where(key <= qry, 0.0, NEG_INF)
    zeros = jnp.zeros((hd, G * L), BF16)
    ones = jnp.ones((16, 3 * L), BF16)

    def scores(j, g):
        qg = jnp.concatenate([qt_ref[0, j, (G * g + i) * hd:(G * g + i + 1) * hd, :] for i in range(G)],
                             axis=1)
        qz = jnp.concatenate([qg, zeros] if g % 2 == 0 else [zeros, qg], axis=0)
        return _dot(k_all[j * L:(j + 3) * L, (g // 2) * L:(g // 2 + 1) * L], qz)

    items = [(j, g) for j in range(nblk) for g in range(ATTN_KV_HEADS)]
    ahead = 2
    pending = [scores(*it) for it in items[:ahead]]
    for n, (j, g) in enumerate(items):
        s = pending.pop(0)
        if n + ahead < len(items):
            pending.append(scores(*items[n + ahead]))
        if g == 0:
            blk = c * nblk + j
            prev_bias = prev_mask + jnp.where(blk > 0, 0.0, NEG_INF)
            next_bias = next_mask + jnp.where(blk < nb - 1, 0.0, NEG_INF)
        s = [s[0:L] + prev_bias, s[L:2 * L], s[2 * L:] + next_bias]
        sink = sink_ref[g:g + 1, :]
        m8 = functools.reduce(jnp.maximum, [sj.reshape(L // 8, 8, G * L).max(axis=0) for sj in s])
        m = jnp.maximum(sink, jnp.max(m8, axis=0, keepdims=True))
        p = jnp.concatenate([jnp.exp2(sj - m).astype(BF16) for sj in s], axis=0)
        vg = jnp.concatenate([v_all[j + i, g * hd:(g + 1) * hd, :] for i in range(3)], axis=1)
        acc = _dot(jnp.concatenate([vg, ones], axis=0), p)
        denom = acc[hd:hd + 1] + jnp.exp2(sink - m)
        out = acc[:hd] * (1.0 / denom)
        for pair in range(G // 2):
            tile = jnp.concatenate([out[:, (2 * pair) * L:(2 * pair + 1) * L],
                                    out[:, (2 * pair + 1) * L:(2 * pair + 2) * L]], axis=0)
            col = (G * g + 2 * pair) * hd
            o_ref[0, j * L:(j + 1) * L, col:col + L] = tile.T.astype(BF16)


def _attn_core(qt, k, vt, sink_rows, ffn_w_in, ffn_w_out, layer, *, nblk=8):
    B, nb, nq, L = qt.shape
    nk = k.shape[-1]
    S = nb * L
    nblk = min(nblk, nb)
    steps = nb // nblk
    prev = lambda b, c: (b, jnp.maximum(c * nblk - 1, 0), 0)
    cur = lambda b, c: (b, c, 0)
    nxt = lambda b, c: (b, jnp.minimum((c + 1) * nblk, nb - 1), 0)
    four = lambda f: (lambda b, c: f(b, c) + (0,))
    rep, w_args, w_in_specs, w_out_specs, w_out_shape = _ffn_cast_specs(
        ffn_w_in, ffn_w_out, layer, lambda b, c: b * steps + c, B * steps)
    return pl.pallas_call(
        functools.partial(_attn_core_kernel, nb=nb, nblk=nblk, cast_rep=rep),
        grid=(B, steps),
        in_specs=[pl.BlockSpec((1, nblk, nq, L), four(cur)),
                  pl.BlockSpec((1, L, nk), prev), pl.BlockSpec((1, nblk * L, nk), cur),
                  pl.BlockSpec((1, L, nk), nxt),
                  pl.BlockSpec((1, 1, nk, L), four(prev)), pl.BlockSpec((1, nblk, nk, L), four(cur)),
                  pl.BlockSpec((1, 1, nk, L), four(nxt)),
                  _resident(sink_rows.shape)] + w_in_specs,
        out_specs=[pl.BlockSpec((1, nblk * L, nq), cur)] + w_out_specs,
        out_shape=[jax.ShapeDtypeStruct((B, S, nq), BF16)] + w_out_shape,
        scratch_shapes=[pltpu.VMEM(((nblk + 2) * L, nk), BF16),
                        pltpu.VMEM((nblk + 2, nk, L), BF16)],
        compiler_params=_params(),
        name="attn_core",
    )(qt, k, k, k, vt, vt, vt, sink_rows, *w_args)


def _mlstm_mixer(x, norm_w, w_t, layer, b_gate, ffn_w_in, ffn_w_out, ffn_layer):
    bias_rows = jnp.broadcast_to(b_gate.astype(F32).reshape(-1, 1), (b_gate.shape[0], LANES))
    q, kt, v, og, rf = _mlstm_proj(x, norm_w, w_t, layer, bias_rows)
    hs, wi, wo = _mlstm_core(q, kt, v, rf, ffn_w_in, ffn_w_out, ffn_layer)
    return (hs, og), wi, wo


def _attn_mixer(x, pos_rows, inv_freq, norm_w, w_in, layer, sink, ffn_w_in, ffn_w_out, ffn_layer):
    G, L = ATTN_GROUP, LANES
    qt, k, vt = _attn_proj(x, norm_w, w_in, layer, pos_rows, inv_freq)
    sink_rows = jnp.repeat(LOG2E * sink.astype(F32).reshape(ATTN_KV_HEADS, G), L, axis=1)
    a, wi, wo = _attn_core(qt, k, vt, sink_rows, ffn_w_in, ffn_w_out, ffn_layer)
    return (a,), wi, wo


def kernel(x, positions, norm_mix_w, norm_ffn_w, norm_final_w, mlstm_w_in, mlstm_b_gate, mlstm_norm_w, mlstm_w_out, attn_w_in, attn_sink, attn_w_out, ffn_w_in, ffn_w_out):
    depth = norm_mix_w.shape[0]
    B, S = positions.shape
    half = ATTN_HEAD_DIM // 2
    inv_freq = (ROPE_THETA ** (-jnp.arange(half, dtype=F32) / half)).reshape(half, 1)
    pos_rows = positions.reshape(B, 1, S)
    mlstm_wo, attn_wo = mlstm_w_out.astype(BF16), attn_w_out.astype(BF16)
    mlstm_wt = jnp.swapaxes(mlstm_w_in, 1, 2)
    for i in range(depth):
        j = i // 2
        final_w = norm_final_w if i == depth - 1 else None
        if i % 2 == 0:
            (hs, og), wi, wo = _mlstm_mixer(x, norm_mix_w[i], mlstm_wt, j, mlstm_b_gate[j],
                                            ffn_w_in, ffn_w_out, i)
            x = _layer_tail("mlstm", (hs, og, mlstm_norm_w[j]), x, mlstm_wo, j, norm_ffn_w[i], wi, wo,
                            final_w)
        else:
            mixed, wi, wo = _attn_mixer(x, pos_rows, inv_freq, norm_mix_w[i], attn_w_in, j, attn_sink[j],
                                        ffn_w_in, ffn_w_out, i)
            x = _layer_tail("attn", mixed, x, attn_wo, j, norm_ffn_w[i], wi, wo, final_w)
    return x
```

```python
import functools

import jax
import jax.numpy as jnp
from jax import lax
from jax.experimental import pallas as pl
from jax.experimental.pallas import tpu as pltpu

F32 = jnp.float32
BF16 = jnp.bfloat16

EPS = 1e-6
LANES = 128

MLSTM_HEADS = 4
MLSTM_CHUNK = 128
ATTN_HEAD_DIM = 64
ATTN_Q_HEADS = 16
ATTN_KV_HEADS = 4
ATTN_GROUP = ATTN_Q_HEADS // ATTN_KV_HEADS
ATTN_BLOCK = 128
ROPE_THETA = 10000.0

NEG_INF = float("-inf")
LOG2E = 1.4426950408889634
VMEM_LIMIT = 56 * 1024 * 1024


def _params():
    return pltpu.CompilerParams(dimension_semantics=("arbitrary", "arbitrary"),
                                vmem_limit_bytes=VMEM_LIMIT)


def _resident(shape):
    return pl.BlockSpec(shape, lambda *_: (0,) * len(shape), pipeline_mode=pl.Buffered(1))


def _rmsnorm(x, w):
    ms = jnp.mean(x * x, axis=-1, keepdims=True)
    return x * lax.rsqrt(ms + EPS) * w


def _sigmoid(x):
    return 1.0 / (1.0 + jnp.exp(-x))


def _dot(a, b):
    return jnp.dot(a, b, preferred_element_type=F32)


def _dot_nt(a, b):
    return lax.dot_general(a, b, (((1,), (1,)), ((), ())), preferred_element_type=F32)


def _layer_tail_kernel(*refs, d_ff, tf, nsub, mixer, final):
    it = iter(refs)
    if mixer == "mlstm":
        hs_ref, og_ref, hw_ref = next(it), next(it), next(it)
    else:
        a_ref = next(it)
    x_ref, wo_ref, nw_ref, win_ref, wout_ref = (next(it) for _ in range(5))
    fw_ref = next(it) if final else None
    o_ref, hn_ref, act_ref = (next(it) for _ in range(3))

    tm = x_ref.shape[1]
    subs = [slice(r * (tm // nsub), (r + 1) * (tm // nsub)) for r in range(nsub)]
    for rs in subs:
        if mixer == "mlstm":
            dv = hs_ref.shape[-1] // MLSTM_HEADS
            parts = []
            for h in range(MLSTM_HEADS):
                hs = hs_ref[0, rs, h * dv:(h + 1) * dv]
                ms = jnp.mean(hs * hs, axis=-1, keepdims=True)
                parts.append(hs * lax.rsqrt(ms + EPS))
            y = (jnp.concatenate(parts, axis=-1) * hw_ref[...] * og_ref[0, rs, :].astype(F32)).astype(BF16)
        else:
            y = a_ref[0, rs, :]
        o_ref[0, rs, :] = x_ref[0, rs, :] + _dot(y, wo_ref[...])
    for rs in subs:
        hn_ref[rs, :] = _rmsnorm(o_ref[0, rs, :], nw_ref[...]).astype(BF16)
    for rs in subs:
        for j in range(d_ff // tf):
            hn = hn_ref[rs, :]
            g = _dot(hn, win_ref[:, j * tf:(j + 1) * tf])
            u = _dot(hn, win_ref[:, d_ff + j * tf:d_ff + (j + 1) * tf])
            act_ref[rs, j * tf:(j + 1) * tf] = (g * _sigmoid(g) * u).astype(BF16)
    for rs in subs:
        out = o_ref[0, rs, :] + _dot(act_ref[rs, :], wout_ref[...])
        if final:
            out = _rmsnorm(out, fw_ref[...])
        o_ref[0, rs, :] = out


def _layer_tail(mixer, mixer_args, x, w_o, layer_o, norm_w, w_in, w_out, final_w=None,
                *, tf=256):
    B, S, D = x.shape
    tm, nsub = (1024, 4) if mixer == "attn" else (512, 2)
    tm = min(tm, S)
    d_ff = w_out.shape[0]
    final = final_w is not None
    row = pl.BlockSpec((1, tm, D), lambda b, i: (b, i, 0))

    def stacked(shape, l):
        return pl.BlockSpec((None,) + shape, lambda b, i: (l,) + (0,) * len(shape),
                            pipeline_mode=pl.Buffered(1))

    if mixer == "mlstm":
        hs, og, head_w = mixer_args
        args = [hs, og, head_w.reshape(1, D)]
        in_specs = [row, row, _resident((1, D))]
    else:
        args = list(mixer_args)
        in_specs = [row]
    args += [x, w_o, norm_w.reshape(1, D), w_in, w_out]
    in_specs += [row, stacked((w_o.shape[1], D), layer_o), _resident((1, D)),
                 _resident((D, 2 * d_ff)), _resident((d_ff, D))]
    if final:
        args.append(final_w.reshape(1, D))
        in_specs.append(_resident((1, D)))
    return pl.pallas_call(
        functools.partial(_layer_tail_kernel, d_ff=d_ff, tf=tf, nsub=nsub, mixer=mixer, final=final),
        grid=(B, S // tm),
        in_specs=in_specs,
        out_specs=row,
        out_shape=jax.ShapeDtypeStruct((B, S, D), F32),
        scratch_shapes=[pltpu.VMEM((tm, D), BF16), pltpu.VMEM((tm, d_ff), BF16)],
        compiler_params=_params(),
        name=mixer + "_tail",
    )(*args)


def _mlstm_proj_kernel(x_ref, nw_ref, wt_ref, bias_ref,
                       q_ref, kt_ref, v_ref, og_ref, rf_ref, w_ref, wkt_ref, *, dk, tm, nsub):
    d = v_ref.shape[-1]
    hdk = q_ref.shape[-1]
    ng = rf_ref.shape[1] // (tm // LANES)
    sub = tm // nsub

    @pl.when((pl.program_id(0) == 0) & (pl.program_id(1) == 0))
    def _():
        for c in range(0, hdk, LANES):
            w_ref[:, c:c + LANES] = wt_ref[c:c + LANES, :].T.astype(BF16)
            wkt_ref[c:c + LANES, :] = wt_ref[hdk + c:hdk + c + LANES, :].astype(BF16)
        for c in range(0, 2 * d, LANES):
            w_ref[:, hdk + c:hdk + c + LANES] = wt_ref[2 * hdk + c:2 * hdk + c + LANES, :].T.astype(BF16)
        wkt_ref[hdk:, :] = wt_ref[2 * hdk + 2 * d:, :].astype(BF16)

    for r in range(nsub):
        rs = slice(r * sub, (r + 1) * sub)
        hn = _rmsnorm(x_ref[0, rs, :], nw_ref[...]).astype(BF16)
        og_ref[0, rs, :] = _sigmoid(_dot(hn, w_ref[:, hdk + d:])).astype(BF16)
        v_ref[0, rs, :] = _dot(hn, w_ref[:, hdk:hdk + d]).astype(BF16)
        q_ref[0, rs, :] = (_dot(hn, w_ref[:, :hdk]) * (dk ** -0.5)).astype(BF16)
        kg = _dot_nt(wkt_ref[...], hn)
        kt = kg[:hdk].astype(BF16)
        for jj in range(sub // LANES):
            j = r * (sub // LANES) + jj
            kt_ref[0, j] = kt[:, jj * LANES:(jj + 1) * LANES]
            rf_ref[0, j * ng:(j + 1) * ng, :] = kg[hdk:, jj * LANES:(jj + 1) * LANES] + bias_ref[...]


def _mlstm_proj(x, norm_w, w_t, layer, bias_rows, *, tm=1024, nsub=4):
    B, S, D = x.shape
    H = MLSTM_HEADS
    hdk = D // 2
    dk = hdk // H
    tm = min(tm, S)
    ng = bias_rows.shape[0]
    nj = tm // LANES
    NC = S // LANES
    row = lambda b, i: (b, i, 0)
    return pl.pallas_call(
        functools.partial(_mlstm_proj_kernel, dk=dk, tm=tm, nsub=nsub),
        grid=(B, S // tm),
        in_specs=[pl.BlockSpec((1, tm, D), row), _resident((1, D)),
                  pl.BlockSpec((None,) + w_t.shape[1:], lambda b, i: (layer, 0, 0),
                               pipeline_mode=pl.Buffered(1)),
                  _resident((ng, LANES))],
        out_specs=[pl.BlockSpec((1, tm, hdk), row),
                   pl.BlockSpec((1, nj, hdk, LANES), lambda b, i: (b, i, 0, 0)),
                   pl.BlockSpec((1, tm, D), row),
                   pl.BlockSpec((1, tm, D), row),
                   pl.BlockSpec((1, nj * ng, LANES), row)],
        out_shape=[jax.ShapeDtypeStruct((B, S, hdk), BF16),
                   jax.ShapeDtypeStruct((B, NC, hdk, LANES), BF16),
                   jax.ShapeDtypeStruct((B, S, D), BF16),
                   jax.ShapeDtypeStruct((B, S, D), BF16),
                   jax.ShapeDtypeStruct((B, NC * ng, LANES), F32)],
        scratch_shapes=[pltpu.VMEM((D, hdk + 2 * D), BF16), pltpu.VMEM((hdk + ng, D), BF16)],
        compiler_params=_params(),
        name="mlstm_proj",
    )(x, norm_w.reshape(1, D), w_t, bias_rows)


FFN_CAST_BLOCKS = 8


def _ffn_cast_specs(ffn_w_in, ffn_w_out, layer, step_of, total_steps):
    nblk = min(FFN_CAST_BLOCKS, total_steps)
    rep = total_steps // nblk
    args, in_specs, out_specs, out_shape = [], [], [], []
    for w in (ffn_w_in, ffn_w_out):
        rows, cols = w.shape[1] // nblk, w.shape[2]
        args.append(w)
        in_specs.append(pl.BlockSpec((None, rows, cols),
                                     lambda *g, l=layer: (l, step_of(*g) // rep, 0)))
        out_specs.append(pl.BlockSpec((rows, cols), lambda *g: (step_of(*g) // rep, 0)))
        out_shape.append(jax.ShapeDtypeStruct(w.shape[1:], BF16))
    return rep, args, in_specs, out_specs, out_shape


def _ffn_cast(step, rep, srcs, dsts):
    @pl.when(step % rep == 0)
    def _():
        for src, dst in zip(srcs, dsts):
            dst[...] = src[...].astype(BF16)


def _log_sigmoid(x):
    return jnp.minimum(x, 0.0) - jnp.log1p(jnp.exp(-jnp.abs(x)))


def _mlstm_core_kernel(q_ref, kt_ref, v_ref, rf_ref, wi_ref, wo_ref, o_ref, wi_out, wo_out, st_ref, c_ref,
                       *, nc, dv, unroll, cast_rep):
    L = LANES
    h = pl.program_id(1)
    ng = 4 * MLSTM_HEADS
    _ffn_cast(pl.program_id(0) * pl.num_programs(1) + h, cast_rep, (wi_ref, wo_ref), (wi_out, wo_out))

    def gate_rows(g):
        return rf_ref[0, pl.ds(MLSTM_HEADS * g + h, nc, stride=ng), :]

    lane = lax.broadcasted_iota(jnp.int32, (nc, L), 1)

    def prefix_sum(x):
        for s in (1, 2, 4, 8, 16, 32, 64):
            x = x + jnp.where(lane >= s, pltpu.roll(x, s, axis=1), 0.0)
        return x

    def suffix_sum(x):
        for s in (1, 2, 4, 8, 16, 32, 64):
            x = x + jnp.where(lane < L - s, pltpu.roll(x, L - s, axis=1), 0.0)
        return x

    for d in range(2):
        log_i = gate_rows(2 * d)
        log_f = _log_sigmoid(gate_rows(2 * d + 1))
        if d == 0:
            b = prefix_sum(log_f)
            b_last = jnp.broadcast_to(b[:, L - 1:L], (nc, L))
        else:
            b = suffix_sum(log_f)
            b_last = jnp.broadcast_to(b[:, 0:1], (nc, L))
        r = log_i - b
        a = b_last + r
        a_max = jnp.broadcast_to(jnp.max(a, axis=1, keepdims=True), (nc, L))
        st_ref[d, 0] = log_f
        st_ref[d, 1] = r
        st_ref[d, 2] = jnp.exp(a - a_max)
        st_ref[d, 3] = b_last
        st_ref[d, 4] = a_max

    def m_scan(i, carry):
        new = []
        for d, c in ((0, i), (1, nc - 1 - i)):
            row = pl.ds(c, 1)
            st_ref[d, 5, row, :] = carry[d]
            new.append(jnp.maximum(st_ref[d, 3, row, :] + carry[d], st_ref[d, 4, row, :]))
        return tuple(new)

    m0 = jnp.zeros((1, L), F32)
    lax.fori_loop(0, nc, m_scan, (m0, m0))
    for d in range(2):
        b_last, a_max, m_prev = st_ref[d, 3], st_ref[d, 4], st_ref[d, 5]
        m_new = jnp.maximum(b_last + m_prev, a_max)
        st_ref[d, 3] = jnp.exp(b_last + m_prev - m_new)
        st_ref[d, 2] = st_ref[d, 2] * jnp.exp(a_max - m_new)
        st_ref[d, 0] = st_ref[d, 0] * LOG2E
        st_ref[d, 1] = st_ref[d, 1] * LOG2E
        st_ref[d, 5] = m_prev * LOG2E

    c_ref[...] = jnp.zeros_like(c_ref)

    t_idx = lax.broadcasted_iota(jnp.int32, (L, L), 0)
    s_idx = lax.broadcasted_iota(jnp.int32, (L, L), 1)
    masks = (s_idx <= t_idx, s_idx >= t_idx)
    ones = jnp.ones((L, L), BF16)

    def tile3(x):
        return jnp.concatenate([x] * (dv // L + 1), axis=1)

    def body(i, carry, assign):
        items = []
        for u in range(unroll):
            c = i * unroll + u
            items += [(0, c), (1, nc - 1 - c)]

        work = []
        for d, c in items:
            rows = pl.ds(pl.multiple_of(c * L, L), L)
            qc = q_ref[0, rows, :]
            kt = kt_ref[0, c]
            vaug = jnp.concatenate([v_ref[0, rows, :], ones], axis=1)
            w = st_ref[d, 2, pl.ds(c, 1), :]
            scores = _dot(qc, kt)
            kv = _dot(kt * w.astype(BF16), vaug)
            work.append((rows, qc, vaug, scores, kv))

        states = [c_ref[0], c_ref[1]]
        prev_states = []
        for (d, c), (rows, qc, vaug, scores, kv) in zip(items, work):
            prev_states.append(states[d].astype(BF16))
            states[d] = tile3(st_ref[d, 3, pl.ds(c, 1), :]) * states[d] + kv
        c_ref[0] = states[0]
        c_ref[1] = states[1]

        for (d, c), (rows, qc, vaug, scores, kv), prev_state in zip(items, work, prev_states):
            log_f = st_ref[d, 0, pl.ds(c, 1), :]
            r = st_ref[d, 1, pl.ds(c, 1), :]
            m_prev = st_ref[d, 5, pl.ds(c, 1), :]
            mask = masks[d]
            r_masked = jnp.where(mask, r, NEG_INF)
            cm = jnp.max(r_masked, axis=1, keepdims=True)
            b_col = jnp.sum(jnp.where(mask, log_f, 0.0), axis=1, keepdims=True)
            mu = jnp.broadcast_to(jnp.maximum(m_prev[:, 0:1], cm), (L, L))
            decay = jnp.exp2(r_masked - mu)
            inter = jnp.exp2(m_prev - mu)
            lhs = jnp.concatenate([(scores * decay).astype(BF16), inter.astype(BF16) * qc], axis=1)
            both = _dot(lhs, jnp.concatenate([vaug, prev_state], axis=0))
            den = jnp.maximum(jnp.abs(both[:, dv:]), jnp.exp2(-(b_col + mu)))
            out = both[:, :dv] * tile3(1.0 / den)[:, :dv]
            if assign:
                o_ref[0, rows, :] = out
            else:
                o_ref[0, rows, :] += out
        return carry

    steps = nc // 2 // unroll
    lax.fori_loop(0, steps, functools.partial(body, assign=True), 0)
    lax.fori_loop(steps, 2 * steps, functools.partial(body, assign=False), 0)


def _mlstm_core(q, kt, v, rf, ffn_w_in, ffn_w_out, layer, *, unroll=8):
    B, S, hdk = q.shape
    H = MLSTM_HEADS
    dk = hdk // H
    dv = v.shape[-1] // H
    nc = S // LANES
    rep, w_args, w_in_specs, w_out_specs, w_out_shape = _ffn_cast_specs(
        ffn_w_in, ffn_w_out, layer, lambda b, h: b * H + h, B * H)
    return pl.pallas_call(
        functools.partial(_mlstm_core_kernel, nc=nc, dv=dv, unroll=min(unroll, nc // 2), cast_rep=rep),
        grid=(B, H),
        in_specs=[pl.BlockSpec((1, S, dk), lambda b, h: (b, 0, h)),
                  pl.BlockSpec((1, nc, dk, LANES), lambda b, h: (b, 0, h, 0)),
                  pl.BlockSpec((1, S, dv), lambda b, h: (b, 0, h)),
                  pl.BlockSpec((1, rf.shape[1], LANES), lambda b, h: (b, 0, 0))] + w_in_specs,
        out_specs=[pl.BlockSpec((1, S, dv), lambda b, h: (b, 0, h))] + w_out_specs,
        out_shape=[jax.ShapeDtypeStruct((B, S, H * dv), F32)] + w_out_shape,
        scratch_shapes=[pltpu.VMEM((2, 6, nc, LANES), F32),
                        pltpu.VMEM((2, dk, dv + LANES), F32)],
        compiler_params=_params(),
        name="mlstm_core",
    )(q, kt, v, rf, *w_args)


def _attn_weight_prep(w_ref, wt_ref):
    @pl.when((pl.program_id(0) == 0) & (pl.program_id(1) == 0))
    def _():
        for c in range(0, wt_ref.shape[0], LANES):
            wt_ref[c:c + LANES, :] = w_ref[:, c:c + LANES].T.astype(BF16)


def _attn_proj_rows(x_rows, pos_row, invf, nw, wt_ref, qt_ref, k_ref, vt_ref, chunk0):
    hd = ATTN_HEAD_DIM
    half = hd // 2
    nq = ATTN_Q_HEADS * hd
    nk = ATTN_KV_HEADS * hd
    scale = LOG2E * hd ** -0.5
    piece = 8 * hd
    sub = x_rows.shape[0]
    hn = _rmsnorm(x_rows, nw).astype(BF16)
    ang = invf * pos_row.astype(F32)
    cos = jnp.cos(ang)
    sin = jnp.sin(ang)
    chunks = [(chunk0 + jj, slice(jj * LANES, (jj + 1) * LANES)) for jj in range(sub // LANES)]

    def rope(xh):
        x1, x2 = xh[:half], xh[half:]
        return jnp.concatenate([x1 * cos - x2 * sin, x2 * cos + x1 * sin], axis=0)

    qp = _dot_nt(wt_ref[:nq, :], hn)
    kv = _dot_nt(wt_ref[nq:, :], hn)
    for lo in range(0, nq, piece):
        qt = jnp.concatenate([rope(qp[lo + h * hd:lo + (h + 1) * hd]) * scale for h in range(piece // hd)],
                             axis=0).astype(BF16)
        for j, cols in chunks:
            qt_ref[0, j, lo:lo + piece, :] = qt[:, cols]
    vt = kv[nk:].astype(BF16)
    for j, cols in chunks:
        vt_ref[0, j] = vt[:, cols]
    kt = jnp.concatenate([rope(kv[g * hd:(g + 1) * hd]) for g in range(ATTN_KV_HEADS)], axis=0)
    for j, cols in chunks:
        for f in range(nk // LANES):
            k_ref[0, j * LANES:(j + 1) * LANES, f * LANES:(f + 1) * LANES] = (
                kt[f * LANES:(f + 1) * LANES, cols].T.astype(BF16))


def _attn_proj_kernel(x_ref, nw_ref, w_ref, pos_ref, invf_ref, qt_ref, k_ref, vt_ref, wt_ref, *, tm, nsub):
    _attn_weight_prep(w_ref, wt_ref)
    sub = tm // nsub
    for r in range(nsub):
        rs = slice(r * sub, (r + 1) * sub)
        _attn_proj_rows(x_ref[0, rs, :], pos_ref[0, :, rs], invf_ref[...], nw_ref[...], wt_ref,
                        qt_ref, k_ref, vt_ref, r * (sub // LANES))


def _attn_proj_specs(S, D, tm, w_in, layer):
    nq = ATTN_Q_HEADS * ATTN_HEAD_DIM
    nk = ATTN_KV_HEADS * ATTN_HEAD_DIM
    nj = tm // LANES
    nb = S // LANES
    in_specs = [_resident((1, D)),
                pl.BlockSpec((None,) + w_in.shape[1:], lambda b, i: (layer, 0, 0),
                             pipeline_mode=pl.Buffered(1)),
                pl.BlockSpec((1, 1, tm), lambda b, i: (b, 0, i)),
                _resident((ATTN_HEAD_DIM // 2, 1))]
    out_specs = [pl.BlockSpec((1, nj, nq, LANES), lambda b, i: (b, i, 0, 0)),
                 pl.BlockSpec((1, tm, nk), lambda b, i: (b, i, 0)),
                 pl.BlockSpec((1, nj, nk, LANES), lambda b, i: (b, i, 0, 0))]
    out_shape = lambda B: [jax.ShapeDtypeStruct((B, nb, nq, LANES), BF16),
                           jax.ShapeDtypeStruct((B, S, nk), BF16),
                           jax.ShapeDtypeStruct((B, nb, nk, LANES), BF16)]
    scratch = [pltpu.VMEM((nq + 2 * nk, D), BF16)]
    return in_specs, out_specs, out_shape, scratch


def _attn_proj(x, norm_w, w_in, layer, pos_rows, inv_freq, *, tm=1024, nsub=4):
    B, S, D = x.shape
    tm = min(tm, S)
    p_in, p_out, p_shape, p_scratch = _attn_proj_specs(S, D, tm, w_in, layer)
    return pl.pallas_call(
        functools.partial(_attn_proj_kernel, tm=tm, nsub=nsub),
        grid=(B, S // tm),
        in_specs=[pl.BlockSpec((1, tm, D), lambda b, i: (b, i, 0))] + p_in,
        out_specs=p_out,
        out_shape=p_shape(B),
        scratch_shapes=p_scratch,
        compiler_params=_params(),
        name="attn_proj",
    )(x, norm_w.reshape(1, D), w_in, pos_rows, inv_freq)


def _attn_core_kernel(qt_ref, kp_ref, kc_ref, kn_ref, vp_ref, vc_ref, vn_ref, sink_ref, wi_ref, wo_ref,
                      o_ref, wi_out, wo_out, k_all, v_all, *, nb, nblk, cast_rep):
    L = LANES
    hd = ATTN_HEAD_DIM
    G = ATTN_GROUP
    c = pl.program_id(1)
    _ffn_cast(pl.program_id(0) * pl.num_programs(1) + c, cast_rep, (wi_ref, wo_ref), (wi_out, wo_out))
    k_all[0:L] = kp_ref[0]
    k_all[L:(nblk + 1) * L] = kc_ref[0]
    k_all[(nblk + 1) * L:] = kn_ref[0]
    v_all[0] = vp_ref[0, 0]
    v_all[1:nblk + 1] = vc_ref[0]
    v_all[nblk + 1] = vn_ref[0, 0]

    key = lax.broadcasted_iota(jnp.int32, (L, G * L), 0)
    qry = lax.broadcasted_iota(jnp.int32, (L, G * L), 1) % L
    prev_mask = jnp.where(key >= qry, 0.0, NEG_INF)
    next_mask = jnp.where(key <= qry, 0.0, NEG_INF)
    zeros = jnp.zeros((hd, G * L), BF16)
    ones = jnp.ones((16, 3 * L), BF16)

    def scores(j, g):
        qg = jnp.concatenate([qt_ref[0, j, (G * g + i) * hd:(G * g + i + 1) * hd, :] for i in range(G)],
                             axis=1)
        qz = jnp.concatenate([qg, zeros] if g % 2 == 0 else [zeros, qg], axis=0)
        return _dot(k_all[j * L:(j + 3) * L, (g // 2) * L:(g // 2 + 1) * L], qz)

    items = [(j, g) for j in range(nblk) for g in range(ATTN_KV_HEADS)]
    ahead = 2
    pending = [scores(*it) for it in items[:ahead]]
    for n, (j, g) in enumerate(items):
        s = pending.pop(0)
        if n + ahead < len(items):
            pending.append(scores(*items[n + ahead]))
        if g == 0:
            blk = c * nblk + j
            prev_bias = prev_mask + jnp.where(blk > 0, 0.0, NEG_INF)
            next_bias = next_mask + jnp.where(blk < nb - 1, 0.0, NEG_INF)
        s = [s[0:L] + prev_bias, s[L:2 * L], s[2 * L:] + next_bias]
        sink = sink_ref[g:g + 1, :]
        m8 = functools.reduce(jnp.maximum, [sj.reshape(L // 8, 8, G * L).max(axis=0) for sj in s])
        m = jnp.maximum(sink, jnp.max(m8, axis=0, keepdims=True))
        p = jnp.concatenate([jnp.exp2(sj - m).astype(BF16) for sj in s], axis=0)
        vg = jnp.concatenate([v_all[j + i, g * hd:(g + 1) * hd, :] for i in range(3)], axis=1)
        acc = _dot(jnp.concatenate([vg, ones], axis=0), p)
        denom = acc[hd:hd + 1] + jnp.exp2(sink - m)
        out = acc[:hd] * (1.0 / denom)
        for pair in range(G // 2):
            tile = jnp.concatenate([out[:, (2 * pair) * L:(2 * pair + 1) * L],
                                    out[:, (2 * pair + 1) * L:(2 * pair + 2) * L]], axis=0)
            col = (G * g + 2 * pair) * hd
            o_ref[0, j * L:(j + 1) * L, col:col + L] = tile.T.astype(BF16)


def _attn_core(qt, k, vt, sink_rows, ffn_w_in, ffn_w_out, layer, *, nblk=8):
    B, nb, nq, L = qt.shape
    nk = k.shape[-1]
    S = nb * L
    nblk = min(nblk, nb)
    steps = nb // nblk
    prev = lambda b, c: (b, jnp.maximum(c * nblk - 1, 0), 0)
    cur = lambda b, c: (b, c, 0)
    nxt = lambda b, c: (b, jnp.minimum((c + 1) * nblk, nb - 1), 0)
    four = lambda f: (lambda b, c: f(b, c) + (0,))
    rep, w_args, w_in_specs, w_out_specs, w_out_shape = _ffn_cast_specs(
        ffn_w_in, ffn_w_out, layer, lambda b, c: b * steps + c, B * steps)
    return pl.pallas_call(
        functools.partial(_attn_core_kernel, nb=nb, nblk=nblk, cast_rep=rep),
        grid=(B, steps),
        in_specs=[pl.BlockSpec((1, nblk, nq, L), four(cur)),
                  pl.BlockSpec((1, L, nk), prev), pl.BlockSpec((1, nblk * L, nk), cur),
                  pl.BlockSpec((1, L, nk), nxt),
                  pl.BlockSpec((1, 1, nk, L), four(prev)), pl.BlockSpec((1, nblk, nk, L), four(cur)),
                  pl.BlockSpec((1, 1, nk, L), four(nxt)),
                  _resident(sink_rows.shape)] + w_in_specs,
        out_specs=[pl.BlockSpec((1, nblk * L, nq), cur)] + w_out_specs,
        out_shape=[jax.ShapeDtypeStruct((B, S, nq), BF16)] + w_out_shape,
        scratch_shapes=[pltpu.VMEM(((nblk + 2) * L, nk), BF16),
                        pltpu.VMEM((nblk + 2, nk, L), BF16)],
        compiler_params=_params(),
        name="attn_core",
    )(qt, k, k, k, vt, vt, vt, sink_rows, *w_args)


def _mlstm_mixer(x, norm_w, w_t, layer, b_gate, ffn_w_in, ffn_w_out, ffn_layer):
    bias_rows = jnp.broadcast_to(b_gate.astype(F32).reshape(-1, 1), (b_gate.shape[0], LANES))
    q, kt, v, og, rf = _mlstm_proj(x, norm_w, w_t, layer, bias_rows)
    hs, wi, wo = _mlstm_core(q, kt, v, rf, ffn_w_in, ffn_w_out, ffn_layer)
    return (hs, og), wi, wo


def _attn_mixer(x, pos_rows, inv_freq, norm_w, w_in, layer, sink, ffn_w_in, ffn_w_out, ffn_layer):
    G, L = ATTN_GROUP, LANES
    qt, k, vt = _attn_proj(x, norm_w, w_in, layer, pos_rows, inv_freq)
    sink_rows = jnp.repeat(LOG2E * sink.astype(F32).reshape(ATTN_KV_HEADS, G), L, axis=1)
    a, wi, wo = _attn_core(qt, k, vt, sink_rows, ffn_w_in, ffn_w_out, ffn_layer)
    return (a,), wi, wo


def kernel(x, positions, norm_mix_w, norm_ffn_w, norm_final_w, mlstm_w_in, mlstm_b_gate, mlstm_norm_w, mlstm_w_out, attn_w_in, attn_sink, attn_w_out, ffn_w_in, ffn_w_out):
    depth = norm_mix_w.shape[0]
    B, S = positions.shape
    half = ATTN_HEAD_DIM // 2
    inv_freq = (ROPE_THETA ** (-jnp.arange(half, dtype=F32) / half)).reshape(half, 1)
    pos_rows = positions.reshape(B, 1, S)
    mlstm_wo, attn_wo = mlstm_w_out.astype(BF16), attn_w_out.astype(BF16)
    mlstm_wt = jnp.swapaxes(mlstm_w_in, 1, 2)
    for i in range(depth):
        j = i // 2
        final_w = norm_final_w if i == depth - 1 else None
        if i % 2 == 0:
            (hs, og), wi, wo = _mlstm_mixer(x, norm_mix_w[i], mlstm_wt, j, mlstm_b_gate[j],
                                            ffn_w_in, ffn_w_out, i)
            x = _layer_tail("mlstm", (hs, og, mlstm_norm_w[j]), x, mlstm_wo, j, norm_ffn_w[i], wi, wo,
                            final_w)
        else:
            mixed, wi, wo = _attn_mixer(x, pos_rows, inv_freq, norm_mix_w[i], attn_w_in, j, attn_sink[j],
                                        ffn_w_in, ffn_w_out, i)
            x = _layer_tail("attn", mixed, x, attn_wo, j, norm_ffn_w[i], wi, wo, final_w)
    return x
```

```python
import functools

import jax
import jax.numpy as jnp
from jax import lax
from jax.experimental import pallas as pl
from jax.experimental.pallas import tpu as pltpu

F32 = jnp.float32
BF16 = jnp.bfloat16

EPS = 1e-6
LANES = 128

MLSTM_HEADS = 4
MLSTM_CHUNK = 128
ATTN_HEAD_DIM = 64
ATTN_Q_HEADS = 16
ATTN_KV_HEADS = 4
ATTN_GROUP = ATTN_Q_HEADS // ATTN_KV_HEADS
ATTN_BLOCK = 128
ROPE_THETA = 10000.0

NEG_INF = float("-inf")
LOG2E = 1.4426950408889634
VMEM_LIMIT = 56 * 1024 * 1024


def _params():
    return pltpu.CompilerParams(dimension_semantics=("arbitrary", "arbitrary"),
                                vmem_limit_bytes=VMEM_LIMIT)


def _resident(shape):
    return pl.BlockSpec(shape, lambda *_: (0,) * len(shape), pipeline_mode=pl.Buffered(1))


def _rmsnorm(x, w):
    ms = jnp.mean(x * x, axis=-1, keepdims=True)
    return x * lax.rsqrt(ms + EPS) * w


def _sigmoid(x):
    return 1.0 / (1.0 + jnp.exp(-x))


def _dot(a, b):
    return jnp.dot(a, b, preferred_element_type=F32)


def _dot_nt(a, b):
    return lax.dot_general(a, b, (((1,), (1,)), ((), ())), preferred_element_type=F32)


def _layer_tail_kernel(*refs, d_ff, tf, nsub, mixer, final):
    it = iter(refs)
    if mixer == "mlstm":
        hs_ref, og_ref, hw_ref = next(it), next(it), next(it)
    else:
        a_ref = next(it)
    x_ref, wo_ref, nw_ref, win_hbm, wout_hbm = (next(it) for _ in range(5))
    fw_ref = next(it) if final else None
    o_ref, hn_ref, act_ref, win_ref, wout_ref, sem = (next(it) for _ in range(6))

    first = (pl.program_id(0) == 0) & (pl.program_id(1) == 0)
    copy_in = pltpu.make_async_copy(win_hbm, win_ref, sem.at[0])
    copy_out = pltpu.make_async_copy(wout_hbm, wout_ref, sem.at[1])

    @pl.when(first)
    def _():
        copy_in.start()
        copy_out.start()

    tm = x_ref.shape[1]
    subs = [slice(r * (tm // nsub), (r + 1) * (tm // nsub)) for r in range(nsub)]
    for rs in subs:
        if mixer == "mlstm":
            dv = hs_ref.shape[-1] // MLSTM_HEADS
            parts = []
            for h in range(MLSTM_HEADS):
                hs = hs_ref[0, rs, h * dv:(h + 1) * dv]
                ms = jnp.mean(hs * hs, axis=-1, keepdims=True)
                parts.append(hs * lax.rsqrt(ms + EPS))
            y = (jnp.concatenate(parts, axis=-1) * hw_ref[...] * og_ref[0, rs, :].astype(F32)).astype(BF16)
        else:
            y = a_ref[0, rs, :]
        o_ref[0, rs, :] = x_ref[0, rs, :] + _dot(y, wo_ref[...])
    for rs in subs:
        hn_ref[rs, :] = _rmsnorm(o_ref[0, rs, :], nw_ref[...]).astype(BF16)
    pl.when(first)(copy_in.wait)
    for rs in subs:
        for j in range(d_ff // tf):
            hn = hn_ref[rs, :]
            g = _dot(hn, win_ref[:, j * tf:(j + 1) * tf])
            u = _dot(hn, win_ref[:, d_ff + j * tf:d_ff + (j + 1) * tf])
            act_ref[rs, j * tf:(j + 1) * tf] = (g * _sigmoid(g) * u).astype(BF16)
    pl.when(first)(copy_out.wait)
    for rs in subs:
        out = o_ref[0, rs, :] + _dot(act_ref[rs, :], wout_ref[...])
        if final:
            out = _rmsnorm(out, fw_ref[...])
        o_ref[0, rs, :] = out


def _layer_tail(mixer, mixer_args, x, w_o, layer_o, norm_w, w_in, w_out, final_w=None,
                *, tf=256):
    B, S, D = x.shape
    tm, nsub = (1024, 4) if mixer == "attn" else (512, 2)
    tm = min(tm, S)
    d_ff = w_out.shape[0]
    final = final_w is not None
    row = pl.BlockSpec((1, tm, D), lambda b, i: (b, i, 0))

    def stacked(shape, l):
        return pl.BlockSpec((None,) + shape, lambda b, i: (l,) + (0,) * len(shape),
                            pipeline_mode=pl.Buffered(1))

    if mixer == "mlstm":
        hs, og, head_w = mixer_args
        args = [hs, og, head_w.reshape(1, D)]
        in_specs = [row, row, _resident((1, D))]
    else:
        args = list(mixer_args)
        in_specs = [row]
    args += [x, w_o, norm_w.reshape(1, D), w_in, w_out]
    in_specs += [row, stacked((w_o.shape[1], D), layer_o), _resident((1, D)),
                 pl.BlockSpec(memory_space=pl.ANY), pl.BlockSpec(memory_space=pl.ANY)]
    if final:
        args.append(final_w.reshape(1, D))
        in_specs.append(_resident((1, D)))
    return pl.pallas_call(
        functools.partial(_layer_tail_kernel, d_ff=d_ff, tf=tf, nsub=nsub, mixer=mixer, final=final),
        grid=(B, S // tm),
        in_specs=in_specs,
        out_specs=row,
        out_shape=jax.ShapeDtypeStruct((B, S, D), F32),
        scratch_shapes=[pltpu.VMEM((tm, D), BF16), pltpu.VMEM((tm, d_ff), BF16),
                        pltpu.VMEM((D, 2 * d_ff), BF16), pltpu.VMEM((d_ff, D), BF16),
                        pltpu.SemaphoreType.DMA((2,))],
        compiler_params=_params(),
        name=mixer + "_tail",
    )(*args)


def _mlstm_proj_kernel(x_ref, nw_ref, wt_ref, bias_ref,
                       q_ref, kt_ref, v_ref, og_ref, rf_ref, w_ref, wkt_ref, *, dk, tm, nsub):
    d = v_ref.shape[-1]
    hdk = q_ref.shape[-1]
    ng = rf_ref.shape[1] // (tm // LANES)
    sub = tm // nsub

    @pl.when((pl.program_id(0) == 0) & (pl.program_id(1) == 0))
    def _():
        for c in range(0, hdk, LANES):
            w_ref[:, c:c + LANES] = wt_ref[c:c + LANES, :].T.astype(BF16)
            wkt_ref[c:c + LANES, :] = wt_ref[hdk + c:hdk + c + LANES, :].astype(BF16)
        for c in range(0, 2 * d, LANES):
            w_ref[:, hdk + c:hdk + c + LANES] = wt_ref[2 * hdk + c:2 * hdk + c + LANES, :].T.astype(BF16)
        wkt_ref[hdk:, :] = wt_ref[2 * hdk + 2 * d:, :].astype(BF16)

    for r in range(nsub):
        rs = slice(r * sub, (r + 1) * sub)
        hn = _rmsnorm(x_ref[0, rs, :], nw_ref[...]).astype(BF16)
        og_ref[0, rs, :] = _sigmoid(_dot(hn, w_ref[:, hdk + d:])).astype(BF16)
        v_ref[0, rs, :] = _dot(hn, w_ref[:, hdk:hdk + d]).astype(BF16)
        q_ref[0, rs, :] = (_dot(hn, w_ref[:, :hdk]) * (dk ** -0.5)).astype(BF16)
        kg = _dot_nt(wkt_ref[...], hn)
        kt = kg[:hdk].astype(BF16)
        for jj in range(sub // LANES):
            j = r * (sub // LANES) + jj
            kt_ref[0, j] = kt[:, jj * LANES:(jj + 1) * LANES]
            rf_ref[0, j * ng:(j + 1) * ng, :] = kg[hdk:, jj * LANES:(jj + 1) * LANES] + bias_ref[...]


def _mlstm_proj(x, norm_w, w_t, layer, bias_rows, *, tm=1024, nsub=4):
    B, S, D = x.shape
    H = MLSTM_HEADS
    hdk = D // 2
    dk = hdk // H
    tm = min(tm, S)
    ng = bias_rows.shape[0]
    nj = tm // LANES
    NC = S // LANES
    row = lambda b, i: (b, i, 0)
    return pl.pallas_call(
        functools.partial(_mlstm_proj_kernel, dk=dk, tm=tm, nsub=nsub),
        grid=(B, S // tm),
        in_specs=[pl.BlockSpec((1, tm, D), row), _resident((1, D)),
                  pl.BlockSpec((None,) + w_t.shape[1:], lambda b, i: (layer, 0, 0),
                               pipeline_mode=pl.Buffered(1)),
                  _resident((ng, LANES))],
        out_specs=[pl.BlockSpec((1, tm, hdk), row),
                   pl.BlockSpec((1, nj, hdk, LANES), lambda b, i: (b, i, 0, 0)),
                   pl.BlockSpec((1, tm, D), row),
                   pl.BlockSpec((1, tm, D), row),
                   pl.BlockSpec((1, nj * ng, LANES), row)],
        out_shape=[jax.ShapeDtypeStruct((B, S, hdk), BF16),
                   jax.ShapeDtypeStruct((B, NC, hdk, LANES), BF16),
                   jax.ShapeDtypeStruct((B, S, D), BF16),
                   jax.ShapeDtypeStruct((B, S, D), BF16),
                   jax.ShapeDtypeStruct((B, NC * ng, LANES), F32)],
        scratch_shapes=[pltpu.VMEM((D, hdk + 2 * D), BF16), pltpu.VMEM((hdk + ng, D), BF16)],
        compiler_params=_params(),
        name="mlstm_proj",
    )(x, norm_w.reshape(1, D), w_t, bias_rows)


FFN_CAST_BLOCKS = 8


def _ffn_cast_specs(ffn_w_in, ffn_w_out, layer, step_of, total_steps):
    nblk = min(FFN_CAST_BLOCKS, total_steps)
    rep = total_steps // nblk
    args, in_specs, out_specs, out_shape = [], [], [], []
    for w in (ffn_w_in, ffn_w_out):
        rows, cols = w.shape[1] // nblk, w.shape[2]
        args.append(w)
        in_specs.append(pl.BlockSpec((None, rows, cols),
                                     lambda *g, l=layer: (l, step_of(*g) // rep, 0)))
        out_specs.append(pl.BlockSpec((rows, cols), lambda *g: (step_of(*g) // rep, 0)))
        out_shape.append(jax.ShapeDtypeStruct(w.shape[1:], BF16))
    return rep, args, in_specs, out_specs, out_shape


def _ffn_cast(step, rep, srcs, dsts):
    @pl.when(step % rep == 0)
    def _():
        for src, dst in zip(srcs, dsts):
            dst[...] = src[...].astype(BF16)


def _log_sigmoid(x):
    return jnp.minimum(x, 0.0) - jnp.log1p(jnp.exp(-jnp.abs(x)))


def _mlstm_core_kernel(q_ref, kt_ref, v_ref, rf_ref, wi_ref, wo_ref, o_ref, wi_out, wo_out, st_ref, c_ref,
                       *, nc, dv, unroll, cast_rep):
    L = LANES
    h = pl.program_id(1)
    ng = 4 * MLSTM_HEADS
    _ffn_cast(pl.program_id(0) * pl.num_programs(1) + h, cast_rep, (wi_ref, wo_ref), (wi_out, wo_out))

    def gate_rows(g):
        return rf_ref[0, pl.ds(MLSTM_HEADS * g + h, nc, stride=ng), :]

    lane = lax.broadcasted_iota(jnp.int32, (nc, L), 1)

    def prefix_sum(x):
        for s in (1, 2, 4, 8, 16, 32, 64):
            x = x + jnp.where(lane >= s, pltpu.roll(x, s, axis=1), 0.0)
        return x

    def suffix_sum(x):
        for s in (1, 2, 4, 8, 16, 32, 64):
            x = x + jnp.where(lane < L - s, pltpu.roll(x, L - s, axis=1), 0.0)
        return x

    for d in range(2):
        log_i = gate_rows(2 * d)
        log_f = _log_sigmoid(gate_rows(2 * d + 1))
        if d == 0:
            b = prefix_sum(log_f)
            b_last = jnp.broadcast_to(b[:, L - 1:L], (nc, L))
        else:
            b = suffix_sum(log_f)
            b_last = jnp.broadcast_to(b[:, 0:1], (nc, L))
        r = log_i - b
        a = b_last + r
        a_max = jnp.broadcast_to(jnp.max(a, axis=1, keepdims=True), (nc, L))
        st_ref[d, 0] = log_f
        st_ref[d, 1] = r
        st_ref[d, 2] = jnp.exp(a - a_max)
        st_ref[d, 3] = b_last
        st_ref[d, 4] = a_max

    def m_scan(i, carry):
        new = []
        for d, c in ((0, i), (1, nc - 1 - i)):
            row = pl.ds(c, 1)
            st_ref[d, 5, row, :] = carry[d]
            new.append(jnp.maximum(st_ref[d, 3, row, :] + carry[d], st_ref[d, 4, row, :]))
        return tuple(new)

    m0 = jnp.zeros((1, L), F32)
    lax.fori_loop(0, nc, m_scan, (m0, m0))
    for d in range(2):
        b_last, a_max, m_prev = st_ref[d, 3], st_ref[d, 4], st_ref[d, 5]
        m_new = jnp.maximum(b_last + m_prev, a_max)
        st_ref[d, 3] = jnp.exp(b_last + m_prev - m_new)
        st_ref[d, 2] = st_ref[d, 2] * jnp.exp(a_max - m_new)
        st_ref[d, 0] = st_ref[d, 0] * LOG2E
        st_ref[d, 1] = st_ref[d, 1] * LOG2E
        st_ref[d, 5] = m_prev * LOG2E

    c_ref[...] = jnp.zeros_like(c_ref)

    t_idx = lax.broadcasted_iota(jnp.int32, (L, L), 0)
    s_idx = lax.broadcasted_iota(jnp.int32, (L, L), 1)
    masks = (s_idx <= t_idx, s_idx >= t_idx)
    ones = jnp.ones((L, L), BF16)

    def tile3(x):
        return jnp.concatenate([x] * (dv // L + 1), axis=1)

    def body(i, carry, assign):
        items = []
        for u in range(unroll):
            c = i * unroll + u
            items += [(0, c), (1, nc - 1 - c)]

        work = []
        for d, c in items:
            rows = pl.ds(pl.multiple_of(c * L, L), L)
            qc = q_ref[0, rows, :]
            kt = kt_ref[0, c]
            vaug = jnp.concatenate([v_ref[0, rows, :], ones], axis=1)
            w = st_ref[d, 2, pl.ds(c, 1), :]
            scores = _dot(qc, kt)
            kv = _dot(kt * w.astype(BF16), vaug)
            work.append((rows, qc, vaug, scores, kv))

        states = [c_ref[0], c_ref[1]]
        prev_states = []
        for (d, c), (rows, qc, vaug, scores, kv) in zip(items, work):
            prev_states.append(states[d].astype(BF16))
            states[d] = tile3(st_ref[d, 3, pl.ds(c, 1), :]) * states[d] + kv
        c_ref[0] = states[0]
        c_ref[1] = states[1]

        for (d, c), (rows, qc, vaug, scores, kv), prev_state in zip(items, work, prev_states):
            log_f = st_ref[d, 0, pl.ds(c, 1), :]
            r = st_ref[d, 1, pl.ds(c, 1), :]
            m_prev = st_ref[d, 5, pl.ds(c, 1), :]
            mask = masks[d]
            r_masked = jnp.where(mask, r, NEG_INF)
            cm = jnp.max(r_masked, axis=1, keepdims=True)
            b_col = jnp.sum(jnp.where(mask, log_f, 0.0), axis=1, keepdims=True)
            mu = jnp.broadcast_to(jnp.maximum(m_prev[:, 0:1], cm), (L, L))
            decay = jnp.exp2(r_masked - mu)
            inter = jnp.exp2(m_prev - mu)
            lhs = jnp.concatenate([(scores * decay).astype(BF16), inter.astype(BF16) * qc], axis=1)
            both = _dot(lhs, jnp.concatenate([vaug, prev_state], axis=0))
            den = jnp.maximum(jnp.abs(both[:, dv:]), jnp.exp2(-(b_col + mu)))
            out = both[:, :dv] * tile3(1.0 / den)[:, :dv]
            if assign:
                o_ref[0, rows, :] = out
            else:
                o_ref[0, rows, :] += out
        return carry

    steps = nc // 2 // unroll
    lax.fori_loop(0, steps, functools.partial(body, assign=True), 0)
    lax.fori_loop(steps, 2 * steps, functools.partial(body, assign=False), 0)


def _mlstm_core(q, kt, v, rf, ffn_w_in, ffn_w_out, layer, *, unroll=8):
    B, S, hdk = q.shape
    H = MLSTM_HEADS
    dk = hdk // H
    dv = v.shape[-1] // H
    nc = S // LANES
    rep, w_args, w_in_specs, w_out_specs, w_out_shape = _ffn_cast_specs(
        ffn_w_in, ffn_w_out, layer, lambda b, h: b * H + h, B * H)
    return pl.pallas_call(
        functools.partial(_mlstm_core_kernel, nc=nc, dv=dv, unroll=min(unroll, nc // 2), cast_rep=rep),
        grid=(B, H),
        in_specs=[pl.BlockSpec((1, S, dk), lambda b, h: (b, 0, h)),
                  pl.BlockSpec((1, nc, dk, LANES), lambda b, h: (b, 0, h, 0)),
                  pl.BlockSpec((1, S, dv), lambda b, h: (b, 0, h)),
                  pl.BlockSpec((1, rf.shape[1], LANES), lambda b, h: (b, 0, 0))] + w_in_specs,
        out_specs=[pl.BlockSpec((1, S, dv), lambda b, h: (b, 0, h))] + w_out_specs,
        out_shape=[jax.ShapeDtypeStruct((B, S, H * dv), F32)] + w_out_shape,
        scratch_shapes=[pltpu.VMEM((2, 6, nc, LANES), F32),
                        pltpu.VMEM((2, dk, dv + LANES), F32)],
        compiler_params=_params(),
        name="mlstm_core",
    )(q, kt, v, rf, *w_args)


def _attn_weight_prep(w_ref, wt_ref):
    @pl.when((pl.program_id(0) == 0) & (pl.program_id(1) == 0))
    def _():
        for c in range(0, wt_ref.shape[0], LANES):
            wt_ref[c:c + LANES, :] = w_ref[:, c:c + LANES].T.astype(BF16)


def _attn_proj_rows(x_rows, pos_row, invf, nw, wt_ref, qt_ref, k_ref, vt_ref, chunk0):
    hd = ATTN_HEAD_DIM
    half = hd // 2
    nq = ATTN_Q_HEADS * hd
    nk = ATTN_KV_HEADS * hd
    scale = LOG2E * hd ** -0.5
    piece = 8 * hd
    sub = x_rows.shape[0]
    hn = _rmsnorm(x_rows, nw).astype(BF16)
    ang = invf * pos_row.astype(F32)
    cos = jnp.cos(ang)
    sin = jnp.sin(ang)
    chunks = [(chunk0 + jj, slice(jj * LANES, (jj + 1) * LANES)) for jj in range(sub // LANES)]

    def rope(xh):
        x1, x2 = xh[:half], xh[half:]
        return jnp.concatenate([x1 * cos - x2 * sin, x2 * cos + x1 * sin], axis=0)

    qp = _dot_nt(wt_ref[:nq, :], hn)
    kv = _dot_nt(wt_ref[nq:, :], hn)
    for lo in range(0, nq, piece):
        qt = jnp.concatenate([rope(qp[lo + h * hd:lo + (h + 1) * hd]) * scale for h in range(piece // hd)],
                             axis=0).astype(BF16)
        for j, cols in chunks:
            qt_ref[0, j, lo:lo + piece, :] = qt[:, cols]
    vt = kv[nk:].astype(BF16)
    for j, cols in chunks:
        vt_ref[0, j] = vt[:, cols]
    kt = jnp.concatenate([rope(kv[g * hd:(g + 1) * hd]) for g in range(ATTN_KV_HEADS)], axis=0)
    for j, cols in chunks:
        for f in range(nk // LANES):
            k_ref[0, j * LANES:(j + 1) * LANES, f * LANES:(f + 1) * LANES] = (
                kt[f * LANES:(f + 1) * LANES, cols].T.astype(BF16))


def _attn_proj_kernel(x_ref, nw_ref, w_ref, pos_ref, invf_ref, qt_ref, k_ref, vt_ref, wt_ref, *, tm, nsub):
    _attn_weight_prep(w_ref, wt_ref)
    sub = tm // nsub
    for r in range(nsub):
        rs = slice(r * sub, (r + 1) * sub)
        _attn_proj_rows(x_ref[0, rs, :], pos_ref[0, :, rs], invf_ref[...], nw_ref[...], wt_ref,
                        qt_ref, k_ref, vt_ref, r * (sub // LANES))


def _attn_proj_specs(S, D, tm, w_in, layer):
    nq = ATTN_Q_HEADS * ATTN_HEAD_DIM
    nk = ATTN_KV_HEADS * ATTN_HEAD_DIM
    nj = tm // LANES
    nb = S // LANES
    in_specs = [_resident((1, D)),
                pl.BlockSpec((None,) + w_in.shape[1:], lambda b, i: (layer, 0, 0),
                             pipeline_mode=pl.Buffered(1)),
                pl.BlockSpec((1, 1, tm), lambda b, i: (b, 0, i)),
                _resident((ATTN_HEAD_DIM // 2, 1))]
    out_specs = [pl.BlockSpec((1, nj, nq, LANES), lambda b, i: (b, i, 0, 0)),
                 pl.BlockSpec((1, tm, nk), lambda b, i: (b, i, 0)),
                 pl.BlockSpec((1, nj, nk, LANES), lambda b, i: (b, i, 0, 0))]
    out_shape = lambda B: [jax.ShapeDtypeStruct((B, nb, nq, LANES), BF16),
                           jax.ShapeDtypeStruct((B, S, nk), BF16),
                           jax.ShapeDtypeStruct((B, nb, nk, LANES), BF16)]
    scratch = [pltpu.VMEM((nq + 2 * nk, D), BF16)]
    return in_specs, out_specs, out_shape, scratch


def _attn_proj(x, norm_w, w_in, layer, pos_rows, inv_freq, *, tm=1024, nsub=4):
    B, S, D = x.shape
    tm = min(tm, S)
    p_in, p_out, p_shape, p_scratch = _attn_proj_specs(S, D, tm, w_in, layer)
    return pl.pallas_call(
        functools.partial(_attn_proj_kernel, tm=tm, nsub=nsub),
        grid=(B, S // tm),
        in_specs=[pl.BlockSpec((1, tm, D), lambda b, i: (b, i, 0))] + p_in,
        out_specs=p_out,
        out_shape=p_shape(B),
        scratch_shapes=p_scratch,
        compiler_params=_params(),
        name="attn_proj",
    )(x, norm_w.reshape(1, D), w_in, pos_rows, inv_freq)


def _attn_core_kernel(qt_ref, kp_ref, kc_ref, kn_ref, vp_ref, vc_ref, vn_ref, sink_ref, wi_ref, wo_ref,
                      o_ref, wi_out, wo_out, k_all, v_all, *, nb, nblk, cast_rep):
    L = LANES
    hd = ATTN_HEAD_DIM
    G = ATTN_GROUP
    c = pl.program_id(1)
    _ffn_cast(pl.program_id(0) * pl.num_programs(1) + c, cast_rep, (wi_ref, wo_ref), (wi_out, wo_out))
    k_all[0:L] = kp_ref[0]
    k_all[L:(nblk + 1) * L] = kc_ref[0]
    k_all[(nblk + 1) * L:] = kn_ref[0]
    v_all[0] = vp_ref[0, 0]
    v_all[1:nblk + 1] = vc_ref[0]
    v_all[nblk + 1] = vn_ref[0, 0]

    key = lax.broadcasted_iota(jnp.int32, (L, G * L), 0)
    qry = lax.broadcasted_iota(jnp.int32, (L, G * L), 1) % L
    prev_mask = jnp.where(key >= qry, 0.0, NEG_INF)
    next_mask = jnp.where(key <= qry, 0.0, NEG_INF)
    zeros = jnp.zeros((hd, G * L), BF16)
    ones = jnp.ones((16, 3 * L), BF16)

    def scores(j, g):
        qg = jnp.concatenate([qt_ref[0, j, (G * g + i) * hd:(G * g + i + 1) * hd, :] for i in range(G)],
                             axis=1)
        qz = jnp.concatenate([qg, zeros] if g % 2 == 0 else [zeros, qg], axis=0)
        return _dot(k_all[j * L:(j + 3) * L, (g // 2) * L:(g // 2 + 1) * L], qz)

    items = [(j, g) for j in range(nblk) for g in range(ATTN_KV_HEADS)]
    ahead = 2
    pending = [scores(*it) for it in items[:ahead]]
    for n, (j, g) in enumerate(items):
        s = pending.pop(0)
        if n + ahead < len(items):
            pending.append(scores(*items[n + ahead]))
        if g == 0:
            blk = c * nblk + j
            prev_bias = prev_mask + jnp.where(blk > 0, 0.0, NEG_INF)
            next_bias = next_mask + jnp.where(blk < nb - 1, 0.0, NEG_INF)
        s = [s[0:L] + prev_bias, s[L:2 * L], s[2 * L:] + next_bias]
        sink = sink_ref[g:g + 1, :]
        m8 = functools.reduce(jnp.maximum, [sj.reshape(L // 8, 8, G * L).max(axis=0) for sj in s])
        m = jnp.maximum(sink, jnp.max(m8, axis=0, keepdims=True))
        p = jnp.concatenate([jnp.exp2(sj - m).astype(BF16) for sj in s], axis=0)
        vg = jnp.concatenate([v_all[j + i, g * hd:(g + 1) * hd, :] for i in range(3)], axis=1)
        acc = _dot(jnp.concatenate([vg, ones], axis=0), p)
        denom = acc[hd:hd + 1] + jnp.exp2(sink - m)
        out = acc[:hd] * (1.0 / denom)
        for pair in range(G // 2):
            tile = jnp.concatenate([out[:, (2 * pair) * L:(2 * pair + 1) * L],
                                    out[:, (2 * pair + 1) * L:(2 * pair + 2) * L]], axis=0)
            col = (G * g + 2 * pair) * hd
            o_ref[0, j * L:(j + 1) * L, col:col + L] = tile.T.astype(BF16)


def _attn_core(qt, k, vt, sink_rows, ffn_w_in, ffn_w_out, layer, *, nblk=8):
    B, nb, nq, L = qt.shape
    nk = k.shape[-1]
    S = nb * L
    nblk = min(nblk, nb)
    steps = nb // nblk
    prev = lambda b, c: (b, jnp.maximum(c * nblk - 1, 0), 0)
    cur = lambda b, c: (b, c, 0)
    nxt = lambda b, c: (b, jnp.minimum((c + 1) * nblk, nb - 1), 0)
    four = lambda f: (lambda b, c: f(b, c) + (0,))
    rep, w_args, w_in_specs, w_out_specs, w_out_shape = _ffn_cast_specs(
        ffn_w_in, ffn_w_out, layer, lambda b, c: b * steps + c, B * steps)
    return pl.pallas_call(
        functools.partial(_attn_core_kernel, nb=nb, nblk=nblk, cast_rep=rep),
        grid=(B, steps),
        in_specs=[pl.BlockSpec((1, nblk, nq, L), four(cur)),
                  pl.BlockSpec((1, L, nk), prev), pl.BlockSpec((1, nblk * L, nk), cur),
                  pl.BlockSpec((1, L, nk), nxt),
                  pl.BlockSpec((1, 1, nk, L), four(prev)), pl.BlockSpec((1, nblk, nk, L), four(cur)),
                  pl.BlockSpec((1, 1, nk, L), four(nxt)),
                  _resident(sink_rows.shape)] + w_in_specs,
        out_specs=[pl.BlockSpec((1, nblk * L, nq), cur)] + w_out_specs,
        out_shape=[jax.ShapeDtypeStruct((B, S, nq), BF16)] + w_out_shape,
        scratch_shapes=[pltpu.VMEM(((nblk + 2) * L, nk), BF16),
                        pltpu.VMEM((nblk + 2, nk, L), BF16)],
        compiler_params=_params(),
        name="attn_core",
    )(qt, k, k, k, vt, vt, vt, sink_rows, *w_args)


def _mlstm_mixer(x, norm_w, w_t, layer, b_gate, ffn_w_in, ffn_w_out, ffn_layer):
    bias_rows = jnp.broadcast_to(b_gate.astype(F32).reshape(-1, 1), (b_gate.shape[0], LANES))
    q, kt, v, og, rf = _mlstm_proj(x, norm_w, w_t, layer, bias_rows)
    hs, wi, wo = _mlstm_core(q, kt, v, rf, ffn_w_in, ffn_w_out, ffn_layer)
    return (hs, og), wi, wo


def _attn_mixer(x, pos_rows, inv_freq, norm_w, w_in, layer, sink, ffn_w_in, ffn_w_out, ffn_layer):
    G, L = ATTN_GROUP, LANES
    qt, k, vt = _attn_proj(x, norm_w, w_in, layer, pos_rows, inv_freq)
    sink_rows = jnp.repeat(LOG2E * sink.astype(F32).reshape(ATTN_KV_HEADS, G), L, axis=1)
    a, wi, wo = _attn_core(qt, k, vt, sink_rows, ffn_w_in, ffn_w_out, ffn_layer)
    return (a,), wi, wo


def kernel(x, positions, norm_mix_w, norm_ffn_w, norm_final_w, mlstm_w_in, mlstm_b_gate, mlstm_norm_w, mlstm_w_out, attn_w_in, attn_sink, attn_w_out, ffn_w_in, ffn_w_out):
    depth = norm_mix_w.shape[0]
    B, S = positions.shape
    half = ATTN_HEAD_DIM // 2
    inv_freq = (ROPE_THETA ** (-jnp.arange(half, dtype=F32) / half)).reshape(half, 1)
    pos_rows = positions.reshape(B, 1, S)
    mlstm_wo, attn_wo = mlstm_w_out.astype(BF16), attn_w_out.astype(BF16)
    mlstm_wt = jnp.swapaxes(mlstm_w_in, 1, 2)
    for i in range(depth):
        j = i // 2
        final_w = norm_final_w if i == depth - 1 else None
        if i % 2 == 0:
            (hs, og), wi, wo = _mlstm_mixer(x, norm_mix_w[i], mlstm_wt, j, mlstm_b_gate[j],
                                            ffn_w_in, ffn_w_out, i)
            x = _layer_tail("mlstm", (hs, og, mlstm_norm_w[j]), x, mlstm_wo, j, norm_ffn_w[i], wi, wo,
                            final_w)
        else:
            mixed, wi, wo = _attn_mixer(x, pos_rows, inv_freq, norm_mix_w[i], attn_w_in, j, attn_sink[j],
                                        ffn_w_in, ffn_w_out, i)
            x = _layer_tail("attn", mixed, x, attn_wo, j, norm_ffn_w[i], wi, wo, final_w)
    return x
```

```python
import functools

import jax
import jax.numpy as jnp
from jax import lax
from jax.experimental import pallas as pl
from jax.experimental.pallas import tpu as pltpu

F32 = jnp.float32
BF16 = jnp.bfloat16

EPS = 1e-6
LANES = 128

MLSTM_HEADS = 4
MLSTM_CHUNK = 128
ATTN_HEAD_DIM = 64
ATTN_Q_HEADS = 16
ATTN_KV_HEADS = 4
ATTN_GROUP = ATTN_Q_HEADS // ATTN_KV_HEADS
ATTN_BLOCK = 128
ROPE_THETA = 10000.0

NEG_INF = float("-inf")
LOG2E = 1.4426950408889634
VMEM_LIMIT = 56 * 1024 * 1024


def _params():
    return pltpu.CompilerParams(dimension_semantics=("arbitrary", "arbitrary"),
                                vmem_limit_bytes=VMEM_LIMIT)


def _resident(shape):
    return pl.BlockSpec(shape, lambda *_: (0,) * len(shape), pipeline_mode=pl.Buffered(1))


def _rmsnorm(x, w):
    ms = jnp.mean(x * x, axis=-1, keepdims=True)
    return x * lax.rsqrt(ms + EPS) * w


def _sigmoid(x):
    return 1.0 / (1.0 + jnp.exp(-x))


def _dot(a, b):
    return jnp.dot(a, b, preferred_element_type=F32)


def _dot_nt(a, b):
    return lax.dot_general(a, b, (((1,), (1,)), ((), ())), preferred_element_type=F32)


def _layer_tail_kernel(*refs, d_ff, tf, nsub, mixer, final):
    it = iter(refs)
    if mixer == "mlstm":
        hs_ref, og_ref, hw_ref = next(it), next(it), next(it)
    else:
        a_ref = next(it)
    x_ref, wo_ref, nw_ref, win_ref, wout_ref = (next(it) for _ in range(5))
    fw_ref = next(it) if final else None
    o_ref, hn_ref, act_ref = (next(it) for _ in range(3))

    tm = x_ref.shape[1]
    subs = [slice(r * (tm // nsub), (r + 1) * (tm // nsub)) for r in range(nsub)]
    for rs in subs:
        if mixer == "mlstm":
            dv = hs_ref.shape[-1] // MLSTM_HEADS
            parts = []
            for h in range(MLSTM_HEADS):
                hs = hs_ref[0, rs, h * dv:(h + 1) * dv]
                ms = jnp.mean(hs * hs, axis=-1, keepdims=True)
                parts.append(hs * lax.rsqrt(ms + EPS))
            y = (jnp.concatenate(parts, axis=-1) * hw_ref[...] * og_ref[0, rs, :].astype(F32)).astype(BF16)
        else:
            y = a_ref[0, rs, :]
        o_ref[0, rs, :] = x_ref[0, rs, :] + _dot(y, wo_ref[...])
    for rs in subs:
        hn_ref[rs, :] = _rmsnorm(o_ref[0, rs, :], nw_ref[...]).astype(BF16)
    for rs in subs:
        for j in range(d_ff // tf):
            hn = hn_ref[rs, :]
            g = _dot(hn, win_ref[:, j * tf:(j + 1) * tf])
            u = _dot(hn, win_ref[:, d_ff + j * tf:d_ff + (j + 1) * tf])
            act_ref[rs, j * tf:(j + 1) * tf] = (g * _sigmoid(g) * u).astype(BF16)
    for rs in subs:
        out = o_ref[0, rs, :] + _dot(act_ref[rs, :], wout_ref[...])
        if final:
            out = _rmsnorm(out, fw_ref[...])
        o_ref[0, rs, :] = out


def _layer_tail(mixer, mixer_args, x, w_o, layer_o, norm_w, w_in, w_out, final_w=None,
                *, tf=256):
    B, S, D = x.shape
    tm, nsub = (1024, 4) if mixer == "attn" else (512, 2)
    tm = min(tm, S)
    d_ff = w_out.shape[0]
    final = final_w is not None
    row = pl.BlockSpec((1, tm, D), lambda b, i: (b, i, 0))

    def stacked(shape, l):
        return pl.BlockSpec((None,) + shape, lambda b, i: (l,) + (0,) * len(shape),
                            pipeline_mode=pl.Buffered(1))

    if mixer == "mlstm":
        hs, og, head_w = mixer_args
        args = [hs, og, head_w.reshape(1, D)]
        in_specs = [row, row, _resident((1, D))]
    else:
        args = list(mixer_args)
        in_specs = [row]
    args += [x, w_o, norm_w.reshape(1, D), w_in, w_out]
    in_specs += [row, stacked((w_o.shape[1], D), layer_o), _resident((1, D)),
                 _resident((D, 2 * d_ff)), _resident((d_ff, D))]
    if final:
        args.append(final_w.reshape(1, D))
        in_specs.append(_resident((1, D)))
    return pl.pallas_call(
        functools.partial(_layer_tail_kernel, d_ff=d_ff, tf=tf, nsub=nsub, mixer=mixer, final=final),
        grid=(B, S // tm),
        in_specs=in_specs,
        out_specs=row,
        out_shape=jax.ShapeDtypeStruct((B, S, D), F32),
        scratch_shapes=[pltpu.VMEM((tm, D), BF16), pltpu.VMEM((tm, d_ff), BF16)],
        compiler_params=_params(),
        name=mixer + "_tail",
    )(*args)


def _mlstm_proj_kernel(x_ref, nw_ref, wt_ref, bias_ref,
                       q_ref, kt_ref, v_ref, og_ref, rf_ref, w_ref, wkt_ref, *, dk, tm, nsub):
    d = v_ref.shape[-1]
    hdk = q_ref.shape[-1]
    ng = rf_ref.shape[1] // (tm // LANES)
    sub = tm // nsub

    @pl.when((pl.program_id(0) == 0) & (pl.program_id(1) == 0))
    def _():
        for c in range(0, hdk, LANES):
            w_ref[:, c:c + LANES] = wt_ref[c:c + LANES, :].T.astype(BF16)
            wkt_ref[c:c + LANES, :] = wt_ref[hdk + c:hdk + c + LANES, :].astype(BF16)
        for c in range(0, 2 * d, LANES):
            w_ref[:, hdk + c:hdk + c + LANES] = wt_ref[2 * hdk + c:2 * hdk + c + LANES, :].T.astype(BF16)
        wkt_ref[hdk:, :] = wt_ref[2 * hdk + 2 * d:, :].astype(BF16)

    for r in range(nsub):
        rs = slice(r * sub, (r + 1) * sub)
        hn = _rmsnorm(x_ref[0, rs, :], nw_ref[...]).astype(BF16)
        og_ref[0, rs, :] = _sigmoid(_dot(hn, w_ref[:, hdk + d:])).astype(BF16)
        v_ref[0, rs, :] = _dot(hn, w_ref[:, hdk:hdk + d]).astype(BF16)
        q_ref[0, rs, :] = (_dot(hn, w_ref[:, :hdk]) * (dk ** -0.5)).astype(BF16)
        kg = _dot_nt(wkt_ref[...], hn)
        kt = kg[:hdk].astype(BF16)
        for jj in range(sub // LANES):
            j = r * (sub // LANES) + jj
            kt_ref[0, j] = kt[:, jj * LANES:(jj + 1) * LANES]
            rf_ref[0, j * ng:(j + 1) * ng, :] = kg[hdk:, jj * LANES:(jj + 1) * LANES] + bias_ref[...]


def _mlstm_proj(x, norm_w, w_t, layer, bias_rows, *, tm=1024, nsub=4):
    B, S, D = x.shape
    H = MLSTM_HEADS
    hdk = D // 2
    dk = hdk // H
    tm = min(tm, S)
    ng = bias_rows.shape[0]
    nj = tm // LANES
    NC = S // LANES
    row = lambda b, i: (b, i, 0)
    return pl.pallas_call(
        functools.partial(_mlstm_proj_kernel, dk=dk, tm=tm, nsub=nsub),
        grid=(B, S // tm),
        in_specs=[pl.BlockSpec((1, tm, D), row), _resident((1, D)),
                  pl.BlockSpec((None,) + w_t.shape[1:], lambda b, i: (layer, 0, 0),
                               pipeline_mode=pl.Buffered(1)),
                  _resident((ng, LANES))],
        out_specs=[pl.BlockSpec((1, tm, hdk), row),
                   pl.BlockSpec((1, nj, hdk, LANES), lambda b, i: (b, i, 0, 0)),
                   pl.BlockSpec((1, tm, D), row),
                   pl.BlockSpec((1, tm, D), row),
                   pl.BlockSpec((1, nj * ng, LANES), row)],
        out_shape=[jax.ShapeDtypeStruct((B, S, hdk), BF16),
                   jax.ShapeDtypeStruct((B, NC, hdk, LANES), BF16),
                   jax.ShapeDtypeStruct((B, S, D), BF16),
                   jax.ShapeDtypeStruct((B, S, D), BF16),
                   jax.ShapeDtypeStruct((B, NC * ng, LANES), F32)],
        scratch_shapes=[pltpu.VMEM((D, hdk + 2 * D), BF16), pltpu.VMEM((hdk + ng, D), BF16)],
        compiler_params=_params(),
        name="mlstm_proj",
    )(x, norm_w.reshape(1, D), w_t, bias_rows)


FFN_CAST_BLOCKS = 8


def _ffn_cast_specs(ffn_w_in, ffn_w_out, layer, step_of, total_steps):
    nblk = min(FFN_CAST_BLOCKS, total_steps)
    rep = total_steps // nblk
    args, in_specs, out_specs, out_shape = [], [], [], []
    for w in (ffn_w_in, ffn_w_out):
        rows, cols = w.shape[1] // nblk, w.shape[2]
        args.append(w)
        in_specs.append(pl.BlockSpec((None, rows, cols),
                                     lambda *g, l=layer: (l, step_of(*g) // rep, 0)))
        out_specs.append(pl.BlockSpec((rows, cols), lambda *g: (step_of(*g) // rep, 0)))
        out_shape.append(jax.ShapeDtypeStruct(w.shape[1:], BF16))
    return rep, args, in_specs, out_specs, out_shape


def _ffn_cast(step, rep, srcs, dsts):
    @pl.when(step % rep == 0)
    def _():
        for src, dst in zip(srcs, dsts):
            dst[...] = src[...].astype(BF16)


def _log_sigmoid(x):
    return jnp.minimum(x, 0.0) - jnp.log1p(jnp.exp(-jnp.abs(x)))


def _mlstm_core_kernel(q_ref, kt_ref, v_ref, rf_ref, wi_ref, wo_ref, o_ref, wi_out, wo_out, st_ref, c_ref,
                       *, nc, dv, unroll, cast_rep):
    L = LANES
    h = pl.program_id(1)
    ng = 4 * MLSTM_HEADS
    _ffn_cast(pl.program_id(0) * pl.num_programs(1) + h, cast_rep, (wi_ref, wo_ref), (wi_out, wo_out))

    def gate_rows(g):
        return rf_ref[0, pl.ds(MLSTM_HEADS * g + h, nc, stride=ng), :]

    lane = lax.broadcasted_iota(jnp.int32, (nc, L), 1)

    def prefix_sum(x):
        for s in (1, 2, 4, 8, 16, 32, 64):
            x = x + jnp.where(lane >= s, pltpu.roll(x, s, axis=1), 0.0)
        return x

    def suffix_sum(x):
        for s in (1, 2, 4, 8, 16, 32, 64):
            x = x + jnp.where(lane < L - s, pltpu.roll(x, L - s, axis=1), 0.0)
        return x

    for d in range(2):
        log_i = gate_rows(2 * d)
        log_f = _log_sigmoid(gate_rows(2 * d + 1))
        if d == 0:
            b = prefix_sum(log_f)
            b_last = jnp.broadcast_to(b[:, L - 1:L], (nc, L))
        else:
            b = suffix_sum(log_f)
            b_last = jnp.broadcast_to(b[:, 0:1], (nc, L))
        r = log_i - b
        a = b_last + r
        a_max = jnp.broadcast_to(jnp.max(a, axis=1, keepdims=True), (nc, L))
        st_ref[d, 0] = log_f
        st_ref[d, 1] = r
        st_ref[d, 2] = jnp.exp(a - a_max)
        st_ref[d, 3] = b_last
        st_ref[d, 4] = a_max

    def m_scan(i, carry):
        new = []
        for d, c in ((0, i), (1, nc - 1 - i)):
            row = pl.ds(c, 1)
            st_ref[d, 5, row, :] = carry[d]
            new.append(jnp.maximum(st_ref[d, 3, row, :] + carry[d], st_ref[d, 4, row, :]))
        return tuple(new)

    m0 = jnp.zeros((1, L), F32)
    lax.fori_loop(0, nc, m_scan, (m0, m0), unroll=8)
    for d in range(2):
        b_last, a_max, m_prev = st_ref[d, 3], st_ref[d, 4], st_ref[d, 5]
        m_new = jnp.maximum(b_last + m_prev, a_max)
        st_ref[d, 3] = jnp.exp(b_last + m_prev - m_new)
        st_ref[d, 2] = st_ref[d, 2] * jnp.exp(a_max - m_new)
        st_ref[d, 0] = st_ref[d, 0] * LOG2E
        st_ref[d, 1] = st_ref[d, 1] * LOG2E
        st_ref[d, 5] = m_prev * LOG2E

    c_ref[...] = jnp.zeros_like(c_ref)

    t_idx = lax.broadcasted_iota(jnp.int32, (L, L), 0)
    s_idx = lax.broadcasted_iota(jnp.int32, (L, L), 1)
    masks = (s_idx <= t_idx, s_idx >= t_idx)
    ones = jnp.ones((L, L), BF16)

    def tile3(x):
        return jnp.concatenate([x] * (dv // L + 1), axis=1)

    def body(i, carry, assign):
        items = []
        for u in range(unroll):
            c = i * unroll + u
            items += [(0, c), (1, nc - 1 - c)]

        work = []
        for d, c in items:
            rows = pl.ds(pl.multiple_of(c * L, L), L)
            qc = q_ref[0, rows, :]
            kt = kt_ref[0, c]
            vaug = jnp.concatenate([v_ref[0, rows, :], ones], axis=1)
            w = st_ref[d, 2, pl.ds(c, 1), :]
            scores = _dot(qc, kt)
            kv = _dot(kt * w.astype(BF16), vaug)
            work.append((rows, qc, vaug, scores, kv))

        states = [c_ref[0], c_ref[1]]
        prev_states = []
        for (d, c), (rows, qc, vaug, scores, kv) in zip(items, work):
            prev_states.append(states[d].astype(BF16))
            states[d] = tile3(st_ref[d, 3, pl.ds(c, 1), :]) * states[d] + kv
        c_ref[0] = states[0]
        c_ref[1] = states[1]

        for (d, c), (rows, qc, vaug, scores, kv), prev_state in zip(items, work, prev_states):
            log_f = st_ref[d, 0, pl.ds(c, 1), :]
            r = st_ref[d, 1, pl.ds(c, 1), :]
            m_prev = st_ref[d, 5, pl.ds(c, 1), :]
            mask = masks[d]
            r_masked = jnp.where(mask, r, NEG_INF)
            cm = jnp.max(r_masked, axis=1, keepdims=True)
            b_col = jnp.sum(jnp.where(mask, log_f, 0.0), axis=1, keepdims=True)
            mu = jnp.broadcast_to(jnp.maximum(m_prev[:, 0:1], cm), (L, L))
            decay = jnp.exp2(r_masked - mu)
            inter = jnp.exp2(m_prev - mu)
            lhs = jnp.concatenate([(scores * decay).astype(BF16), inter.astype(BF16) * qc], axis=1)
            both = _dot(lhs, jnp.concatenate([vaug, prev_state], axis=0))
            den = jnp.maximum(jnp.abs(both[:, dv:]), jnp.exp2(-(b_col + mu)))
            out = both[:, :dv] * tile3(1.0 / den)[:, :dv]
            if assign:
                o_ref[0, rows, :] = out
            else:
                o_ref[0, rows, :] += out
        return carry

    steps = nc // 2 // unroll
    lax.fori_loop(0, steps, functools.partial(body, assign=True), 0)
    lax.fori_loop(steps, 2 * steps, functools.partial(body, assign=False), 0)


def _mlstm_core(q, kt, v, rf, ffn_w_in, ffn_w_out, layer, *, unroll=8):
    B, S, hdk = q.shape
    H = MLSTM_HEADS
    dk = hdk // H
    dv = v.shape[-1] // H
    nc = S // LANES
    rep, w_args, w_in_specs, w_out_specs, w_out_shape = _ffn_cast_specs(
        ffn_w_in, ffn_w_out, layer, lambda b, h: b * H + h, B * H)
    return pl.pallas_call(
        functools.partial(_mlstm_core_kernel, nc=nc, dv=dv, unroll=min(unroll, nc // 2), cast_rep=rep),
        grid=(B, H),
        in_specs=[pl.BlockSpec((1, S, dk), lambda b, h: (b, 0, h)),
                  pl.BlockSpec((1, nc, dk, LANES), lambda b, h: (b, 0, h, 0)),
                  pl.BlockSpec((1, S, dv), lambda b, h: (b, 0, h)),
                  pl.BlockSpec((1, rf.shape[1], LANES), lambda b, h: (b, 0, 0))] + w_in_specs,
        out_specs=[pl.BlockSpec((1, S, dv), lambda b, h: (b, 0, h))] + w_out_specs,
        out_shape=[jax.ShapeDtypeStruct((B, S, H * dv), F32)] + w_out_shape,
        scratch_shapes=[pltpu.VMEM((2, 6, nc, LANES), F32),
                        pltpu.VMEM((2, dk, dv + LANES), F32)],
        compiler_params=_params(),
        name="mlstm_core",
    )(q, kt, v, rf, *w_args)


def _attn_weight_prep(w_ref, wt_ref):
    @pl.when((pl.program_id(0) == 0) & (pl.program_id(1) == 0))
    def _():
        for c in range(0, wt_ref.shape[0], LANES):
            wt_ref[c:c + LANES, :] = w_ref[:, c:c + LANES].T.astype(BF16)


def _attn_proj_rows(x_rows, pos_row, invf, nw, wt_ref, qt_ref, k_ref, vt_ref, chunk0):
    hd = ATTN_HEAD_DIM
    half = hd // 2
    nq = ATTN_Q_HEADS * hd
    nk = ATTN_KV_HEADS * hd
    scale = LOG2E * hd ** -0.5
    piece = 8 * hd
    sub = x_rows.shape[0]
    hn = _rmsnorm(x_rows, nw).astype(BF16)
    ang = invf * pos_row.astype(F32)
    cos = jnp.cos(ang)
    sin = jnp.sin(ang)
    chunks = [(chunk0 + jj, slice(jj * LANES, (jj + 1) * LANES)) for jj in range(sub // LANES)]

    def rope(xh):
        x1, x2 = xh[:half], xh[half:]
        return jnp.concatenate([x1 * cos - x2 * sin, x2 * cos + x1 * sin], axis=0)

    qp = _dot_nt(wt_ref[:nq, :], hn)
    kv = _dot_nt(wt_ref[nq:, :], hn)
    for lo in range(0, nq, piece):
        qt = jnp.concatenate([rope(qp[lo + h * hd:lo + (h + 1) * hd]) * scale for h in range(piece // hd)],
                             axis=0).astype(BF16)
        for j, cols in chunks:
            qt_ref[0, j, lo:lo + piece, :] = qt[:, cols]
    vt = kv[nk:].astype(BF16)
    for j, cols in chunks:
        vt_ref[0, j] = vt[:, cols]
    kt = jnp.concatenate([rope(kv[g * hd:(g + 1) * hd]) for g in range(ATTN_KV_HEADS)], axis=0)
    for j, cols in chunks:
        for f in range(nk // LANES):
            k_ref[0, j * LANES:(j + 1) * LANES, f * LANES:(f + 1) * LANES] = (
                kt[f * LANES:(f + 1) * LANES, cols].T.astype(BF16))


def _attn_proj_kernel(x_ref, nw_ref, w_ref, pos_ref, invf_ref, qt_ref, k_ref, vt_ref, wt_ref, *, tm, nsub):
    _attn_weight_prep(w_ref, wt_ref)
    sub = tm // nsub
    for r in range(nsub):
        rs = slice(r * sub, (r + 1) * sub)
        _attn_proj_rows(x_ref[0, rs, :], pos_ref[0, :, rs], invf_ref[...], nw_ref[...], wt_ref,
                        qt_ref, k_ref, vt_ref, r * (sub // LANES))


def _attn_proj_specs(S, D, tm, w_in, layer):
    nq = ATTN_Q_HEADS * ATTN_HEAD_DIM
    nk = ATTN_KV_HEADS * ATTN_HEAD_DIM
    nj = tm // LANES
    nb = S // LANES
    in_specs = [_resident((1, D)),
                pl.BlockSpec((None,) + w_in.shape[1:], lambda b, i: (layer, 0, 0),
                             pipeline_mode=pl.Buffered(1)),
                pl.BlockSpec((1, 1, tm), lambda b, i: (b, 0, i)),
                _resident((ATTN_HEAD_DIM // 2, 1))]
    out_specs = [pl.BlockSpec((1, nj, nq, LANES), lambda b, i: (b, i, 0, 0)),
                 pl.BlockSpec((1, tm, nk), lambda b, i: (b, i, 0)),
                 pl.BlockSpec((1, nj, nk, LANES), lambda b, i: (b, i, 0, 0))]
    out_shape = lambda B: [jax.ShapeDtypeStruct((B, nb, nq, LANES), BF16),
                           jax.ShapeDtypeStruct((B, S, nk), BF16),
                           jax.ShapeDtypeStruct((B, nb, nk, LANES), BF16)]
    scratch = [pltpu.VMEM((nq + 2 * nk, D), BF16)]
    return in_specs, out_specs, out_shape, scratch


def _attn_proj(x, norm_w, w_in, layer, pos_rows, inv_freq, *, tm=1024, nsub=4):
    B, S, D = x.shape
    tm = min(tm, S)
    p_in, p_out, p_shape, p_scratch = _attn_proj_specs(S, D, tm, w_in, layer)
    return pl.pallas_call(
        functools.partial(_attn_proj_kernel, tm=tm, nsub=nsub),
        grid=(B, S // tm),
        in_specs=[pl.BlockSpec((1, tm, D), lambda b, i: (b, i, 0))] + p_in,
        out_specs=p_out,
        out_shape=p_shape(B),
        scratch_shapes=p_scratch,
        compiler_params=_params(),
        name="attn_proj",
    )(x, norm_w.reshape(1, D), w_in, pos_rows, inv_freq)


def _attn_core_kernel(qt_ref, kp_ref, kc_ref, kn_ref, vp_ref, vc_ref, vn_ref, sink_ref, wi_ref, wo_ref,
                      o_ref, wi_out, wo_out, k_all, v_all, *, nb, nblk, cast_rep):
    L = LANES
    hd = ATTN_HEAD_DIM
    G = ATTN_GROUP
    c = pl.program_id(1)
    _ffn_cast(pl.program_id(0) * pl.num_programs(1) + c, cast_rep, (wi_ref, wo_ref), (wi_out, wo_out))
    k_all[0:L] = kp_ref[0]
    k_all[L:(nblk + 1) * L] = kc_ref[0]
    k_all[(nblk + 1) * L:] = kn_ref[0]
    v_all[0] = vp_ref[0, 0]
    v_all[1:nblk + 1] = vc_ref[0]
    v_all[nblk + 1] = vn_ref[0, 0]

    key = lax.broadcasted_iota(jnp.int32, (L, G * L), 0)
    qry = lax.broadcasted_iota(jnp.int32, (L, G * L), 1) % L
    prev_mask = jnp.where(key >= qry, 0.0, NEG_INF)
    next_mask = jnp.where(key <= qry, 0.0, NEG_INF)
    zeros = jnp.zeros((hd, G * L), BF16)
    ones = jnp.ones((16, 3 * L), BF16)

    def scores(j, g):
        qg = jnp.concatenate([qt_ref[0, j, (G * g + i) * hd:(G * g + i + 1) * hd, :] for i in range(G)],
                             axis=1)
        qz = jnp.concatenate([qg, zeros] if g % 2 == 0 else [zeros, qg], axis=0)
        return _dot(k_all[j * L:(j + 3) * L, (g // 2) * L:(g // 2 + 1) * L], qz)

    items = [(j, g) for j in range(nblk) for g in range(ATTN_KV_HEADS)]
    ahead = 2
    pending = [scores(*it) for it in items[:ahead]]
    for n, (j, g) in enumerate(items):
        s = pending.pop(0)
        if n + ahead < len(items):
            pending.append(scores(*items[n + ahead]))
        if g == 0:
            blk = c * nblk + j
            prev_bias = prev_mask + jnp.where(blk > 0, 0.0, NEG_INF)
            next_bias = next_mask + jnp.where(blk < nb - 1, 0.0, NEG_INF)
        s = [s[0:L] + prev_bias, s[L:2 * L], s[2 * L:] + next_bias]
        sink = sink_ref[g:g + 1, :]
        m8 = functools.reduce(jnp.maximum, [sj.reshape(L // 8, 8, G * L).max(axis=0) for sj in s])
        m = jnp.maximum(sink, jnp.max(m8, axis=0, keepdims=True))
        p = jnp.concatenate([jnp.exp2(sj - m).astype(BF16) for sj in s], axis=0)
        vg = jnp.concatenate([v_all[j + i, g * hd:(g + 1) * hd, :] for i in range(3)], axis=1)
        acc = _dot(jnp.concatenate([vg, ones], axis=0), p)
        denom = acc[hd:hd + 1] + jnp.exp2(sink - m)
        out = acc[:hd] * (1.0 / denom)
        for pair in range(G // 2):
            tile = jnp.concatenate([out[:, (2 * pair) * L:(2 * pair + 1) * L],
                                    out[:, (2 * pair + 1) * L:(2 * pair + 2) * L]], axis=0)
            col = (G * g + 2 * pair) * hd
            o_ref[0, j * L:(j + 1) * L, col:col + L] = tile.T.astype(BF16)


def _attn_core(qt, k, vt, sink_rows, ffn_w_in, ffn_w_out, layer, *, nblk=8):
    B, nb, nq, L = qt.shape
    nk = k.shape[-1]
    S = nb * L
    nblk = min(nblk, nb)
    steps = nb // nblk
    prev = lambda b, c: (b, jnp.maximum(c * nblk - 1, 0), 0)
    cur = lambda b, c: (b, c, 0)
    nxt = lambda b, c: (b, jnp.minimum((c + 1) * nblk, nb - 1), 0)
    four = lambda f: (lambda b, c: f(b, c) + (0,))
    rep, w_args, w_in_specs, w_out_specs, w_out_shape = _ffn_cast_specs(
        ffn_w_in, ffn_w_out, layer, lambda b, c: b * steps + c, B * steps)
    return pl.pallas_call(
        functools.partial(_attn_core_kernel, nb=nb, nblk=nblk, cast_rep=rep),
        grid=(B, steps),
        in_specs=[pl.BlockSpec((1, nblk, nq, L), four(cur)),
                  pl.BlockSpec((1, L, nk), prev), pl.BlockSpec((1, nblk * L, nk), cur),
                  pl.BlockSpec((1, L, nk), nxt),
                  pl.BlockSpec((1, 1, nk, L), four(prev)), pl.BlockSpec((1, nblk, nk, L), four(cur)),
                  pl.BlockSpec((1, 1, nk, L), four(nxt)),
                  _resident(sink_rows.shape)] + w_in_specs,
        out_specs=[pl.BlockSpec((1, nblk * L, nq), cur)] + w_out_specs,
        out_shape=[jax.ShapeDtypeStruct((B, S, nq), BF16)] + w_out_shape,
        scratch_shapes=[pltpu.VMEM(((nblk + 2) * L, nk), BF16),
                        pltpu.VMEM((nblk + 2, nk, L), BF16)],
        compiler_params=_params(),
        name="attn_core",
    )(qt, k, k, k, vt, vt, vt, sink_rows, *w_args)


def _mlstm_mixer(x, norm_w, w_t, layer, b_gate, ffn_w_in, ffn_w_out, ffn_layer):
    bias_rows = jnp.broadcast_to(b_gate.astype(F32).reshape(-1, 1), (b_gate.shape[0], LANES))
    q, kt, v, og, rf = _mlstm_proj(x, norm_w, w_t, layer, bias_rows)
    hs, wi, wo = _mlstm_core(q, kt, v, rf, ffn_w_in, ffn_w_out, ffn_layer)
    return (hs, og), wi, wo


def _attn_mixer(x, pos_rows, inv_freq, norm_w, w_in, layer, sink, ffn_w_in, ffn_w_out, ffn_layer):
    G, L = ATTN_GROUP, LANES
    qt, k, vt = _attn_proj(x, norm_w, w_in, layer, pos_rows, inv_freq)
    sink_rows = jnp.repeat(LOG2E * sink.astype(F32).reshape(ATTN_KV_HEADS, G), L, axis=1)
    a, wi, wo = _attn_core(qt, k, vt, sink_rows, ffn_w_in, ffn_w_out, ffn_layer)
    return (a,), wi, wo


def kernel(x, positions, norm_mix_w, norm_ffn_w, norm_final_w, mlstm_w_in, mlstm_b_gate, mlstm_norm_w, mlstm_w_out, attn_w_in, attn_sink, attn_w_out, ffn_w_in, ffn_w_out):
    depth = norm_mix_w.shape[0]
    B, S = positions.shape
    half = ATTN_HEAD_DIM // 2
    inv_freq = (ROPE_THETA ** (-jnp.arange(half, dtype=F32) / half)).reshape(half, 1)
    pos_rows = positions.reshape(B, 1, S)
    mlstm_wo, attn_wo = mlstm_w_out.astype(BF16), attn_w_out.astype(BF16)
    mlstm_wt = jnp.swapaxes(mlstm_w_in, 1, 2)
    for i in range(depth):
        j = i // 2
        final_w = norm_final_w if i == depth - 1 else None
        if i % 2 == 0:
            (hs, og), wi, wo = _mlstm_mixer(x, norm_mix_w[i], mlstm_wt, j, mlstm_b_gate[j],
                                            ffn_w_in, ffn_w_out, i)
            x = _layer_tail("mlstm", (hs, og, mlstm_norm_w[j]), x, mlstm_wo, j, norm_ffn_w[i], wi, wo,
                            final_w)
        else:
            mixed, wi, wo = _attn_mixer(x, pos_rows, inv_freq, norm_mix_w[i], attn_w_in, j, attn_sink[j],
                                        ffn_w_in, ffn_w_out, i)
            x = _layer_tail("attn", mixed, x, attn_wo, j, norm_ffn_w[i], wi, wo, final_w)
    return x
```
